```python
import math
import jax, jax.numpy as jnp
from jax import lax
import numpy as np

D_MODEL = 2048
BATCH = 4
SEQ = 8192
DEPTH = 1

HEAD_DIM = 128
DIL_GROUPS = ((128, 1), (512, 4), (2048, 16))
DIL_HEADS_PER_GROUP = 8
DIL_N_HEADS = DIL_HEADS_PER_GROUP * len(DIL_GROUPS)
DIL_QKV = DIL_N_HEADS * HEAD_DIM
DIL_WIDTH = DIL_HEADS_PER_GROUP * HEAD_DIM
DIFF_N_HEADS = 6
DIFF_HEAD_DIM = 2 * HEAD_DIM
DIFF_WIDTH = DIFF_N_HEADS * DIFF_HEAD_DIM
Q_BLOCK = 128
EPS = 1e-6
NEG_INF = -1e30
IN_SIZES = (DIL_QKV, DIL_QKV, DIL_QKV, DIL_WIDTH,
            DIFF_WIDTH, DIFF_WIDTH, DIFF_WIDTH, DIFF_WIDTH,
            D_MODEL, D_MODEL)
N_IN = sum(IN_SIZES)
IN_SPLITS = [int(v) for v in np.cumsum(IN_SIZES)[:-1]]

kernel_name = "hybrid_dilated_diff_attn_gated_merge"


def rmsnorm(x, g):
    xf = x.astype(jnp.float32)
    y = xf * lax.rsqrt(jnp.mean(xf * xf, axis=-1, keepdims=True) + EPS)
    return y.astype(x.dtype) * g


def alibi_slopes(n):
    return jnp.asarray(2.0 ** (-8.0 * np.arange(1, n + 1) / n), dtype=jnp.float32)


def lambda_init_fn(layer):
    return 0.8 - 0.6 * math.exp(-0.3 * layer)


def dilated_group_attention(q, k, v, slopes, window, dilation):
    b, s, h, dh = q.shape
    r = dilation
    span = window // dilation
    L = s // r
    qb = span
    nb = -(-L // qb)
    lp = nb * qb
    qr, kr, vr = (t.reshape(b, L, r, h, dh) for t in (q, k, v))
    qp = jnp.pad(qr, ((0, 0), (0, lp - L), (0, 0), (0, 0), (0, 0))).reshape(b, nb, qb, r, h, dh)

    def band(t):
        tp = jnp.pad(t, ((0, 0), (qb, lp - L), (0, 0), (0, 0), (0, 0)))
        prev = tp[:, :lp].reshape(b, nb, qb, r, h, dh)
        cur = tp[:, qb:qb + lp].reshape(b, nb, qb, r, h, dh)
        return jnp.concatenate([prev, cur], axis=2)

    kb, vb = band(kr), band(vr)
    scores = jnp.einsum('bnqrhd,bnkrhd->bnrhqk', qp, kb).astype(jnp.float32) * (dh ** -0.5)
    delta = qb + jnp.arange(qb)[:, None] - jnp.arange(2 * qb)[None, :]
    key_u = (jnp.arange(nb)[:, None] - 1) * qb + jnp.arange(2 * qb)[None, :]
    valid = ((delta >= 0) & (delta <= span))[None] & (key_u >= 0)[:, None, :]
    bias = -slopes[:, None, None] * (r * delta).astype(jnp.float32)
    logits = jnp.where(valid[None, :, None, None], scores + bias[None, None, None], NEG_INF)
    lse = jax.nn.logsumexp(logits, axis=-1)
    p = jnp.exp(logits - lse[..., None])
    o = jnp.einsum('bnrhqk,bnkrhd->bnqrhd', p.astype(v.dtype), vb)
    o = o.reshape(b, lp, r, h, dh)[:, :L].reshape(b, s, h, dh)
    lse = lse.transpose(0, 1, 4, 2, 3).reshape(b, lp, r, h)[:, :L].reshape(b, s, h)
    return o, lse


def dilated_mixer(q, k, v, slopes):
    b, s, _ = q.shape
    ng = len(DIL_GROUPS)
    q, k, v = (t.reshape(b, s, ng, DIL_HEADS_PER_GROUP, HEAD_DIM) for t in (q, k, v))
    sl = slopes.reshape(ng, DIL_HEADS_PER_GROUP)
    outs, lses = [], []
    for g, (window, dilation) in enumerate(DIL_GROUPS):
        o, lse = dilated_group_attention(q[:, :, g], k[:, :, g], v[:, :, g], sl[g], window, dilation)
        outs.append(o)
        lses.append(lse)
    wgt = jax.nn.softmax(jnp.stack(lses, axis=0), axis=0)
    o = jnp.sum(wgt[..., None] * jnp.stack(outs, axis=0).astype(jnp.float32), axis=0)
    return o.reshape(b, s, DIL_WIDTH).astype(q.dtype)


def diff_attention(q, k, v, slopes, lam, g_subln, lambda_init):
    b, s, h, _, dh = q.shape
    nq = s // Q_BLOCK
    qblocks = q.reshape(b, nq, Q_BLOCK, h, 2, dh).transpose(1, 0, 2, 3, 4, 5)
    tk = jnp.arange(s)

    def one_block(args):
        qblk, n = args
        sc = jnp.einsum('bqhcd,bkhcd->bhcqk', qblk, k).astype(jnp.float32) * (dh ** -0.5)
        dist = n * Q_BLOCK + jnp.arange(Q_BLOCK)[:, None] - tk[None, :]
        logits = sc - slopes[None, :, None, None, None] * dist.astype(jnp.float32)[None, None, None]
        logits = jnp.where((dist >= 0)[None, None, None], logits, NEG_INF)
        p = jax.nn.softmax(logits, axis=-1)
        a = p[:, :, 0] - lam * p[:, :, 1]
        return jnp.einsum('bhqk,bkhe->bqhe', a.astype(v.dtype), v)

    o = lax.map(one_block, (qblocks, jnp.arange(nq)))
    o = o.transpose(1, 0, 2, 3, 4).reshape(b, s, h, 2 * dh)
    o = rmsnorm(o, g_subln) * (1.0 - lambda_init)
    return o.reshape(b, s, h * 2 * dh)


def setup_inputs(seed: int = 0) -> dict:
    key = jax.random.key(seed)
    ks = jax.random.split(key, 15)
    f32 = jnp.float32
    nrm = lambda k, shp, sc: jax.random.normal(k, shp, f32) * sc
    return {
        "x": nrm(ks[0], (BATCH, SEQ, D_MODEL), 1.0),
        "c": nrm(ks[1], (BATCH, D_MODEL), 1.0),
        "w_ada": nrm(ks[2], (DEPTH, D_MODEL, 3 * D_MODEL), 0.5 * D_MODEL ** -0.5),
        "b_ada": nrm(ks[3], (DEPTH, 3 * D_MODEL), 0.01),
        "g_norm": 1.0 + nrm(ks[4], (DEPTH, D_MODEL), 0.02),
        "w_in": nrm(ks[5], (DEPTH, D_MODEL, N_IN), D_MODEL ** -0.5),
        "w_o_dil": nrm(ks[6], (DEPTH, DIL_WIDTH, D_MODEL), DIL_WIDTH ** -0.5),
        "w_o_diff": nrm(ks[7], (DEPTH, DIFF_WIDTH, D_MODEL), DIFF_WIDTH ** -0.5),
        "w_out": nrm(ks[8], (DEPTH, D_MODEL, D_MODEL), D_MODEL ** -0.5),
        "lambda_q1": nrm(ks[9], (DEPTH, HEAD_DIM), 0.1),
        "lambda_k1": nrm(ks[10], (DEPTH, HEAD_DIM), 0.1),
        "lambda_q2": nrm(ks[11], (DEPTH, HEAD_DIM), 0.1),
        "lambda_k2": nrm(ks[12], (DEPTH, HEAD_DIM), 0.1),
        "g_subln": 1.0 + nrm(ks[13], (DEPTH, DIFF_HEAD_DIM), 0.02),
        "g_final": 1.0 + nrm(ks[14], (D_MODEL,), 0.02),
    }


def reference(x, c, w_ada, b_ada, g_norm, w_in, w_o_dil, w_o_diff, w_out,
              lambda_q1, lambda_k1, lambda_q2, lambda_k2, g_subln, g_final):
    b, s, _ = x.shape
    slopes_dil = alibi_slopes(DIL_N_HEADS)
    slopes_diff = alibi_slopes(DIFF_N_HEADS)
    for l in range(DEPTH):
        ada = jax.nn.silu(c) @ w_ada[l] + b_ada[l]
        shift, scale, gate = jnp.split(ada, 3, axis=-1)
        hn = rmsnorm(x, g_norm[l]) * (1.0 + scale[:, None]) + shift[:, None]
        proj = hn @ w_in[l]
        q_a, k_a, v_a, z_a, q_b, k_b, v_b, z_b, gl_a, gl_b = jnp.split(proj, IN_SPLITS, axis=-1)
        y_a = dilated_mixer(q_a, k_a, v_a, slopes_dil) * jax.nn.silu(z_a)
        lam_init = lambda_init_fn(l)
        lam = (jnp.exp(jnp.sum(lambda_q1[l] * lambda_k1[l]).astype(jnp.float32))
               - jnp.exp(jnp.sum(lambda_q2[l] * lambda_k2[l]).astype(jnp.float32)) + lam_init)
        y_b = diff_attention(q_b.reshape(b, s, DIFF_N_HEADS, 2, HEAD_DIM),
                             k_b.reshape(b, s, DIFF_N_HEADS, 2, HEAD_DIM),
                             v_b.reshape(b, s, DIFF_N_HEADS, DIFF_HEAD_DIM),
                             slopes_diff, lam, g_subln[l], lam_init).astype(x.dtype)
        y_b = y_b * jax.nn.silu(z_b)
        merged = jax.nn.sigmoid(gl_a) * (y_a @ w_o_dil[l]) + jax.nn.sigmoid(gl_b) * (y_b @ w_o_diff[l])
        x = x + gate[:, None] * (merged @ w_out[l])
    return rmsnorm(x, g_final)
```

```python
import functools
import math

import numpy as np
import jax
import jax.numpy as jnp
from jax import lax
from jax.experimental import pallas as pl
from jax.experimental.pallas import tpu as pltpu

F32 = jnp.float32
BF16 = jnp.bfloat16

D_MODEL = 2048
HEAD_DIM = 128
DIL_GROUPS = ((128, 1), (512, 4), (2048, 16))
DIL_HEADS_PER_GROUP = 8
DIL_N_HEADS = DIL_HEADS_PER_GROUP * len(DIL_GROUPS)
DIL_QKV = DIL_N_HEADS * HEAD_DIM
DIL_WIDTH = DIL_HEADS_PER_GROUP * HEAD_DIM
DIL_SPAN = 128
DIFF_N_HEADS = 6
DIFF_HEAD_DIM = 2 * HEAD_DIM
DIFF_WIDTH = DIFF_N_HEADS * DIFF_HEAD_DIM
EPS = 1e-6
NEG_INF = -1e30
IN_SIZES = (DIL_QKV, DIL_QKV, DIL_QKV, DIL_WIDTH,
            DIFF_WIDTH, DIFF_WIDTH, DIFF_WIDTH, DIFF_WIDTH, D_MODEL, D_MODEL)
N_IN = sum(IN_SIZES)
(OFF_QA, OFF_KA, OFF_VA, OFF_ZA, OFF_QB, OFF_KB, OFF_VB, OFF_ZB, OFF_GLA, OFF_GLB) = (
    int(v) for v in np.concatenate([[0], np.cumsum(IN_SIZES)[:-1]]))
SM_SCALE = HEAD_DIM ** -0.5
LAMBDA_INIT = 0.8 - 0.6 * math.exp(-0.3 * 0)

VMEM_LIMIT_BYTES = 56 * 1024 * 1024

PROJ_TM = 1024
PROJ_TN = 1024
PROJ_NORM_CHUNK = 32
DIFF_TQ = 512
OUT_TM = 256


def _alibi_slopes(n):
    return [float(2.0 ** (-8.0 * (i + 1) / n)) for i in range(n)]


def _dot_nt(a, b):
    return lax.dot_general(a, b, (((1,), (1,)), ((), ())), preferred_element_type=F32)


def _silu(z):
    return z / (1.0 + jnp.exp(-z))


def _sigmoid(z):
    return 1.0 / (1.0 + jnp.exp(-z))


def _ada_kernel(c_ref, w_ref, b_ref, o_ref):
    c = c_ref[...]
    o_ref[...] = jnp.dot(_silu(c), w_ref[...], preferred_element_type=F32,
                         precision=lax.Precision.HIGHEST) + b_ref[...]


def _ada(c_pad, w_ada, b_ada):
    rows, d = c_pad.shape
    n = w_ada.shape[1]
    tn = 512
    return pl.pallas_call(
        _ada_kernel,
        grid=(n // tn,),
        in_specs=[pl.BlockSpec((rows, d), lambda j: (0, 0)),
                  pl.BlockSpec((d, tn), lambda j: (0, j)),
                  pl.BlockSpec((1, tn), lambda j: (0, j))],
        out_specs=pl.BlockSpec((rows, tn), lambda j: (0, j)),
        out_shape=jax.ShapeDtypeStruct((rows, n), F32),
        name="ada",
    )(c_pad, w_ada, b_ada)


def _proj_kernel(x_ref, mod_ref, g_ref, w_ref, o_ref, hn_ref):
    @pl.when(pl.program_id(1) == 0)
    def _():
        shift = mod_ref[0, 0:1, :]
        mult = g_ref[...] * (1.0 + mod_ref[0, 1:2, :])

        def body(i, carry):
            r = pl.multiple_of(i * PROJ_NORM_CHUNK, PROJ_NORM_CHUNK)
            xx = x_ref[pl.ds(r, PROJ_NORM_CHUNK), :]
            ms = jnp.mean(xx * xx, axis=-1, keepdims=True)
            hn_ref[pl.ds(r, PROJ_NORM_CHUNK), :] = (
                xx * lax.rsqrt(ms + EPS) * mult + shift).astype(BF16)
            return carry

        lax.fori_loop(0, x_ref.shape[0] // PROJ_NORM_CHUNK, body, 0)

    o_ref[...] = jnp.dot(hn_ref[...], w_ref[...], preferred_element_type=F32).astype(BF16)


def _proj(x2d, mod, g_norm, w_in_bf16, seq):
    t, d = x2d.shape
    n = w_in_bf16.shape[1]
    tm, tn = PROJ_TM, PROJ_TN
    tiles_per_batch = seq // tm
    return pl.pallas_call(
        _proj_kernel,
        grid=(t // tm, n // tn),
        in_specs=[pl.BlockSpec((tm, d), lambda i, j: (i, 0)),
                  pl.BlockSpec((1, 3, d), lambda i, j: (i // tiles_per_batch, 0, 0)),
                  pl.BlockSpec((1, d), lambda i, j: (0, 0)),
                  pl.BlockSpec((d, tn), lambda i, j: (0, j))],
        out_specs=pl.BlockSpec((tm, tn), lambda i, j: (i, j)),
        out_shape=jax.ShapeDtypeStruct((t, n), BF16),
        scratch_shapes=[pltpu.VMEM((tm, d), BF16)],
        compiler_params=pltpu.CompilerParams(
            dimension_semantics=("parallel", "arbitrary"),
            vmem_limit_bytes=VMEM_LIMIT_BYTES),
        name="proj",
    )(x2d, mod, g_norm, w_in_bf16)


def _dil_kernel(q_ref, kp_ref, kc_ref, vp_ref, vc_ref, o_ref, lse_ref, *, coefs):
    has_prev = pl.program_id(2) > 0
    iq = lax.broadcasted_iota(jnp.int32, (DIL_SPAN, DIL_SPAN), 0)
    jk = lax.broadcasted_iota(jnp.int32, (DIL_SPAN, DIL_SPAN), 1)
    dist_cur = (iq - jk).astype(F32)
    dist_prev = dist_cur + float(DIL_SPAN)
    cur_ok = jk <= iq
    prev_ok = jnp.logical_and(jk >= iq, has_prev)
    lane = lax.broadcasted_iota(jnp.int32, (DIL_SPAN, HEAD_DIM), 1)
    lse_tile = jnp.zeros((DIL_SPAN, HEAD_DIM), F32)
    for h in range(DIL_HEADS_PER_GROUP):
        sl = slice(h * HEAD_DIM, (h + 1) * HEAD_DIM)
        q = q_ref[0, :, sl]
        s_c = _dot_nt(q, kc_ref[0, :, sl]) * SM_SCALE - coefs[h] * dist_cur
        s_p = _dot_nt(q, kp_ref[0, :, sl]) * SM_SCALE - coefs[h] * dist_prev
        s_c = jnp.where(cur_ok, s_c, NEG_INF)
        s_p = jnp.where(prev_ok, s_p, NEG_INF)
        m = jnp.maximum(jnp.max(s_c, axis=-1, keepdims=True), jnp.max(s_p, axis=-1, keepdims=True))
        p_c = jnp.exp(s_c - m)
        p_p = jnp.exp(s_p - m)
        l = jnp.sum(p_c, axis=-1, keepdims=True) + jnp.sum(p_p, axis=-1, keepdims=True)
        o = (jnp.dot(p_c.astype(BF16), vc_ref[0, :, sl], preferred_element_type=F32)
             + jnp.dot(p_p.astype(BF16), vp_ref[0, :, sl], preferred_element_type=F32))
        o_ref[0, :, sl] = (o / l).astype(BF16)
        lse_tile = jnp.where(lane == h, m + jnp.log(l), lse_tile)
    lse_ref[0] = lse_tile


def _dil_group(proj3, g, batch, seq):
    _, r = DIL_GROUPS[g]
    length = seq // r
    nb = length // DIL_SPAN
    pv = proj3.reshape(batch, length, r * N_IN)
    wblk = N_IN // DIL_WIDTH
    slopes = _alibi_slopes(DIL_N_HEADS)[g * DIL_HEADS_PER_GROUP:(g + 1) * DIL_HEADS_PER_GROUP]
    coefs = tuple(s * r for s in slopes)
    blk = (1, DIL_SPAN, DIL_WIDTH)
    qcol, kcol, vcol = (OFF_QA // DIL_WIDTH + g, OFF_KA // DIL_WIDTH + g, OFF_VA // DIL_WIDTH + g)

    def cur(col):
        return pl.BlockSpec(blk, lambda b, p, n: (b, n, p * wblk + col))

    def prev(col):
        return pl.BlockSpec(blk, lambda b, p, n: (b, jnp.maximum(n - 1, 0), p * wblk + col))

    o, lse = pl.pallas_call(
        functools.partial(_dil_kernel, coefs=coefs),
        grid=(batch, r, nb),
        in_specs=[cur(qcol), prev(kcol), cur(kcol), prev(vcol), cur(vcol)],
        out_specs=[pl.BlockSpec(blk, lambda b, p, n: (b, n, p)),
                   pl.BlockSpec((1, DIL_SPAN, HEAD_DIM), lambda b, p, n: (b, n, p))],
        out_shape=[jax.ShapeDtypeStruct((batch, length, r * DIL_WIDTH), BF16),
                   jax.ShapeDtypeStruct((batch, length, r * HEAD_DIM), F32)],
        compiler_params=pltpu.CompilerParams(
            dimension_semantics=("parallel", "parallel", "arbitrary")),
        name=f"dil{g}",
    )(pv, pv, pv, pv, pv)
    return o.reshape(batch * seq, DIL_WIDTH), lse.reshape(batch * seq, HEAD_DIM)


def _diff_kernel(qi_ref, kj_ref, q_ref, k_ref, v_ref, lamv_ref, gsub_ref, o_ref,
                 m_ref, l_ref, acc_ref, *, slopes):
    h = pl.program_id(1)
    t = pl.program_id(2)
    qi = qi_ref[t]
    kj = kj_ref[t]
    tq = q_ref.shape[0]
    tk = k_ref.shape[0]

    slope = jnp.float32(slopes[0])
    for i in range(1, len(slopes)):
        slope = jnp.where(h == i, jnp.float32(slopes[i]), slope)

    @pl.when(kj == 0)
    def _():
        m_ref[...] = jnp.full(m_ref.shape, NEG_INF, F32)
        l_ref[...] = jnp.zeros(l_ref.shape, F32)
        acc_ref[...] = jnp.zeros(acc_ref.shape, F32)

    col = lax.broadcasted_iota(jnp.int32, (1, tk), 1) + (kj - qi) * tk
    colbias = slope * col.astype(F32)

    def step(masked):
        v = v_ref[...]
        for c in range(2):
            sl = slice(c * HEAD_DIM, (c + 1) * HEAD_DIM)
            s = _dot_nt(q_ref[:, sl], k_ref[:, sl]) * SM_SCALE + colbias
            if masked:
                ii = lax.broadcasted_iota(jnp.int32, (tq, tk), 0)
                jj = lax.broadcasted_iota(jnp.int32, (tq, tk), 1)
                s = jnp.where(jj <= ii, s, NEG_INF)
            m_old = m_ref[c]
            m_new = jnp.maximum(m_old, jnp.max(s, axis=-1, keepdims=True))
            alpha = jnp.exp(m_old - m_new)
            p = jnp.exp(s - m_new)
            l_ref[c] = alpha * l_ref[c] + jnp.sum(p, axis=-1, keepdims=True)
            acc_ref[c] = alpha * acc_ref[c] + jnp.dot(p.astype(BF16), v, preferred_element_type=F32)
            m_ref[c] = m_new

    @pl.when(kj < qi)
    def _():
        step(False)

    @pl.when(kj == qi)
    def _():
        step(True)
        lamv = lamv_ref[...]
        lam = (jnp.exp(jnp.sum(lamv[0:1] * lamv[1:2], axis=-1, keepdims=True))
               - jnp.exp(jnp.sum(lamv[2:3] * lamv[3:4], axis=-1, keepdims=True)) + LAMBDA_INIT)
        o = acc_ref[0] / l_ref[0] - lam * (acc_ref[1] / l_ref[1])
        ms = jnp.mean(o * o, axis=-1, keepdims=True)
        o = o * lax.rsqrt(ms + EPS) * gsub_ref[...] * (1.0 - LAMBDA_INIT)
        o_ref[...] = o.astype(o_ref.dtype)


def _diff(proj2d, lamv, g_subln, batch, seq):
    tq = DIFF_TQ
    nq = seq // tq
    qi_tab = np.array([q for q in range(nq) for _ in range(q + 1)], np.int32)
    kj_tab = np.array([k for q in range(nq) for k in range(q + 1)], np.int32)
    wq, wk, wv = OFF_QB // DIFF_HEAD_DIM, OFF_KB // DIFF_HEAD_DIM, OFF_VB // DIFF_HEAD_DIM
    blk = (tq, DIFF_HEAD_DIM)
    grid_spec = pltpu.PrefetchScalarGridSpec(
        num_scalar_prefetch=2,
        grid=(batch, DIFF_N_HEADS, len(qi_tab)),
        in_specs=[
            pl.BlockSpec(blk, lambda b, h, t, qi, kj: (b * nq + qi[t], wq + h)),
            pl.BlockSpec(blk, lambda b, h, t, qi, kj: (b * nq + kj[t], wk + h)),
            pl.BlockSpec(blk, lambda b, h, t, qi, kj: (b * nq + kj[t], wv + h)),
            pl.BlockSpec((4, HEAD_DIM), lambda b, h, t, qi, kj: (0, 0)),
            pl.BlockSpec((1, DIFF_HEAD_DIM), lambda b, h, t, qi, kj: (0, 0)),
        ],
        out_specs=pl.BlockSpec(blk, lambda b, h, t, qi, kj: (b * nq + qi[t], h)),
        scratch_shapes=[pltpu.VMEM((2, tq, 1), F32), pltpu.VMEM((2, tq, 1), F32),
                        pltpu.VMEM((2, tq, DIFF_HEAD_DIM), F32)],
    )
    return pl.pallas_call(
        functools.partial(_diff_kernel, slopes=tuple(_alibi_slopes(DIFF_N_HEADS))),
        grid_spec=grid_spec,
        out_shape=jax.ShapeDtypeStruct((batch * seq, DIFF_WIDTH), BF16),
        compiler_params=pltpu.CompilerParams(
            dimension_semantics=("parallel", "parallel", "arbitrary"),
            vmem_limit_bytes=VMEM_LIMIT_BYTES),
        name="diff",
    )(jnp.asarray(qi_tab), jnp.asarray(kj_tab), proj2d, proj2d, proj2d, lamv, g_subln)


def _out_kernel(o0_ref, o1_ref, o2_ref, l0_ref, l1_ref, l2_ref, za_ref, yb_ref,
                zb0_ref, zb1_ref, zb2_ref, gla_ref, glb_ref, x_ref, mod_ref, gf_ref,
                wd_ref, wf_ref, wo_ref, out_ref, ya_ref, ybz_ref):
    l0, l1, l2 = l0_ref[...], l1_ref[...], l2_ref[...]
    mx = jnp.maximum(jnp.maximum(l0, l1), l2)
    e0, e1, e2 = jnp.exp(l0 - mx), jnp.exp(l1 - mx), jnp.exp(l2 - mx)
    inv = 1.0 / (e0 + e1 + e2)
    w0, w1, w2 = e0 * inv, e1 * inv, e2 * inv
    for h in range(DIL_HEADS_PER_GROUP):
        sl = slice(h * HEAD_DIM, (h + 1) * HEAD_DIM)
        om = (w0[:, h:h + 1] * o0_ref[:, sl].astype(F32)
              + w1[:, h:h + 1] * o1_ref[:, sl].astype(F32)
              + w2[:, h:h + 1] * o2_ref[:, sl].astype(F32))
        ya_ref[:, sl] = (om * _silu(za_ref[:, sl].astype(F32))).astype(BF16)
    zb_refs = (zb0_ref, zb1_ref, zb2_ref)
    wz = zb0_ref.shape[1]
    for i in range(3):
        sl = slice(i * wz, (i + 1) * wz)
        ybz_ref[:, sl] = (yb_ref[:, sl].astype(F32) * _silu(zb_refs[i][...].astype(F32))).astype(BF16)
    pa = jnp.dot(ya_ref[...], wd_ref[...], preferred_element_type=F32)
    pb = jnp.dot(ybz_ref[...], wf_ref[...], preferred_element_type=F32)
    merged = (_sigmoid(gla_ref[...].astype(F32)) * pa + _sigmoid(glb_ref[...].astype(F32)) * pb)
    delta = jnp.dot(merged.astype(BF16), wo_ref[...], preferred_element_type=F32)
    y = x_ref[...] + mod_ref[0, 2:3, :] * delta
    ms = jnp.mean(y * y, axis=-1, keepdims=True)
    out_ref[...] = y * lax.rsqrt(ms + EPS) * gf_ref[...]


def _const_spec(shape):
    return pl.BlockSpec(shape, lambda i: (0,) * len(shape), pipeline_mode=pl.Buffered(1))


def _out(o_list, lse_list, proj2d, yb, x2d, mod, g_final, wd, wf, wo, seq):
    t, d = x2d.shape
    tm = OUT_TM
    tiles_per_batch = seq // tm
    wz = 512
    zb_blk = OFF_ZB // wz
    row = lambda w: pl.BlockSpec((tm, w), lambda i: (i, 0))
    in_specs = (
        [row(DIL_WIDTH)] * 3 + [row(HEAD_DIM)] * 3
        + [pl.BlockSpec((tm, DIL_WIDTH), lambda i: (i, OFF_ZA // DIL_WIDTH)),
           row(DIFF_WIDTH)]
        + [pl.BlockSpec((tm, wz), functools.partial(lambda i, k: (i, zb_blk + k), k=k))
           for k in range(3)]
        + [pl.BlockSpec((tm, d), lambda i: (i, OFF_GLA // D_MODEL)),
           pl.BlockSpec((tm, d), lambda i: (i, OFF_GLB // D_MODEL)),
           row(d),
           pl.BlockSpec((1, 3, d), lambda i: (i // tiles_per_batch, 0, 0)),
           _const_spec((1, d)),
           _const_spec(wd.shape), _const_spec(wf.shape), _const_spec(wo.shape)])
    return pl.pallas_call(
        _out_kernel,
        grid=(t // tm,),
        in_specs=in_specs,
        out_specs=row(d),
        out_shape=jax.ShapeDtypeStruct((t, d), F32),
        scratch_shapes=[pltpu.VMEM((tm, DIL_WIDTH), BF16), pltpu.VMEM((tm, DIFF_WIDTH), BF16)],
        compiler_params=pltpu.CompilerParams(
            dimension_semantics=("parallel",),
            vmem_limit_bytes=VMEM_LIMIT_BYTES),
        name="out",
    )(*o_list, *lse_list, proj2d, yb, proj2d, proj2d, proj2d, proj2d, proj2d,
      x2d, mod, g_final, wd, wf, wo)


def kernel(x, c, w_ada, b_ada, g_norm, w_in, w_o_dil, w_o_diff, w_out,
           lambda_q1, lambda_k1, lambda_q2, lambda_k2, g_subln, g_final):
    batch, seq, d = x.shape
    assert w_ada.shape[0] == 1 and d == D_MODEL and w_in.shape[2] == N_IN
    assert seq % (DIL_SPAN * DIL_GROUPS[-1][1]) == 0 and seq % PROJ_TM == 0
    rows = -(-batch // 8) * 8
    c_pad = jnp.pad(c, ((0, rows - batch), (0, 0)))
    x2d = x.reshape(batch * seq, d)
    ada = _ada(c_pad, w_ada[0], b_ada[0].reshape(1, 3 * d))[:batch]
    mod = ada.reshape(batch, 3, d)
    proj2d = _proj(x2d, mod, g_norm[0].reshape(1, d), w_in[0].astype(BF16), seq)
    proj3 = proj2d.reshape(batch, seq, N_IN)
    dil = [_dil_group(proj3, g, batch, seq) for g in range(len(DIL_GROUPS))]
    lamv = jnp.stack([lambda_q1[0], lambda_k1[0], lambda_q2[0], lambda_k2[0]])
    yb = _diff(proj2d, lamv, g_subln[0].reshape(1, DIFF_HEAD_DIM), batch, seq)
    out = _out([o for o, _ in dil], [s for _, s in dil], proj2d, yb, x2d, mod,
               g_final.reshape(1, d), w_o_dil[0].astype(BF16), w_o_diff[0].astype(BF16),
               w_out[0].astype(BF16), seq)
    return out.reshape(batch, seq, d)
```

```python
import functools
import math

import numpy as np
import jax
import jax.numpy as jnp
from jax import lax
from jax.experimental import pallas as pl
from jax.experimental.pallas import tpu as pltpu

F32 = jnp.float32
BF16 = jnp.bfloat16

D_MODEL = 2048
HEAD_DIM = 128
LANES = 128
DIL_GROUPS = ((128, 1), (512, 4), (2048, 16))
DIL_HEADS_PER_GROUP = 8
DIL_N_HEADS = DIL_HEADS_PER_GROUP * len(DIL_GROUPS)
DIL_QKV = DIL_N_HEADS * HEAD_DIM
DIL_WIDTH = DIL_HEADS_PER_GROUP * HEAD_DIM
DIL_SPAN = 128
DIFF_N_HEADS = 6
DIFF_HEAD_DIM = 2 * HEAD_DIM
DIFF_WIDTH = DIFF_N_HEADS * DIFF_HEAD_DIM
EPS = 1e-6
NEG_INF = -1e30
IN_SIZES = (DIL_QKV, DIL_QKV, DIL_QKV, DIL_WIDTH,
            DIFF_WIDTH, DIFF_WIDTH, DIFF_WIDTH, DIFF_WIDTH, D_MODEL, D_MODEL)
N_IN = sum(IN_SIZES)
(OFF_QA, OFF_KA, OFF_VA, OFF_ZA, OFF_QB, OFF_KB, OFF_VB, OFF_ZB, OFF_GLA, OFF_GLB) = (
    int(v) for v in np.concatenate([[0], np.cumsum(IN_SIZES)[:-1]]))
LOG2E = 1.4426950408889634
Q_PRESCALE = HEAD_DIM ** -0.5 * LOG2E
LAMBDA_INIT = 0.8 - 0.6 * math.exp(-0.3 * 0)

VMEM_LIMIT_BYTES = 56 * 1024 * 1024

PROJ_TM = 1024
PROJ_TN = 1024
PROJ_NORM_CHUNK = 32
DIL_TT = DIL_SPAN * DIL_GROUPS[-1][1]
DIL_UNITS = DIL_TT // DIL_SPAN
DIL_STAGE_ROWS = 256
DIL_MERGE_CHUNK = 64
DIFF_T = 512
OUT_TM = 256


def _alibi_slopes(n):
    return [float(2.0 ** (-8.0 * (i + 1) / n)) for i in range(n)]


def _dot_nt(a, b):
    return lax.dot_general(a, b, (((1,), (1,)), ((), ())), preferred_element_type=F32)


def _silu(z):
    return z / (1.0 + jnp.exp(-z))


def _sigmoid(z):
    return 1.0 / (1.0 + jnp.exp(-z))


def _select_by_index(idx, values):
    out = jnp.float32(values[0])
    for i in range(1, len(values)):
        out = jnp.where(idx == i, jnp.float32(values[i]), out)
    return out


def _ada_kernel(c_ref, w_ref, b_ref, o_ref):
    c = c_ref[...]
    o_ref[...] = jnp.dot(_silu(c), w_ref[...], preferred_element_type=F32,
                         precision=lax.Precision.HIGHEST) + b_ref[...]


def _ada(c_pad, w_ada, b_ada):
    rows, d = c_pad.shape
    n = w_ada.shape[1]
    tn = 512
    return pl.pallas_call(
        _ada_kernel,
        grid=(n // tn,),
        in_specs=[pl.BlockSpec((rows, d), lambda j: (0, 0)),
                  pl.BlockSpec((d, tn), lambda j: (0, j)),
                  pl.BlockSpec((1, tn), lambda j: (0, j))],
        out_specs=pl.BlockSpec((rows, tn), lambda j: (0, j)),
        out_shape=jax.ShapeDtypeStruct((rows, n), F32),
        name="ada",
    )(c_pad, w_ada, b_ada)


def _proj_kernel(x_ref, mod_ref, g_ref, w_ref, cs_ref, o_ref, hn_ref):
    @pl.when(pl.program_id(1) == 0)
    def _():
        shift = mod_ref[0, 0:1, :]
        mult = g_ref[...] * (1.0 + mod_ref[0, 1:2, :])

        def body(i, carry):
            r = pl.multiple_of(i * PROJ_NORM_CHUNK, PROJ_NORM_CHUNK)
            xx = x_ref[pl.ds(r, PROJ_NORM_CHUNK), :]
            ms = jnp.mean(xx * xx, axis=-1, keepdims=True)
            hn_ref[pl.ds(r, PROJ_NORM_CHUNK), :] = (
                xx * lax.rsqrt(ms + EPS) * mult + shift).astype(BF16)
            return carry

        lax.fori_loop(0, x_ref.shape[0] // PROJ_NORM_CHUNK, body, 0)

    acc = jnp.dot(hn_ref[...], w_ref[...], preferred_element_type=F32)
    o_ref[...] = (acc * cs_ref[...]).astype(BF16)


def _proj(x2d, mod, g_norm, w_in_bf16, col_scale, seq):
    t, d = x2d.shape
    n = w_in_bf16.shape[1]
    tm, tn = PROJ_TM, PROJ_TN
    tiles_per_batch = seq // tm
    return pl.pallas_call(
        _proj_kernel,
        grid=(t // tm, n // tn),
        in_specs=[pl.BlockSpec((tm, d), lambda i, j: (i, 0)),
                  pl.BlockSpec((1, 3, d), lambda i, j: (i // tiles_per_batch, 0, 0)),
                  pl.BlockSpec((1, d), lambda i, j: (0, 0)),
                  pl.BlockSpec((d, tn), lambda i, j: (0, j)),
                  pl.BlockSpec((1, tn), lambda i, j: (0, j))],
        out_specs=pl.BlockSpec((tm, tn), lambda i, j: (i, j)),
        out_shape=jax.ShapeDtypeStruct((t, n), BF16),
        scratch_shapes=[pltpu.VMEM((tm, d), BF16)],
        compiler_params=pltpu.CompilerParams(
            dimension_semantics=("parallel", "arbitrary"),
            vmem_limit_bytes=VMEM_LIMIT_BYTES),
        name="proj",
    )(x2d, mod, g_norm, w_in_bf16, col_scale)


def _gather_residues(src_ref, src32, dst_ref, r, n_u, dst_pitch, dst_off):
    n32 = n_u * r // 2
    for c0 in range(0, n32, DIL_STAGE_ROWS):
        src32[c0:c0 + DIL_STAGE_ROWS, :] = pltpu.bitcast(
            src_ref[2 * c0:2 * (c0 + DIL_STAGE_ROWS), :], jnp.uint32)
    for p in range(r):
        for c0 in range(0, n_u, DIL_SPAN):
            w = src32[pl.ds(p // 2 + (r // 2) * c0, DIL_SPAN, stride=r // 2), :]
            w = (w << 16) if p % 2 == 0 else (w & jnp.uint32(0xFFFF0000))
            row = p * dst_pitch + dst_off + c0
            dst_ref[row:row + DIL_SPAN, :] = lax.bitcast_convert_type(w, F32).astype(BF16)


def _dil_kernel(*refs):
    ins, y_ref = refs[:10], refs[10]
    qd, w32, bm_ref = refs[11:14]
    kd, vd = refs[14:17], refs[17:20]
    o_scr, lse_scr = refs[20:23], refs[23:26]
    z_ref = ins[9]
    h = pl.program_id(1)
    first_tile = pl.program_id(2) == 0

    iq = lax.broadcasted_iota(jnp.int32, (DIL_SPAN, 2 * DIL_SPAN), 0)
    jk = lax.broadcasted_iota(jnp.int32, (DIL_SPAN, 2 * DIL_SPAN), 1)
    dist = (iq - jk + DIL_SPAN).astype(F32)
    valid = jnp.logical_and(jk >= iq, jk <= iq + DIL_SPAN)
    valid_first = jnp.logical_and(valid, jk >= DIL_SPAN)

    slopes = _alibi_slopes(DIL_N_HEADS)
    for g, (_, r) in enumerate(DIL_GROUPS):
        q_ref, kc_ref, vc_ref = ins[3 * g:3 * g + 3]
        kd_g, vd_g = kd[g], vd[g]
        nb = DIL_UNITS // r
        kpitch = (nb + 1) * DIL_SPAN
        coef = _select_by_index(
            h, [slopes[g * DIL_HEADS_PER_GROUP + i] * r * LOG2E for i in range(DIL_HEADS_PER_GROUP)])
        bias = -coef * dist
        bm_ref[0] = jnp.where(valid, bias, NEG_INF)
        bm_ref[1] = jnp.where(valid_first, bias, NEG_INF)

        @pl.when(first_tile)
        def _(r=r, kpitch=kpitch, kd_g=kd_g, vd_g=vd_g):
            for p in range(r):
                kd_g[p * kpitch:p * kpitch + DIL_SPAN, :] = jnp.zeros((DIL_SPAN, LANES), BF16)
                vd_g[p * kpitch:p * kpitch + DIL_SPAN, :] = jnp.zeros((DIL_SPAN, LANES), BF16)

        if r == 1:
            q_src = q_ref
            kd_g[DIL_SPAN:DIL_SPAN + DIL_TT, :] = kc_ref[...]
            vd_g[DIL_SPAN:DIL_SPAN + DIL_TT, :] = vc_ref[...]
        else:
            q_src = qd
            _gather_residues(q_ref, w32, qd, r, nb * DIL_SPAN, nb * DIL_SPAN, 0)
            _gather_residues(kc_ref, w32, kd_g, r, nb * DIL_SPAN, kpitch, DIL_SPAN)
            _gather_residues(vc_ref, w32, vd_g, r, nb * DIL_SPAN, kpitch, DIL_SPAN)

        for p in range(r):
            for ub in range(nb):
                rq = (p * nb + ub) * DIL_SPAN
                rk = p * kpitch + ub * DIL_SPAN
                bm = bm_ref[jnp.where(first_tile, 1, 0)] if ub == 0 else bm_ref[0]
                s = _dot_nt(q_src[rq:rq + DIL_SPAN, :], kd_g[rk:rk + 2 * DIL_SPAN, :]) + bm
                m = jnp.max(s, axis=-1, keepdims=True)
                e = jnp.exp2(s - m)
                l = jnp.sum(e, axis=-1, keepdims=True)
                o = jnp.dot(e.astype(BF16), vd_g[rk:rk + 2 * DIL_SPAN, :],
                            preferred_element_type=F32) * (1.0 / l)
                lse2 = jnp.broadcast_to(m + jnp.log(l) * LOG2E, (DIL_SPAN, LANES))
                row0 = ub * (DIL_SPAN * r) + p
                rows = pl.ds(row0, DIL_SPAN, stride=r) if r > 1 else pl.ds(row0, DIL_SPAN)
                o_scr[g][rows, :] = o
                lse_scr[g][rows, :] = lse2

        for p in range(r):
            last = p * kpitch + nb * DIL_SPAN
            kd_g[p * kpitch:p * kpitch + DIL_SPAN, :] = kd_g[last:last + DIL_SPAN, :]
            vd_g[p * kpitch:p * kpitch + DIL_SPAN, :] = vd_g[last:last + DIL_SPAN, :]

    def merge(i, carry):
        rows = pl.ds(pl.multiple_of(i * DIL_MERGE_CHUNK, DIL_MERGE_CHUNK), DIL_MERGE_CHUNK)
        l0, l1, l2 = lse_scr[0][rows, :], lse_scr[1][rows, :], lse_scr[2][rows, :]
        mx = jnp.maximum(jnp.maximum(l0, l1), l2)
        e0, e1, e2 = jnp.exp2(l0 - mx), jnp.exp2(l1 - mx), jnp.exp2(l2 - mx)
        om = (e0 * o_scr[0][rows, :] + e1 * o_scr[1][rows, :] + e2 * o_scr[2][rows, :]) / (e0 + e1 + e2)
        y_ref[rows, :] = (om * _silu(z_ref[rows, :].astype(F32))).astype(BF16)
        return carry

    lax.fori_loop(0, DIL_TT // DIL_MERGE_CHUNK, merge, 0)


def _dil(proj2d, batch, seq):
    tt = DIL_TT
    tiles = seq // tt

    def cur(col):
        return pl.BlockSpec((tt, LANES), lambda b, h, t: (b * tiles + t, col + h))

    in_specs = []
    for g in range(len(DIL_GROUPS)):
        in_specs += [cur((off + g * DIL_WIDTH) // LANES) for off in (OFF_QA, OFF_KA, OFF_VA)]
    in_specs.append(cur(OFF_ZA // LANES))
    kv_scratch = [pltpu.VMEM((tt + DIL_SPAN * r, LANES), BF16) for _, r in DIL_GROUPS]
    return pl.pallas_call(
        _dil_kernel,
        grid=(batch, DIL_HEADS_PER_GROUP, tiles),
        in_specs=in_specs,
        out_specs=pl.BlockSpec((tt, LANES), lambda b, h, t: (b * tiles + t, h)),
        out_shape=jax.ShapeDtypeStruct((batch * seq, DIL_WIDTH), BF16),
        scratch_shapes=(
            [pltpu.VMEM((tt, LANES), BF16), pltpu.VMEM((tt // 2, LANES), jnp.uint32),
             pltpu.VMEM((2, DIL_SPAN, 2 * DIL_SPAN), F32)]
            + kv_scratch + kv_scratch + [pltpu.VMEM((tt, LANES), F32)] * 6),
        compiler_params=pltpu.CompilerParams(
            dimension_semantics=("parallel", "parallel", "arbitrary"),
            vmem_limit_bytes=VMEM_LIMIT_BYTES),
        name="dil",
    )(*([proj2d] * 10))


def _diff_kernel(q_ref, k_ref, v_ref, lamv_ref, gsub_ref, o_ref,
                 vt_ref, st_ref, pt_ref, cb_ref, m_ref, l_ref, a_ref, acc_ref, *, slopes2):
    h = pl.program_id(1)
    qi = pl.program_id(2)
    t = q_ref.shape[0]
    nk = vt_ref.shape[0]
    slope2 = _select_by_index(h, slopes2)

    @pl.when(qi == 0)
    def _():
        for j in range(nk):
            vt_ref[j] = v_ref[j * t:(j + 1) * t, :].astype(F32).T.astype(BF16)

    cb_ref[...] = slope2 * lax.broadcasted_iota(jnp.int32, (t, LANES), 0).astype(F32)
    m_ref[...] = jnp.full(m_ref.shape, NEG_INF, F32)
    l_ref[...] = jnp.zeros(l_ref.shape, F32)
    acc_ref[...] = jnp.zeros(acc_ref.shape, F32)

    def scores(kj, slot):
        k0 = pl.multiple_of(kj * t, t)
        for c in range(2):
            sl = slice(c * HEAD_DIM, (c + 1) * HEAD_DIM)
            st_ref[slot, c] = _dot_nt(k_ref[pl.ds(k0, t), sl], q_ref[:, sl])

    def softmax_pv(kj, slot, masked):
        cblk = slope2 * ((kj - qi) * t).astype(F32)
        for c in range(2):
            for cg in range(t // LANES):
                cols = slice(cg * LANES, (cg + 1) * LANES)
                x = st_ref[slot, c, :, cols] + cb_ref[...]
                if masked:
                    key = lax.broadcasted_iota(jnp.int32, (t, LANES), 0)
                    qry = lax.broadcasted_iota(jnp.int32, (t, LANES), 1) + cg * LANES
                    x = jnp.where(key <= qry, x, NEG_INF)
                m_old = m_ref[c, :, cols]
                m_new = jnp.maximum(m_old, jnp.max(x, axis=0, keepdims=True) + cblk)
                alpha = jnp.exp2(m_old - m_new)
                p = jnp.exp2(x - (m_new - cblk))
                l_ref[c, :, cols] = alpha * l_ref[c, :, cols] + jnp.sum(p, axis=0, keepdims=True)
                m_ref[c, :, cols] = m_new
                a_ref[slot, c, :, cols] = alpha
                pt_ref[slot, c, :, cols] = p.astype(BF16)
        vt = vt_ref[kj]
        for c in range(2):
            acc_ref[c] = (acc_ref[c] * a_ref[slot, c]
                          + jnp.dot(vt, pt_ref[slot, c], preferred_element_type=F32))

    scores(0, 0)

    def pair(i, carry):
        kj = 2 * i
        scores(kj + 1, 1)
        softmax_pv(kj, 0, False)
        scores(kj + 2, 0)
        softmax_pv(kj + 1, 1, False)
        return carry

    lax.fori_loop(0, qi // 2, pair, 0)
    odd = qi % 2 == 1

    @pl.when(odd)
    def _():
        scores(qi, 1)
        softmax_pv(qi - 1, 0, False)
        softmax_pv(qi, 1, True)

    @pl.when(jnp.logical_not(odd))
    def _():
        softmax_pv(qi, 0, True)

    lamv = lamv_ref[...]
    lam = (jnp.exp(jnp.sum(lamv[0:1] * lamv[1:2], axis=-1, keepdims=True))
           - jnp.exp(jnp.sum(lamv[2:3] * lamv[3:4], axis=-1, keepdims=True)) + LAMBDA_INIT)
    ot = acc_ref[0] * (1.0 / l_ref[0]) - lam * (acc_ref[1] * (1.0 / l_ref[1]))
    ms = jnp.mean(ot * ot, axis=0, keepdims=True)
    ot = ot * lax.rsqrt(ms + EPS)
    o_ref[...] = (ot.T * gsub_ref[...] * (1.0 - LAMBDA_INIT)).astype(o_ref.dtype)


def _diff(proj2d, lamv, g_subln, batch, seq):
    t = DIFF_T
    nq = seq // t
    wq, wk, wv = OFF_QB // DIFF_HEAD_DIM, OFF_KB // DIFF_HEAD_DIM, OFF_VB // DIFF_HEAD_DIM
    slopes2 = tuple(s * LOG2E for s in _alibi_slopes(DIFF_N_HEADS))
    return pl.pallas_call(
        functools.partial(_diff_kernel, slopes2=slopes2),
        grid=(batch, DIFF_N_HEADS, nq),
        in_specs=[
            pl.BlockSpec((t, DIFF_HEAD_DIM), lambda b, h, q: (b * nq + q, wq + h)),
            pl.BlockSpec((seq, DIFF_HEAD_DIM), lambda b, h, q: (b, wk + h)),
            pl.BlockSpec((seq, DIFF_HEAD_DIM), lambda b, h, q: (b, wv + h)),
            pl.BlockSpec((4, HEAD_DIM), lambda b, h, q: (0, 0)),
            pl.BlockSpec((1, DIFF_HEAD_DIM), lambda b, h, q: (0, 0)),
        ],
        out_specs=pl.BlockSpec((t, DIFF_HEAD_DIM), lambda b, h, q: (b * nq + q, h)),
        out_shape=jax.ShapeDtypeStruct((batch * seq, DIFF_WIDTH), BF16),
        scratch_shapes=[
            pltpu.VMEM((nq, DIFF_HEAD_DIM, t), BF16),
            pltpu.VMEM((2, 2, t, t), F32),
            pltpu.VMEM((2, 2, t, t), BF16),
            pltpu.VMEM((t, LANES), F32),
            pltpu.VMEM((2, 1, t), F32),
            pltpu.VMEM((2, 1, t), F32),
            pltpu.VMEM((2, 2, 1, t), F32),
            pltpu.VMEM((2, DIFF_HEAD_DIM, t), F32),
        ],
        compiler_params=pltpu.CompilerParams(
            dimension_semantics=("parallel", "parallel", "arbitrary"),
            vmem_limit_bytes=VMEM_LIMIT_BYTES),
        name="diff",
    )(proj2d, proj2d, proj2d, lamv, g_subln)


def _out_kernel(ya_ref, yb_ref, zb0_ref, zb1_ref, zb2_ref, gla_ref, glb_ref, x_ref, mod_ref,
                gf_ref, wd_ref, wf_ref, wo_ref, out_ref, ybz_ref):
    zb_refs = (zb0_ref, zb1_ref, zb2_ref)
    wz = zb0_ref.shape[1]
    for i in range(3):
        sl = slice(i * wz, (i + 1) * wz)
        ybz_ref[:, sl] = (yb_ref[:, sl].astype(F32) * _silu(zb_refs[i][...].astype(F32))).astype(BF16)
    pa = jnp.dot(ya_ref[...], wd_ref[...], preferred_element_type=F32)
    pb = jnp.dot(ybz_ref[...], wf_ref[...], preferred_element_type=F32)
    merged = (_sigmoid(gla_ref[...].astype(F32)) * pa + _sigmoid(glb_ref[...].astype(F32)) * pb)
    delta = jnp.dot(merged.astype(BF16), wo_ref[...], preferred_element_type=F32)
    y = x_ref[...] + mod_ref[0, 2:3, :] * delta
    ms = jnp.mean(y * y, axis=-1, keepdims=True)
    out_ref[...] = y * lax.rsqrt(ms + EPS) * gf_ref[...]


def _const_spec(shape):
    return pl.BlockSpec(shape, lambda i: (0,) * len(shape), pipeline_mode=pl.Buffered(1))


def _out(ya, yb, proj2d, x2d, mod, g_final, wd, wf, wo, seq):
    t, d = x2d.shape
    tm = OUT_TM
    tiles_per_batch = seq // tm
    wz = 512
    zb_blk = OFF_ZB // wz
    row = lambda w: pl.BlockSpec((tm, w), lambda i: (i, 0))
    in_specs = (
        [row(DIL_WIDTH), row(DIFF_WIDTH)]
        + [pl.BlockSpec((tm, wz), functools.partial(lambda i, k: (i, zb_blk + k), k=k))
           for k in range(3)]
        + [pl.BlockSpec((tm, d), lambda i: (i, OFF_GLA // D_MODEL)),
           pl.BlockSpec((tm, d), lambda i: (i, OFF_GLB // D_MODEL)),
           row(d),
           pl.BlockSpec((1, 3, d), lambda i: (i // tiles_per_batch, 0, 0)),
           _const_spec((1, d)),
           _const_spec(wd.shape), _const_spec(wf.shape), _const_spec(wo.shape)])
    return pl.pallas_call(
        _out_kernel,
        grid=(t // tm,),
        in_specs=in_specs,
        out_specs=row(d),
        out_shape=jax.ShapeDtypeStruct((t, d), F32),
        scratch_shapes=[pltpu.VMEM((tm, DIFF_WIDTH), BF16)],
        compiler_params=pltpu.CompilerParams(
            dimension_semantics=("parallel",),
            vmem_limit_bytes=VMEM_LIMIT_BYTES),
        name="out",
    )(ya, yb, proj2d, proj2d, proj2d, proj2d, proj2d, x2d, mod, g_final, wd, wf, wo)


def _q_column_scale():
    cs = np.ones((1, N_IN), np.float32)
    cs[:, OFF_QA:OFF_QA + DIL_QKV] = Q_PRESCALE
    cs[:, OFF_QB:OFF_QB + DIFF_WIDTH] = Q_PRESCALE
    return jnp.asarray(cs)


def kernel(x, c, w_ada, b_ada, g_norm, w_in, w_o_dil, w_o_diff, w_out,
           lambda_q1, lambda_k1, lambda_q2, lambda_k2, g_subln, g_final):
    batch, seq, d = x.shape
    assert w_ada.shape[0] == 1 and d == D_MODEL and w_in.shape[2] == N_IN
    assert seq % DIL_TT == 0 and seq % PROJ_TM == 0 and seq % DIFF_T == 0
    rows = -(-batch // 8) * 8
    c_pad = jnp.pad(c, ((0, rows - batch), (0, 0)))
    x2d = x.reshape(batch * seq, d)
    ada = _ada(c_pad, w_ada[0], b_ada[0].reshape(1, 3 * d))[:batch]
    mod = ada.reshape(batch, 3, d)
    proj2d = _proj(x2d, mod, g_norm[0].reshape(1, d), w_in[0].astype(BF16), _q_column_scale(), seq)
    ya = _dil(proj2d, batch, seq)
    lamv = jnp.stack([lambda_q1[0], lambda_k1[0], lambda_q2[0], lambda_k2[0]])
    yb = _diff(proj2d, lamv, g_subln[0].reshape(1, DIFF_HEAD_DIM), batch, seq)
    out = _out(ya, yb, proj2d, x2d, mod, g_final.reshape(1, d), w_o_dil[0].astype(BF16),
               w_o_diff[0].astype(BF16), w_out[0].astype(BF16), seq)
    return out.reshape(batch, seq, d)
```

```python
import functools
import math

import numpy as np
import jax
import jax.numpy as jnp
from jax import lax
from jax.experimental import pallas as pl
from jax.experimental.pallas import tpu as pltpu

F32 = jnp.float32
BF16 = jnp.bfloat16

D_MODEL = 2048
HEAD_DIM = 128
LANES = 128
SUBLANES = 8
DIL_GROUPS = ((128, 1), (512, 4), (2048, 16))
DIL_HEADS_PER_GROUP = 8
DIL_N_HEADS = DIL_HEADS_PER_GROUP * len(DIL_GROUPS)
DIL_QKV = DIL_N_HEADS * HEAD_DIM
DIL_WIDTH = DIL_HEADS_PER_GROUP * HEAD_DIM
DIL_SPAN = 128
DIFF_N_HEADS = 6
DIFF_HEAD_DIM = 2 * HEAD_DIM
DIFF_WIDTH = DIFF_N_HEADS * DIFF_HEAD_DIM
EPS = 1e-6
NEG_INF = -1e30
IN_SIZES = (DIL_QKV, DIL_QKV, DIL_QKV, DIL_WIDTH,
            DIFF_WIDTH, DIFF_WIDTH, DIFF_WIDTH, DIFF_WIDTH, D_MODEL, D_MODEL)
N_IN = sum(IN_SIZES)
(OFF_QA, OFF_KA, OFF_VA, OFF_ZA, OFF_QB, OFF_KB, OFF_VB, OFF_ZB, OFF_GLA, OFF_GLB) = (
    int(v) for v in np.concatenate([[0], np.cumsum(IN_SIZES)[:-1]]))
LOG2E = 1.4426950408889634
Q_PRESCALE = HEAD_DIM ** -0.5 * LOG2E
LAMBDA_INIT = 0.8 - 0.6 * math.exp(-0.3 * 0)

VMEM_LIMIT_BYTES = 56 * 1024 * 1024

PROJ_TM = 1024
PROJ_TN = 1024
PROJ_NORM_CHUNK = 32
DIL_TT = DIL_SPAN * DIL_GROUPS[-1][1]
DIL_UNITS = DIL_TT // DIL_SPAN
DIL_STAGE_ROWS = 256
DIL_MERGE_CHUNK = 64
DIFF_T = 1024
DIFF_DIAG_W = 256
DIFF_ONES_ROWS = 16
OUT_TM = 256


def _alibi_slopes(n):
    return [float(2.0 ** (-8.0 * (i + 1) / n)) for i in range(n)]


def _dot_nt(a, b):
    return lax.dot_general(a, b, (((1,), (1,)), ((), ())), preferred_element_type=F32)


def _silu(z):
    return z / (1.0 + jnp.exp(-z))


def _sigmoid(z):
    return 1.0 / (1.0 + jnp.exp(-z))


def _column_reduce(x, pair_op, final_op):
    parts = [x[i:i + SUBLANES, :] for i in range(0, x.shape[0], SUBLANES)]
    while len(parts) > 1:
        paired = [pair_op(parts[i], parts[i + 1]) for i in range(0, len(parts) - 1, 2)]
        parts = paired + parts[len(parts) - len(parts) % 2:]
    return final_op(parts[0], axis=0, keepdims=True)


def _select_by_index(idx, values):
    out = jnp.float32(values[0])
    for i in range(1, len(values)):
        out = jnp.where(idx == i, jnp.float32(values[i]), out)
    return out


def _ada_kernel(c_ref, w_ref, b_ref, o_ref):
    c = c_ref[...]
    o_ref[...] = jnp.dot(_silu(c), w_ref[...], preferred_element_type=F32,
                         precision=lax.Precision.HIGHEST) + b_ref[...]


def _ada(c_pad, w_ada, b_ada):
    rows, d = c_pad.shape
    n = w_ada.shape[1]
    tn = 512
    return pl.pallas_call(
        _ada_kernel,
        grid=(n // tn,),
        in_specs=[pl.BlockSpec((rows, d), lambda j: (0, 0)),
                  pl.BlockSpec((d, tn), lambda j: (0, j)),
                  pl.BlockSpec((1, tn), lambda j: (0, j))],
        out_specs=pl.BlockSpec((rows, tn), lambda j: (0, j)),
        out_shape=jax.ShapeDtypeStruct((rows, n), F32),
        name="ada",
    )(c_pad, w_ada, b_ada)


def _proj_kernel(x_ref, mod_ref, g_ref, w_ref, cs_ref, o_ref, hn_ref):
    @pl.when(pl.program_id(1) == 0)
    def _():
        shift = mod_ref[0, 0:1, :]
        mult = g_ref[...] * (1.0 + mod_ref[0, 1:2, :])

        def body(i, carry):
            r = pl.multiple_of(i * PROJ_NORM_CHUNK, PROJ_NORM_CHUNK)
            xx = x_ref[pl.ds(r, PROJ_NORM_CHUNK), :]
            ms = jnp.mean(xx * xx, axis=-1, keepdims=True)
            hn_ref[pl.ds(r, PROJ_NORM_CHUNK), :] = (
                xx * lax.rsqrt(ms + EPS) * mult + shift).astype(BF16)
            return carry

        lax.fori_loop(0, x_ref.shape[0] // PROJ_NORM_CHUNK, body, 0)

    acc = jnp.dot(hn_ref[...], w_ref[...], preferred_element_type=F32)
    o_ref[...] = (acc * cs_ref[...]).astype(BF16)


def _proj(x2d, mod, g_norm, w_in_bf16, col_scale, seq):
    t, d = x2d.shape
    n = w_in_bf16.shape[1]
    tm, tn = PROJ_TM, PROJ_TN
    tiles_per_batch = seq // tm
    return pl.pallas_call(
        _proj_kernel,
        grid=(t // tm, n // tn),
        in_specs=[pl.BlockSpec((tm, d), lambda i, j: (i, 0)),
                  pl.BlockSpec((1, 3, d), lambda i, j: (i // tiles_per_batch, 0, 0)),
                  pl.BlockSpec((1, d), lambda i, j: (0, 0)),
                  pl.BlockSpec((d, tn), lambda i, j: (0, j)),
                  pl.BlockSpec((1, tn), lambda i, j: (0, j))],
        out_specs=pl.BlockSpec((tm, tn), lambda i, j: (i, j)),
        out_shape=jax.ShapeDtypeStruct((t, n), BF16),
        scratch_shapes=[pltpu.VMEM((tm, d), BF16)],
        compiler_params=pltpu.CompilerParams(
            dimension_semantics=("parallel", "arbitrary"),
            vmem_limit_bytes=VMEM_LIMIT_BYTES),
        name="proj",
    )(x2d, mod, g_norm, w_in_bf16, col_scale)


def _gather_residues(src_ref, src32, dst_ref, r, n_u, dst_pitch, dst_off):
    n32 = n_u * r // 2
    for c0 in range(0, n32, DIL_STAGE_ROWS):
        src32[c0:c0 + DIL_STAGE_ROWS, :] = pltpu.bitcast(
            src_ref[2 * c0:2 * (c0 + DIL_STAGE_ROWS), :], jnp.uint32)
    for p in range(r):
        for c0 in range(0, n_u, DIL_SPAN):
            w = src32[pl.ds(p // 2 + (r // 2) * c0, DIL_SPAN, stride=r // 2), :]
            w = (w << 16) if p % 2 == 0 else (w & jnp.uint32(0xFFFF0000))
            row = p * dst_pitch + dst_off + c0
            dst_ref[row:row + DIL_SPAN, :] = lax.bitcast_convert_type(w, F32).astype(BF16)


def _dil_kernel(*refs):
    ins, y_ref = refs[:10], refs[10]
    qd, w32, bm_ref = refs[11:14]
    kd, vd = refs[14:17], refs[17:20]
    o_scr, lse_scr = refs[20:23], refs[23:26]
    z_ref = ins[9]
    h = pl.program_id(1)
    first_tile = pl.program_id(2) == 0

    iq = lax.broadcasted_iota(jnp.int32, (DIL_SPAN, 2 * DIL_SPAN), 0)
    jk = lax.broadcasted_iota(jnp.int32, (DIL_SPAN, 2 * DIL_SPAN), 1)
    dist = (iq - jk + DIL_SPAN).astype(F32)
    valid = jnp.logical_and(jk >= iq, jk <= iq + DIL_SPAN)
    valid_first = jnp.logical_and(valid, jk >= DIL_SPAN)

    slopes = _alibi_slopes(DIL_N_HEADS)
    for g, (_, r) in enumerate(DIL_GROUPS):
        q_ref, kc_ref, vc_ref = ins[3 * g:3 * g + 3]
        kd_g, vd_g = kd[g], vd[g]
        nb = DIL_UNITS // r
        kpitch = (nb + 1) * DIL_SPAN
        coef = _select_by_index(
            h, [slopes[g * DIL_HEADS_PER_GROUP + i] * r * LOG2E for i in range(DIL_HEADS_PER_GROUP)])
        bias = -coef * dist
        bm_ref[0] = jnp.where(valid, bias, NEG_INF)
        bm_ref[1] = jnp.where(valid_first, bias, NEG_INF)

        @pl.when(first_tile)
        def _(r=r, kpitch=kpitch, kd_g=kd_g, vd_g=vd_g):
            for p in range(r):
                kd_g[p * kpitch:p * kpitch + DIL_SPAN, :] = jnp.zeros((DIL_SPAN, LANES), BF16)
                vd_g[p * kpitch:p * kpitch + DIL_SPAN, :] = jnp.zeros((DIL_SPAN, LANES), BF16)

        if r == 1:
            q_src = q_ref
            kd_g[DIL_SPAN:DIL_SPAN + DIL_TT, :] = kc_ref[...]
            vd_g[DIL_SPAN:DIL_SPAN + DIL_TT, :] = vc_ref[...]
        else:
            q_src = qd
            _gather_residues(q_ref, w32, qd, r, nb * DIL_SPAN, nb * DIL_SPAN, 0)
            _gather_residues(kc_ref, w32, kd_g, r, nb * DIL_SPAN, kpitch, DIL_SPAN)
            _gather_residues(vc_ref, w32, vd_g, r, nb * DIL_SPAN, kpitch, DIL_SPAN)

        for p in range(r):
            for ub in range(nb):
                rq = (p * nb + ub) * DIL_SPAN
                rk = p * kpitch + ub * DIL_SPAN
                bm = bm_ref[jnp.where(first_tile, 1, 0)] if ub == 0 else bm_ref[0]
                s = _dot_nt(q_src[rq:rq + DIL_SPAN, :], kd_g[rk:rk + 2 * DIL_SPAN, :]) + bm
                m = jnp.max(s, axis=-1, keepdims=True)
                e = jnp.exp2(s - m)
                l = jnp.sum(e, axis=-1, keepdims=True)
                o = jnp.dot(e.astype(BF16), vd_g[rk:rk + 2 * DIL_SPAN, :],
                            preferred_element_type=F32) * (1.0 / l)
                lse2 = jnp.broadcast_to(m + jnp.log(l) * LOG2E, (DIL_SPAN, LANES))
                row0 = ub * (DIL_SPAN * r) + p
                rows = pl.ds(row0, DIL_SPAN, stride=r) if r > 1 else pl.ds(row0, DIL_SPAN)
                o_scr[g][rows, :] = o
                lse_scr[g][rows, :] = lse2

        for p in range(r):
            last = p * kpitch + nb * DIL_SPAN
            kd_g[p * kpitch:p * kpitch + DIL_SPAN, :] = kd_g[last:last + DIL_SPAN, :]
            vd_g[p * kpitch:p * kpitch + DIL_SPAN, :] = vd_g[last:last + DIL_SPAN, :]

    def merge(i, carry):
        rows = pl.ds(pl.multiple_of(i * DIL_MERGE_CHUNK, DIL_MERGE_CHUNK), DIL_MERGE_CHUNK)
        l0, l1, l2 = lse_scr[0][rows, :], lse_scr[1][rows, :], lse_scr[2][rows, :]
        mx = jnp.maximum(jnp.maximum(l0, l1), l2)
        e0, e1, e2 = jnp.exp2(l0 - mx), jnp.exp2(l1 - mx), jnp.exp2(l2 - mx)
        om = (e0 * o_scr[0][rows, :] + e1 * o_scr[1][rows, :] + e2 * o_scr[2][rows, :]) / (e0 + e1 + e2)
        y_ref[rows, :] = (om * _silu(z_ref[rows, :].astype(F32))).astype(BF16)
        return carry

    lax.fori_loop(0, DIL_TT // DIL_MERGE_CHUNK, merge, 0)


def _dil(proj2d, batch, seq):
    tt = DIL_TT
    tiles = seq // tt

    def cur(col):
        return pl.BlockSpec((tt, LANES), lambda b, h, t: (b * tiles + t, col + h))

    in_specs = []
    for g in range(len(DIL_GROUPS)):
        in_specs += [cur((off + g * DIL_WIDTH) // LANES) for off in (OFF_QA, OFF_KA, OFF_VA)]
    in_specs.append(cur(OFF_ZA // LANES))
    kv_scratch = [pltpu.VMEM((tt + DIL_SPAN * r, LANES), BF16) for _, r in DIL_GROUPS]
    return pl.pallas_call(
        _dil_kernel,
        grid=(batch, DIL_HEADS_PER_GROUP, tiles),
        in_specs=in_specs,
        out_specs=pl.BlockSpec((tt, LANES), lambda b, h, t: (b * tiles + t, h)),
        out_shape=jax.ShapeDtypeStruct((batch * seq, DIL_WIDTH), BF16),
        scratch_shapes=(
            [pltpu.VMEM((tt, LANES), BF16), pltpu.VMEM((tt // 2, LANES), jnp.uint32),
             pltpu.VMEM((2, DIL_SPAN, 2 * DIL_SPAN), F32)]
            + kv_scratch + kv_scratch + [pltpu.VMEM((tt, LANES), F32)] * 6),
        compiler_params=pltpu.CompilerParams(
            dimension_semantics=("parallel", "parallel", "arbitrary"),
            vmem_limit_bytes=VMEM_LIMIT_BYTES),
        name="dil",
    )(*([proj2d] * 10))


def _diff_kernel(q_ref, k_ref, v_ref, lamv_ref, gsub_ref, o_ref,
                 vt_ref, st_ref, pt_ref, cb_ref, m_ref, a_ref, acc_ref, *, slopes2):
    h = pl.program_id(1)
    qi = pl.program_id(2)
    t = q_ref.shape[0]
    nk = vt_ref.shape[0]
    slope2 = _select_by_index(h, slopes2)

    @pl.when(qi == 0)
    def _():
        for j in range(nk):
            vt_ref[j, 0:DIFF_HEAD_DIM, :] = v_ref[j * t:(j + 1) * t, :].astype(F32).T.astype(BF16)
            vt_ref[j, DIFF_HEAD_DIM:, :] = jnp.ones((DIFF_ONES_ROWS, t), BF16)

    cb_ref[...] = slope2 * lax.broadcasted_iota(jnp.int32, (t, LANES), 0).astype(F32)
    m_ref[...] = jnp.full(m_ref.shape, NEG_INF, F32)
    acc_ref[...] = jnp.zeros(acc_ref.shape, F32)

    def block(kj):
        k0 = pl.multiple_of(kj * t, t)
        cblk = slope2 * ((kj - qi) * t).astype(F32)
        for c in range(2):
            sl = slice(c * HEAD_DIM, (c + 1) * HEAD_DIM)
            st_ref[c] = _dot_nt(k_ref[pl.ds(k0, t), sl], q_ref[:, sl])
        for c in range(2):
            for cg in range(t // LANES):
                cols = slice(cg * LANES, (cg + 1) * LANES)
                x = st_ref[c, :, cols] + cb_ref[...]
                m_old = m_ref[c, :, cols]
                m_new = jnp.maximum(m_old, _column_reduce(x, jnp.maximum, jnp.max) + cblk)
                m_ref[c, :, cols] = m_new
                a_ref[c, :, cols] = jnp.exp2(m_old - m_new)
                pt_ref[c, :, cols] = jnp.exp2((x - (m_new - cblk)).astype(BF16))
        vt = vt_ref[kj]
        for c in range(2):
            acc_ref[c] = acc_ref[c] * a_ref[c] + jnp.dot(vt, pt_ref[c], preferred_element_type=F32)

    def diagonal_block():
        k0 = pl.multiple_of(qi * t, t)
        w = DIFF_DIAG_W
        for c in range(2):
            sl = slice(c * HEAD_DIM, (c + 1) * HEAD_DIM)
            for qh in range(t // w):
                kend = (qh + 1) * w
                qs = slice(qh * w, kend)
                st_ref[c, 0:kend, qs] = _dot_nt(k_ref[pl.ds(k0, kend), sl], q_ref[qs, sl])
        for c in range(2):
            for cg in range(t // LANES):
                cols = slice(cg * LANES, (cg + 1) * LANES)
                kend = (cg * LANES // w + 1) * w
                key = lax.broadcasted_iota(jnp.int32, (kend, LANES), 0)
                qry = lax.broadcasted_iota(jnp.int32, (kend, LANES), 1) + cg * LANES
                x = jnp.where(key <= qry, st_ref[c, 0:kend, cols] + cb_ref[0:kend, :], NEG_INF)
                m_old = m_ref[c, :, cols]
                m_new = jnp.maximum(m_old, _column_reduce(x, jnp.maximum, jnp.max))
                m_ref[c, :, cols] = m_new
                a_ref[c, :, cols] = jnp.exp2(m_old - m_new)
                pt_ref[c, 0:kend, cols] = jnp.exp2((x - m_new).astype(BF16))
        vt = vt_ref[qi]
        for c in range(2):
            for qh in range(t // w):
                kend = (qh + 1) * w
                qs = slice(qh * w, kend)
                acc_ref[c, :, qs] = (acc_ref[c, :, qs] * a_ref[c, :, qs]
                                     + jnp.dot(vt[:, 0:kend], pt_ref[c, 0:kend, qs],
                                               preferred_element_type=F32))

    def body(kj, carry):
        block(kj)
        return carry

    lax.fori_loop(0, qi, body, 0)
    diagonal_block()

    lamv = lamv_ref[...]
    lam = (jnp.exp(jnp.sum(lamv[0:1] * lamv[1:2], axis=-1, keepdims=True))
           - jnp.exp(jnp.sum(lamv[2:3] * lamv[3:4], axis=-1, keepdims=True)) + LAMBDA_INIT)
    d = DIFF_HEAD_DIM
    ot = (acc_ref[0, 0:d, :] * (1.0 / acc_ref[0, d:d + 1, :])
          - lam * (acc_ref[1, 0:d, :] * (1.0 / acc_ref[1, d:d + 1, :])))
    ms = jnp.mean(ot * ot, axis=0, keepdims=True)
    ot = ot * lax.rsqrt(ms + EPS)
    o_ref[...] = (ot.T * gsub_ref[...] * (1.0 - LAMBDA_INIT)).astype(o_ref.dtype)


def _diff(proj2d, lamv, g_subln, batch, seq):
    t = DIFF_T
    nq = seq // t
    wq, wk, wv = OFF_QB // DIFF_HEAD_DIM, OFF_KB // DIFF_HEAD_DIM, OFF_VB // DIFF_HEAD_DIM
    slopes2 = tuple(s * LOG2E for s in _alibi_slopes(DIFF_N_HEADS))
    vrows = DIFF_HEAD_DIM + DIFF_ONES_ROWS
    return pl.pallas_call(
        functools.partial(_diff_kernel, slopes2=slopes2),
        grid=(batch, DIFF_N_HEADS, nq),
        in_specs=[
            pl.BlockSpec((t, DIFF_HEAD_DIM), lambda b, h, q: (b * nq + q, wq + h)),
            pl.BlockSpec((seq, DIFF_HEAD_DIM), lambda b, h, q: (b, wk + h)),
            pl.BlockSpec((seq, DIFF_HEAD_DIM), lambda b, h, q: (b, wv + h)),
            pl.BlockSpec((4, HEAD_DIM), lambda b, h, q: (0, 0)),
            pl.BlockSpec((1, DIFF_HEAD_DIM), lambda b, h, q: (0, 0)),
        ],
        out_specs=pl.BlockSpec((t, DIFF_HEAD_DIM), lambda b, h, q: (b * nq + q, h)),
        out_shape=jax.ShapeDtypeStruct((batch * seq, DIFF_WIDTH), BF16),
        scratch_shapes=[
            pltpu.VMEM((nq, vrows, t), BF16),
            pltpu.VMEM((2, t, t), F32),
            pltpu.VMEM((2, t, t), BF16),
            pltpu.VMEM((t, LANES), F32),
            pltpu.VMEM((2, 1, t), F32),
            pltpu.VMEM((2, 1, t), F32),
            pltpu.VMEM((2, vrows, t), F32),
        ],
        compiler_params=pltpu.CompilerParams(
            dimension_semantics=("parallel", "parallel", "arbitrary"),
            vmem_limit_bytes=VMEM_LIMIT_BYTES),
        name="diff",
    )(proj2d, proj2d, proj2d, lamv, g_subln)


def _out_kernel(ya_ref, yb_ref, zb0_ref, zb1_ref, zb2_ref, gla_ref, glb_ref, x_ref, mod_ref,
                gf_ref, wd_ref, wf_ref, wo_ref, out_ref, ybz_ref):
    zb_refs = (zb0_ref, zb1_ref, zb2_ref)
    wz = zb0_ref.shape[1]
    for i in range(3):
        sl = slice(i * wz, (i + 1) * wz)
        ybz_ref[:, sl] = (yb_ref[:, sl].astype(F32) * _silu(zb_refs[i][...].astype(F32))).astype(BF16)
    pa = jnp.dot(ya_ref[...], wd_ref[...], preferred_element_type=F32)
    pb = jnp.dot(ybz_ref[...], wf_ref[...], preferred_element_type=F32)
    merged = (_sigmoid(gla_ref[...].astype(F32)) * pa + _sigmoid(glb_ref[...].astype(F32)) * pb)
    delta = jnp.dot(merged.astype(BF16), wo_ref[...], preferred_element_type=F32)
    y = x_ref[...] + mod_ref[0, 2:3, :] * delta
    ms = jnp.mean(y * y, axis=-1, keepdims=True)
    out_ref[...] = y * lax.rsqrt(ms + EPS) * gf_ref[...]


def _const_spec(shape):
    return pl.BlockSpec(shape, lambda i: (0,) * len(shape), pipeline_mode=pl.Buffered(1))


def _out(ya, yb, proj2d, x2d, mod, g_final, wd, wf, wo, seq):
    t, d = x2d.shape
    tm = OUT_TM
    tiles_per_batch = seq // tm
    wz = 512
    zb_blk = OFF_ZB // wz
    row = lambda w: pl.BlockSpec((tm, w), lambda i: (i, 0))
    in_specs = (
        [row(DIL_WIDTH), row(DIFF_WIDTH)]
        + [pl.BlockSpec((tm, wz), functools.partial(lambda i, k: (i, zb_blk + k), k=k))
           for k in range(3)]
        + [pl.BlockSpec((tm, d), lambda i: (i, OFF_GLA // D_MODEL)),
           pl.BlockSpec((tm, d), lambda i: (i, OFF_GLB // D_MODEL)),
           row(d),
           pl.BlockSpec((1, 3, d), lambda i: (i // tiles_per_batch, 0, 0)),
           _const_spec((1, d)),
           _const_spec(wd.shape), _const_spec(wf.shape), _const_spec(wo.shape)])
    return pl.pallas_call(
        _out_kernel,
        grid=(t // tm,),
        in_specs=in_specs,
        out_specs=row(d),
        out_shape=jax.ShapeDtypeStruct((t, d), F32),
        scratch_shapes=[pltpu.VMEM((tm, DIFF_WIDTH), BF16)],
        compiler_params=pltpu.CompilerParams(
            dimension_semantics=("parallel",),
            vmem_limit_bytes=VMEM_LIMIT_BYTES),
        name="out",
    )(ya, yb, proj2d, proj2d, proj2d, proj2d, proj2d, x2d, mod, g_final, wd, wf, wo)


def _q_column_scale():
    cs = np.ones((1, N_IN), np.float32)
    cs[:, OFF_QA:OFF_QA + DIL_QKV] = Q_PRESCALE
    cs[:, OFF_QB:OFF_QB + DIFF_WIDTH] = Q_PRESCALE
    return jnp.asarray(cs)


def kernel(x, c, w_ada, b_ada, g_norm, w_in, w_o_dil, w_o_diff, w_out,
           lambda_q1, lambda_k1, lambda_q2, lambda_k2, g_subln, g_final):
    batch, seq, d = x.shape
    assert w_ada.shape[0] == 1 and d == D_MODEL and w_in.shape[2] == N_IN
    assert seq % DIL_TT == 0 and seq % PROJ_TM == 0 and seq % DIFF_T == 0
    rows = -(-batch // 8) * 8
    c_pad = jnp.pad(c, ((0, rows - batch), (0, 0)))
    x2d = x.reshape(batch * seq, d)
    ada = _ada(c_pad, w_ada[0], b_ada[0].reshape(1, 3 * d))[:batch]
    mod = ada.reshape(batch, 3, d)
    proj2d = _proj(x2d, mod, g_norm[0].reshape(1, d), w_in[0].astype(BF16), _q_column_scale(), seq)
    ya = _dil(proj2d, batch, seq)
    lamv = jnp.stack([lambda_q1[0], lambda_k1[0], lambda_q2[0], lambda_k2[0]])
    yb = _diff(proj2d, lamv, g_subln[0].reshape(1, DIFF_HEAD_DIM), batch, seq)
    out = _out(ya, yb, proj2d, x2d, mod, g_final.reshape(1, d), w_o_dil[0].astype(BF16),
               w_o_diff[0].astype(BF16), w_out[0].astype(BF16), seq)
    return out.reshape(batch, seq, d)
```

```python
import functools
import math

import numpy as np
import jax
import jax.numpy as jnp
from jax import lax
from jax.experimental import pallas as pl
from jax.experimental.pallas import tpu as pltpu

F32 = jnp.float32
BF16 = jnp.bfloat16

D_MODEL = 2048
HEAD_DIM = 128
LANES = 128
SUBLANES = 8
DIL_GROUPS = ((128, 1), (512, 4), (2048, 16))
DIL_HEADS_PER_GROUP = 8
DIL_N_HEADS = DIL_HEADS_PER_GROUP * len(DIL_GROUPS)
DIL_QKV = DIL_N_HEADS * HEAD_DIM
DIL_WIDTH = DIL_HEADS_PER_GROUP * HEAD_DIM
DIL_SPAN = 128
DIFF_N_HEADS = 6
DIFF_HEAD_DIM = 2 * HEAD_DIM
DIFF_WIDTH = DIFF_N_HEADS * DIFF_HEAD_DIM
EPS = 1e-6
NEG_INF = -1e30
IN_SIZES = (DIL_QKV, DIL_QKV, DIL_QKV, DIL_WIDTH,
            DIFF_WIDTH, DIFF_WIDTH, DIFF_WIDTH, DIFF_WIDTH, D_MODEL, D_MODEL)
N_IN = sum(IN_SIZES)
(OFF_QA, OFF_KA, OFF_VA, OFF_ZA, OFF_QB, OFF_KB, OFF_VB, OFF_ZB, OFF_GLA, OFF_GLB) = (
    int(v) for v in np.concatenate([[0], np.cumsum(IN_SIZES)[:-1]]))
LOG2E = 1.4426950408889634
Q_PRESCALE = HEAD_DIM ** -0.5 * LOG2E
LAMBDA_INIT = 0.8 - 0.6 * math.exp(-0.3 * 0)

VMEM_LIMIT_BYTES = 56 * 1024 * 1024

PROJ_TM = 1024
PROJ_TN = 2048
PROJ_NORM_CHUNK = 32
DIL_TT = DIL_SPAN * DIL_GROUPS[-1][1]
DIL_UNITS = DIL_TT // DIL_SPAN
DIL_STAGE_ROWS = 256
DIL_MERGE_CHUNK = 64
DIFF_T = 1024
DIFF_DIAG_W = 256
DIFF_ONES_ROWS = 16
OUT_TM = 256


def _alibi_slopes(n):
    return [float(2.0 ** (-8.0 * (i + 1) / n)) for i in range(n)]


def _dot_nt(a, b):
    return lax.dot_general(a, b, (((1,), (1,)), ((), ())), preferred_element_type=F32)


def _silu(z):
    return z / (1.0 + jnp.exp(-z))


def _sigmoid(z):
    return 1.0 / (1.0 + jnp.exp(-z))


def _column_reduce(x, pair_op, final_op):
    parts = [x[i:i + SUBLANES, :] for i in range(0, x.shape[0], SUBLANES)]
    while len(parts) > 1:
        paired = [pair_op(parts[i], parts[i + 1]) for i in range(0, len(parts) - 1, 2)]
        parts = paired + parts[len(parts) - len(parts) % 2:]
    return final_op(parts[0], axis=0, keepdims=True)


def _select_by_index(idx, values):
    out = jnp.float32(values[0])
    for i in range(1, len(values)):
        out = jnp.where(idx == i, jnp.float32(values[i]), out)
    return out


def _ada_kernel(c_ref, w_ref, b_ref, o_ref):
    c = c_ref[...]
    o_ref[...] = jnp.dot(_silu(c), w_ref[...], preferred_element_type=F32,
                         precision=lax.Precision.HIGHEST) + b_ref[...]


def _ada(c_pad, w_ada, b_ada):
    rows, d = c_pad.shape
    n = w_ada.shape[1]
    tn = 512
    return pl.pallas_call(
        _ada_kernel,
        grid=(n // tn,),
        in_specs=[pl.BlockSpec((rows, d), lambda j: (0, 0)),
                  pl.BlockSpec((d, tn), lambda j: (0, j)),
                  pl.BlockSpec((1, tn), lambda j: (0, j))],
        out_specs=pl.BlockSpec((rows, tn), lambda j: (0, j)),
        out_shape=jax.ShapeDtypeStruct((rows, n), F32),
        name="ada",
    )(c_pad, w_ada, b_ada)


def _proj_kernel(x_ref, mod_ref, g_ref, w_ref, cs_ref, o_ref, hn_ref):
    @pl.when(pl.program_id(1) == 0)
    def _():
        shift = mod_ref[0, 0:1, :]
        mult = g_ref[...] * (1.0 + mod_ref[0, 1:2, :])

        def body(i, carry):
            r = pl.multiple_of(i * PROJ_NORM_CHUNK, PROJ_NORM_CHUNK)
            xx = x_ref[pl.ds(r, PROJ_NORM_CHUNK), :]
            ms = jnp.mean(xx * xx, axis=-1, keepdims=True)
            hn_ref[pl.ds(r, PROJ_NORM_CHUNK), :] = (
                xx * lax.rsqrt(ms + EPS) * mult + shift).astype(BF16)
            return carry

        lax.fori_loop(0, x_ref.shape[0] // PROJ_NORM_CHUNK, body, 0)

    acc = jnp.dot(hn_ref[...], w_ref[...], preferred_element_type=F32)
    o_ref[...] = (acc * cs_ref[...]).astype(BF16)


def _proj(x2d, mod, g_norm, w_in_bf16, col_scale, seq):
    t, d = x2d.shape
    n = w_in_bf16.shape[1]
    tm, tn = PROJ_TM, PROJ_TN
    tiles_per_batch = seq // tm
    return pl.pallas_call(
        _proj_kernel,
        grid=(t // tm, n // tn),
        in_specs=[pl.BlockSpec((tm, d), lambda i, j: (i, 0)),
                  pl.BlockSpec((1, 3, d), lambda i, j: (i // tiles_per_batch, 0, 0)),
                  pl.BlockSpec((1, d), lambda i, j: (0, 0)),
                  pl.BlockSpec((d, tn), lambda i, j: (0, j)),
                  pl.BlockSpec((1, tn), lambda i, j: (0, j))],
        out_specs=pl.BlockSpec((tm, tn), lambda i, j: (i, j)),
        out_shape=jax.ShapeDtypeStruct((t, n), BF16),
        scratch_shapes=[pltpu.VMEM((tm, d), BF16)],
        compiler_params=pltpu.CompilerParams(
            dimension_semantics=("parallel", "arbitrary"),
            vmem_limit_bytes=VMEM_LIMIT_BYTES),
        name="proj",
    )(x2d, mod, g_norm, w_in_bf16, col_scale)


def _gather_residues(src_ref, src32, dst_ref, r, n_u, dst_pitch, dst_off):
    n32 = n_u * r // 2
    for c0 in range(0, n32, DIL_STAGE_ROWS):
        src32[c0:c0 + DIL_STAGE_ROWS, :] = pltpu.bitcast(
            src_ref[2 * c0:2 * (c0 + DIL_STAGE_ROWS), :], jnp.uint32)
    for p in range(r):
        for c0 in range(0, n_u, DIL_SPAN):
            w = src32[pl.ds(p // 2 + (r // 2) * c0, DIL_SPAN, stride=r // 2), :]
            w = (w << 16) if p % 2 == 0 else (w & jnp.uint32(0xFFFF0000))
            row = p * dst_pitch + dst_off + c0
            dst_ref[row:row + DIL_SPAN, :] = lax.bitcast_convert_type(w, F32).astype(BF16)


def _dil_kernel(*refs):
    ins, y_ref = refs[:10], refs[10]
    qd, w32, bm_ref = refs[11:14]
    kd, vd = refs[14:17], refs[17:20]
    o_scr, lse_scr = refs[20:23], refs[23:26]
    z_ref = ins[9]
    h = pl.program_id(1)
    first_tile = pl.program_id(2) == 0

    iq = lax.broadcasted_iota(jnp.int32, (DIL_SPAN, 2 * DIL_SPAN), 0)
    jk = lax.broadcasted_iota(jnp.int32, (DIL_SPAN, 2 * DIL_SPAN), 1)
    dist = (iq - jk + DIL_SPAN).astype(F32)
    valid = jnp.logical_and(jk >= iq, jk <= iq + DIL_SPAN)
    valid_first = jnp.logical_and(valid, jk >= DIL_SPAN)

    slopes = _alibi_slopes(DIL_N_HEADS)
    for g, (_, r) in enumerate(DIL_GROUPS):
        q_ref, kc_ref, vc_ref = ins[3 * g:3 * g + 3]
        kd_g, vd_g = kd[g], vd[g]
        nb = DIL_UNITS // r
        kpitch = (nb + 1) * DIL_SPAN
        coef = _select_by_index(
            h, [slopes[g * DIL_HEADS_PER_GROUP + i] * r * LOG2E for i in range(DIL_HEADS_PER_GROUP)])
        bias = -coef * dist
        bm_ref[0] = jnp.where(valid, bias, NEG_INF)
        bm_ref[1] = jnp.where(valid_first, bias, NEG_INF)

        @pl.when(first_tile)
        def _(r=r, kpitch=kpitch, kd_g=kd_g, vd_g=vd_g):
            for p in range(r):
                kd_g[p * kpitch:p * kpitch + DIL_SPAN, :] = jnp.zeros((DIL_SPAN, LANES), BF16)
                vd_g[p * kpitch:p * kpitch + DIL_SPAN, :] = jnp.zeros((DIL_SPAN, LANES), BF16)

        if r == 1:
            q_src = q_ref
            kd_g[DIL_SPAN:DIL_SPAN + DIL_TT, :] = kc_ref[...]
            vd_g[DIL_SPAN:DIL_SPAN + DIL_TT, :] = vc_ref[...]
        else:
            q_src = qd
            _gather_residues(q_ref, w32, qd, r, nb * DIL_SPAN, nb * DIL_SPAN, 0)
            _gather_residues(kc_ref, w32, kd_g, r, nb * DIL_SPAN, kpitch, DIL_SPAN)
            _gather_residues(vc_ref, w32, vd_g, r, nb * DIL_SPAN, kpitch, DIL_SPAN)

        for p in range(r):
            for ub in range(nb):
                rq = (p * nb + ub) * DIL_SPAN
                rk = p * kpitch + ub * DIL_SPAN
                bm = bm_ref[jnp.where(first_tile, 1, 0)] if ub == 0 else bm_ref[0]
                s = _dot_nt(q_src[rq:rq + DIL_SPAN, :], kd_g[rk:rk + 2 * DIL_SPAN, :]) + bm
                m = jnp.max(s, axis=-1, keepdims=True)
                e = jnp.exp2(s - m)
                l = jnp.sum(e, axis=-1, keepdims=True)
                o = jnp.dot(e.astype(BF16), vd_g[rk:rk + 2 * DIL_SPAN, :],
                            preferred_element_type=F32) * (1.0 / l)
                lse2 = jnp.broadcast_to(m + jnp.log(l) * LOG2E, (DIL_SPAN, LANES))
                row0 = ub * (DIL_SPAN * r) + p
                rows = pl.ds(row0, DIL_SPAN, stride=r) if r > 1 else pl.ds(row0, DIL_SPAN)
                o_scr[g][rows, :] = o
                lse_scr[g][rows, :] = lse2

        for p in range(r):
            last = p * kpitch + nb * DIL_SPAN
            kd_g[p * kpitch:p * kpitch + DIL_SPAN, :] = kd_g[last:last + DIL_SPAN, :]
            vd_g[p * kpitch:p * kpitch + DIL_SPAN, :] = vd_g[last:last + DIL_SPAN, :]

    def merge(i, carry):
        rows = pl.ds(pl.multiple_of(i * DIL_MERGE_CHUNK, DIL_MERGE_CHUNK), DIL_MERGE_CHUNK)
        l0, l1, l2 = lse_scr[0][rows, :], lse_scr[1][rows, :], lse_scr[2][rows, :]
        mx = jnp.maximum(jnp.maximum(l0, l1), l2)
        e0, e1, e2 = jnp.exp2(l0 - mx), jnp.exp2(l1 - mx), jnp.exp2(l2 - mx)
        om = (e0 * o_scr[0][rows, :] + e1 * o_scr[1][rows, :] + e2 * o_scr[2][rows, :]) / (e0 + e1 + e2)
        y_ref[rows, :] = (om * _silu(z_ref[rows, :].astype(F32))).astype(BF16)
        return carry

    lax.fori_loop(0, DIL_TT // DIL_MERGE_CHUNK, merge, 0)


def _dil(proj2d, batch, seq):
    tt = DIL_TT
    tiles = seq // tt

    def cur(col):
        return pl.BlockSpec((tt, LANES), lambda b, h, t: (b * tiles + t, col + h))

    in_specs = []
    for g in range(len(DIL_GROUPS)):
        in_specs += [cur((off + g * DIL_WIDTH) // LANES) for off in (OFF_QA, OFF_KA, OFF_VA)]
    in_specs.append(cur(OFF_ZA // LANES))
    kv_scratch = [pltpu.VMEM((tt + DIL_SPAN * r, LANES), BF16) for _, r in DIL_GROUPS]
    return pl.pallas_call(
        _dil_kernel,
        grid=(batch, DIL_HEADS_PER_GROUP, tiles),
        in_specs=in_specs,
        out_specs=pl.BlockSpec((tt, LANES), lambda b, h, t: (b * tiles + t, h)),
        out_shape=jax.ShapeDtypeStruct((batch * seq, DIL_WIDTH), BF16),
        scratch_shapes=(
            [pltpu.VMEM((tt, LANES), BF16), pltpu.VMEM((tt // 2, LANES), jnp.uint32),
             pltpu.VMEM((2, DIL_SPAN, 2 * DIL_SPAN), F32)]
            + kv_scratch + kv_scratch + [pltpu.VMEM((tt, LANES), F32)] * 6),
        compiler_params=pltpu.CompilerParams(
            dimension_semantics=("parallel", "parallel", "arbitrary"),
            vmem_limit_bytes=VMEM_LIMIT_BYTES),
        name="dil",
    )(*([proj2d] * 10))


def _diff_kernel(q_ref, k_ref, v_ref, lamv_ref, gsub_ref, o_ref,
                 vt_ref, st_ref, pt_ref, cb_ref, m_ref, a_ref, acc_ref, *, slopes2):
    h = pl.program_id(1)
    qi = pl.program_id(2)
    t = q_ref.shape[0]
    nk = vt_ref.shape[0]
    slope2 = _select_by_index(h, slopes2)

    @pl.when(qi == 0)
    def _():
        for j in range(nk):
            vt_ref[j, 0:DIFF_HEAD_DIM, :] = v_ref[j * t:(j + 1) * t, :].astype(F32).T.astype(BF16)
            vt_ref[j, DIFF_HEAD_DIM:, :] = jnp.ones((DIFF_ONES_ROWS, t), BF16)

    cb_ref[...] = slope2 * lax.broadcasted_iota(jnp.int32, (t, LANES), 0).astype(F32)
    m_ref[...] = jnp.full(m_ref.shape, NEG_INF, F32)
    acc_ref[...] = jnp.zeros(acc_ref.shape, F32)

    def block(kj):
        k0 = pl.multiple_of(kj * t, t)
        cblk = slope2 * ((kj - qi) * t).astype(F32)
        for c in range(2):
            sl = slice(c * HEAD_DIM, (c + 1) * HEAD_DIM)
            st_ref[c] = _dot_nt(k_ref[pl.ds(k0, t), sl], q_ref[:, sl])
        for c in range(2):
            for cg in range(t // LANES):
                cols = slice(cg * LANES, (cg + 1) * LANES)
                x = st_ref[c, :, cols] + cb_ref[...]
                m_old = m_ref[c, :, cols]
                m_new = jnp.maximum(m_old, _column_reduce(x, jnp.maximum, jnp.max) + cblk)
                m_ref[c, :, cols] = m_new
                a_ref[c, :, cols] = jnp.exp2(m_old - m_new)
                pt_ref[c, :, cols] = jnp.exp2((x - (m_new - cblk)).astype(BF16))
        vt = vt_ref[kj]
        for c in range(2):
            acc_ref[c] = acc_ref[c] * a_ref[c] + jnp.dot(vt, pt_ref[c], preferred_element_type=F32)

    def diagonal_block():
        k0 = pl.multiple_of(qi * t, t)
        w = DIFF_DIAG_W
        for c in range(2):
            sl = slice(c * HEAD_DIM, (c + 1) * HEAD_DIM)
            for qh in range(t // w):
                kend = (qh + 1) * w
                qs = slice(qh * w, kend)
                st_ref[c, 0:kend, qs] = _dot_nt(k_ref[pl.ds(k0, kend), sl], q_ref[qs, sl])
        for c in range(2):
            for cg in range(t // LANES):
                cols = slice(cg * LANES, (cg + 1) * LANES)
                kend = (cg * LANES // w + 1) * w
                key = lax.broadcasted_iota(jnp.int32, (kend, LANES), 0)
                qry = lax.broadcasted_iota(jnp.int32, (kend, LANES), 1) + cg * LANES
                x = jnp.where(key <= qry, st_ref[c, 0:kend, cols] + cb_ref[0:kend, :], NEG_INF)
                m_old = m_ref[c, :, cols]
                m_new = jnp.maximum(m_old, _column_reduce(x, jnp.maximum, jnp.max))
                m_ref[c, :, cols] = m_new
                a_ref[c, :, cols] = jnp.exp2(m_old - m_new)
                pt_ref[c, 0:kend, cols] = jnp.exp2((x - m_new).astype(BF16))
        vt = vt_ref[qi]
        for c in range(2):
            for qh in range(t // w):
                kend = (qh + 1) * w
                qs = slice(qh * w, kend)
                acc_ref[c, :, qs] = (acc_ref[c, :, qs] * a_ref[c, :, qs]
                                     + jnp.dot(vt[:, 0:kend], pt_ref[c, 0:kend, qs],
                                               preferred_element_type=F32))

    def body(kj, carry):
        block(kj)
        return carry

    lax.fori_loop(0, qi, body, 0)
    diagonal_block()

    lamv = lamv_ref[...]
    lam = (jnp.exp(jnp.sum(lamv[0:1] * lamv[1:2], axis=-1, keepdims=True))
           - jnp.exp(jnp.sum(lamv[2:3] * lamv[3:4], axis=-1, keepdims=True)) + LAMBDA_INIT)
    d = DIFF_HEAD_DIM
    ot = (acc_ref[0, 0:d, :] * (1.0 / acc_ref[0, d:d + 1, :])
          - lam * (acc_ref[1, 0:d, :] * (1.0 / acc_ref[1, d:d + 1, :])))
    ms = jnp.mean(ot * ot, axis=0, keepdims=True)
    ot = ot * lax.rsqrt(ms + EPS)
    o_ref[...] = (ot.T * gsub_ref[...] * (1.0 - LAMBDA_INIT)).astype(o_ref.dtype)


def _diff(proj2d, lamv, g_subln, batch, seq):
    t = DIFF_T
    nq = seq // t
    wq, wk, wv = OFF_QB // DIFF_HEAD_DIM, OFF_KB // DIFF_HEAD_DIM, OFF_VB // DIFF_HEAD_DIM
    slopes2 = tuple(s * LOG2E for s in _alibi_slopes(DIFF_N_HEADS))
    vrows = DIFF_HEAD_DIM + DIFF_ONES_ROWS
    return pl.pallas_call(
        functools.partial(_diff_kernel, slopes2=slopes2),
        grid=(batch, DIFF_N_HEADS, nq),
        in_specs=[
            pl.BlockSpec((t, DIFF_HEAD_DIM), lambda b, h, q: (b * nq + q, wq + h)),
            pl.BlockSpec((seq, DIFF_HEAD_DIM), lambda b, h, q: (b, wk + h)),
            pl.BlockSpec((seq, DIFF_HEAD_DIM), lambda b, h, q: (b, wv + h)),
            pl.BlockSpec((4, HEAD_DIM), lambda b, h, q: (0, 0)),
            pl.BlockSpec((1, DIFF_HEAD_DIM), lambda b, h, q: (0, 0)),
        ],
        out_specs=pl.BlockSpec((t, DIFF_HEAD_DIM), lambda b, h, q: (b * nq + q, h)),
        out_shape=jax.ShapeDtypeStruct((batch * seq, DIFF_WIDTH), BF16),
        scratch_shapes=[
            pltpu.VMEM((nq, vrows, t), BF16),
            pltpu.VMEM((2, t, t), F32),
            pltpu.VMEM((2, t, t), BF16),
            pltpu.VMEM((t, LANES), F32),
            pltpu.VMEM((2, 1, t), F32),
            pltpu.VMEM((2, 1, t), F32),
            pltpu.VMEM((2, vrows, t), F32),
        ],
        compiler_params=pltpu.CompilerParams(
            dimension_semantics=("parallel", "parallel", "arbitrary"),
            vmem_limit_bytes=VMEM_LIMIT_BYTES),
        name="diff",
    )(proj2d, proj2d, proj2d, lamv, g_subln)


def _out_kernel(ya_ref, yb_ref, zb0_ref, zb1_ref, zb2_ref, gla_ref, glb_ref, x_ref, mod_ref,
                gf_ref, wd_ref, wf_ref, wo_ref, out_ref, ybz_ref):
    zb_refs = (zb0_ref, zb1_ref, zb2_ref)
    wz = zb0_ref.shape[1]
    for i in range(3):
        sl = slice(i * wz, (i + 1) * wz)
        ybz_ref[:, sl] = (yb_ref[:, sl].astype(F32) * _silu(zb_refs[i][...].astype(F32))).astype(BF16)
    pa = jnp.dot(ya_ref[...], wd_ref[...], preferred_element_type=F32)
    pb = jnp.dot(ybz_ref[...], wf_ref[...], preferred_element_type=F32)
    merged = (_sigmoid(gla_ref[...].astype(F32)) * pa + _sigmoid(glb_ref[...].astype(F32)) * pb)
    delta = jnp.dot(merged.astype(BF16), wo_ref[...], preferred_element_type=F32)
    y = x_ref[...] + mod_ref[0, 2:3, :] * delta
    ms = jnp.mean(y * y, axis=-1, keepdims=True)
    out_ref[...] = y * lax.rsqrt(ms + EPS) * gf_ref[...]


def _const_spec(shape):
    return pl.BlockSpec(shape, lambda i: (0,) * len(shape), pipeline_mode=pl.Buffered(1))


def _out(ya, yb, proj2d, x2d, mod, g_final, wd, wf, wo, seq):
    t, d = x2d.shape
    tm = OUT_TM
    tiles_per_batch = seq // tm
    wz = 512
    zb_blk = OFF_ZB // wz
    row = lambda w: pl.BlockSpec((tm, w), lambda i: (i, 0))
    in_specs = (
        [row(DIL_WIDTH), row(DIFF_WIDTH)]
        + [pl.BlockSpec((tm, wz), functools.partial(lambda i, k: (i, zb_blk + k), k=k))
           for k in range(3)]
        + [pl.BlockSpec((tm, d), lambda i: (i, OFF_GLA // D_MODEL)),
           pl.BlockSpec((tm, d), lambda i: (i, OFF_GLB // D_MODEL)),
           row(d),
           pl.BlockSpec((1, 3, d), lambda i: (i // tiles_per_batch, 0, 0)),
           _const_spec((1, d)),
           _const_spec(wd.shape), _const_spec(wf.shape), _const_spec(wo.shape)])
    return pl.pallas_call(
        _out_kernel,
        grid=(t // tm,),
        in_specs=in_specs,
        out_specs=row(d),
        out_shape=jax.ShapeDtypeStruct((t, d), F32),
        scratch_shapes=[pltpu.VMEM((tm, DIFF_WIDTH), BF16)],
        compiler_params=pltpu.CompilerParams(
            dimension_semantics=("parallel",),
            vmem_limit_bytes=VMEM_LIMIT_BYTES),
        name="out",
    )(ya, yb, proj2d, proj2d, proj2d, proj2d, proj2d, x2d, mod, g_final, wd, wf, wo)


def _q_column_scale():
    cs = np.ones((1, N_IN), np.float32)
    cs[:, OFF_QA:OFF_QA + DIL_QKV] = Q_PRESCALE
    cs[:, OFF_QB:OFF_QB + DIFF_WIDTH] = Q_PRESCALE
    return jnp.asarray(cs)


def kernel(x, c, w_ada, b_ada, g_norm, w_in, w_o_dil, w_o_diff, w_out,
           lambda_q1, lambda_k1, lambda_q2, lambda_k2, g_subln, g_final):
    batch, seq, d = x.shape
    assert w_ada.shape[0] == 1 and d == D_MODEL and w_in.shape[2] == N_IN
    assert seq % DIL_TT == 0 and seq % PROJ_TM == 0 and seq % DIFF_T == 0
    rows = -(-batch // 8) * 8
    c_pad = jnp.pad(c, ((0, rows - batch), (0, 0)))
    x2d = x.reshape(batch * seq, d)
    ada = _ada(c_pad, w_ada[0], b_ada[0].reshape(1, 3 * d))[:batch]
    mod = ada.reshape(batch, 3, d)
    proj2d = _proj(x2d, mod, g_norm[0].reshape(1, d), w_in[0].astype(BF16), _q_column_scale(), seq)
    ya = _dil(proj2d, batch, seq)
    lamv = jnp.stack([lambda_q1[0], lambda_k1[0], lambda_q2[0], lambda_k2[0]])
    yb = _diff(proj2d, lamv, g_subln[0].reshape(1, DIFF_HEAD_DIM), batch, seq)
    out = _out(ya, yb, proj2d, x2d, mod, g_final.reshape(1, d), w_o_dil[0].astype(BF16),
               w_o_diff[0].astype(BF16), w_out[0].astype(BF16), seq)
    return out.reshape(batch, seq, d)
```

```python
import functools
import math

import numpy as np
import jax
import jax.numpy as jnp
from jax import lax
from jax.experimental import pallas as pl
from jax.experimental.pallas import tpu as pltpu

F32 = jnp.float32
BF16 = jnp.bfloat16

D_MODEL = 2048
HEAD_DIM = 128
LANES = 128
DIL_GROUPS = ((128, 1), (512, 4), (2048, 16))
DIL_HEADS_PER_GROUP = 8
DIL_N_HEADS = DIL_HEADS_PER_GROUP * len(DIL_GROUPS)
DIL_QKV = DIL_N_HEADS * HEAD_DIM
DIL_WIDTH = DIL_HEADS_PER_GROUP * HEAD_DIM
DIL_SPAN = 128
DIFF_N_HEADS = 6
DIFF_HEAD_DIM = 2 * HEAD_DIM
DIFF_WIDTH = DIFF_N_HEADS * DIFF_HEAD_DIM
EPS = 1e-6
NEG_INF = -1e30
IN_SIZES = (DIL_QKV, DIL_QKV, DIL_QKV, DIL_WIDTH,
            DIFF_WIDTH, DIFF_WIDTH, DIFF_WIDTH, DIFF_WIDTH, D_MODEL, D_MODEL)
N_IN = sum(IN_SIZES)
(OFF_QA, OFF_KA, OFF_VA, OFF_ZA, OFF_QB, OFF_KB, OFF_VB, OFF_ZB, OFF_GLA, OFF_GLB) = (
    int(v) for v in np.concatenate([[0], np.cumsum(IN_SIZES)[:-1]]))
LOG2E = 1.4426950408889634
Q_PRESCALE = HEAD_DIM ** -0.5 * LOG2E
LAMBDA_INIT = 0.8 - 0.6 * math.exp(-0.3 * 0)

VMEM_LIMIT_BYTES = 56 * 1024 * 1024

PROJ_TM = 1024
PROJ_TN = 2048
PROJ_NORM_CHUNK = 32
DIL_TT = DIL_SPAN * DIL_GROUPS[-1][1]
DIL_UNITS = DIL_TT // DIL_SPAN
DIL_STAGE_ROWS = 256
DIL_MERGE_CHUNK = 64
DIFF_T = 1024
DIFF_DIAG_W = 256
DIFF_ONES_ROWS = 16
OUT_TM = 256


def _alibi_slopes(n):
    return [float(2.0 ** (-8.0 * (i + 1) / n)) for i in range(n)]


def _dot_nt(a, b):
    return lax.dot_general(a, b, (((1,), (1,)), ((), ())), preferred_element_type=F32)


def _silu(z):
    return z / (1.0 + jnp.exp(-z))


def _sigmoid(z):
    return 1.0 / (1.0 + jnp.exp(-z))


def _select_by_index(idx, values):
    out = jnp.float32(values[0])
    for i in range(1, len(values)):
        out = jnp.where(idx == i, jnp.float32(values[i]), out)
    return out


def _ada_kernel(c_ref, w_ref, b_ref, o_ref):
    c = c_ref[...]
    o_ref[...] = jnp.dot(_silu(c), w_ref[...], preferred_element_type=F32,
                         precision=lax.Precision.HIGHEST) + b_ref[...]


def _ada(c_pad, w_ada, b_ada):
    rows, d = c_pad.shape
    n = w_ada.shape[1]
    tn = 512
    return pl.pallas_call(
        _ada_kernel,
        grid=(n // tn,),
        in_specs=[pl.BlockSpec((rows, d), lambda j: (0, 0)),
                  pl.BlockSpec((d, tn), lambda j: (0, j)),
                  pl.BlockSpec((1, tn), lambda j: (0, j))],
        out_specs=pl.BlockSpec((rows, tn), lambda j: (0, j)),
        out_shape=jax.ShapeDtypeStruct((rows, n), F32),
        name="ada",
    )(c_pad, w_ada, b_ada)


def _proj_kernel(x_ref, mod_ref, g_ref, w_ref, cs_ref, o_ref, hn_ref):
    @pl.when(pl.program_id(1) == 0)
    def _():
        shift = mod_ref[0, 0:1, :]
        mult = g_ref[...] * (1.0 + mod_ref[0, 1:2, :])

        def body(i, carry):
            r = pl.multiple_of(i * PROJ_NORM_CHUNK, PROJ_NORM_CHUNK)
            xx = x_ref[pl.ds(r, PROJ_NORM_CHUNK), :]
            ms = jnp.mean(xx * xx, axis=-1, keepdims=True)
            hn_ref[pl.ds(r, PROJ_NORM_CHUNK), :] = (
                xx * lax.rsqrt(ms + EPS) * mult + shift).astype(BF16)
            return carry

        lax.fori_loop(0, x_ref.shape[0] // PROJ_NORM_CHUNK, body, 0)

    acc = jnp.dot(hn_ref[...], w_ref[...], preferred_element_type=F32)
    o_ref[...] = (acc * cs_ref[...]).astype(BF16)


def _proj(x2d, mod, g_norm, w_in_bf16, col_scale, seq):
    t, d = x2d.shape
    n = w_in_bf16.shape[1]
    tm, tn = PROJ_TM, PROJ_TN
    tiles_per_batch = seq // tm
    return pl.pallas_call(
        _proj_kernel,
        grid=(t // tm, n // tn),
        in_specs=[pl.BlockSpec((tm, d), lambda i, j: (i, 0)),
                  pl.BlockSpec((1, 3, d), lambda i, j: (i // tiles_per_batch, 0, 0)),
                  pl.BlockSpec((1, d), lambda i, j: (0, 0)),
                  pl.BlockSpec((d, tn), lambda i, j: (0, j)),
                  pl.BlockSpec((1, tn), lambda i, j: (0, j))],
        out_specs=pl.BlockSpec((tm, tn), lambda i, j: (i, j)),
        out_shape=jax.ShapeDtypeStruct((t, n), BF16),
        scratch_shapes=[pltpu.VMEM((tm, d), BF16)],
        compiler_params=pltpu.CompilerParams(
            dimension_semantics=("parallel", "arbitrary"),
            vmem_limit_bytes=VMEM_LIMIT_BYTES),
        name="proj",
    )(x2d, mod, g_norm, w_in_bf16, col_scale)


def _gather_residues(src_ref, src32, dst_ref, r, n_u, dst_pitch, dst_off):
    n32 = n_u * r // 2
    for c0 in range(0, n32, DIL_STAGE_ROWS):
        src32[c0:c0 + DIL_STAGE_ROWS, :] = pltpu.bitcast(
            src_ref[2 * c0:2 * (c0 + DIL_STAGE_ROWS), :], jnp.uint32)
    for p in range(r):
        for c0 in range(0, n_u, DIL_SPAN):
            w = src32[pl.ds(p // 2 + (r // 2) * c0, DIL_SPAN, stride=r // 2), :]
            w = (w << 16) if p % 2 == 0 else (w & jnp.uint32(0xFFFF0000))
            row = p * dst_pitch + dst_off + c0
            dst_ref[row:row + DIL_SPAN, 0:LANES] = lax.bitcast_convert_type(w, F32).astype(BF16)


def _dil_kernel(*refs):
    ins, y_ref = refs[:10], refs[10]
    qd, w32, bm_ref = refs[11:14]
    kd, vd = refs[14:17], refs[17:20]
    o_scr, lse_scr = refs[20:23], refs[23:26]
    z_ref = ins[9]
    h = pl.program_id(1)
    first_tile = pl.program_id(2) == 0

    iq = lax.broadcasted_iota(jnp.int32, (DIL_SPAN, 2 * DIL_SPAN), 0)
    jk = lax.broadcasted_iota(jnp.int32, (DIL_SPAN, 2 * DIL_SPAN), 1)
    dist = (iq - jk + DIL_SPAN).astype(F32)
    valid = jnp.logical_and(jk >= iq, jk <= iq + DIL_SPAN)
    valid_first = jnp.logical_and(valid, jk >= DIL_SPAN)

    slopes = _alibi_slopes(DIL_N_HEADS)
    for g, (_, r) in enumerate(DIL_GROUPS):
        q_ref, kc_ref, vc_ref = ins[3 * g:3 * g + 3]
        kd_g, vd_g = kd[g], vd[g]
        nb = DIL_UNITS // r
        kpitch = (nb + 1) * DIL_SPAN
        coef = _select_by_index(
            h, [slopes[g * DIL_HEADS_PER_GROUP + i] * r * LOG2E for i in range(DIL_HEADS_PER_GROUP)])
        bias = -coef * dist
        bm_ref[0] = jnp.where(valid, bias, NEG_INF)
        bm_ref[1] = jnp.where(valid_first, bias, NEG_INF)

        @pl.when(first_tile)
        def _(r=r, kpitch=kpitch, kd_g=kd_g, vd_g=vd_g):
            vd_g[:, LANES:] = jnp.ones((vd_g.shape[0], LANES), BF16)
            for p in range(r):
                kd_g[p * kpitch:p * kpitch + DIL_SPAN, :] = jnp.zeros((DIL_SPAN, LANES), BF16)
                vd_g[p * kpitch:p * kpitch + DIL_SPAN, 0:LANES] = jnp.zeros((DIL_SPAN, LANES), BF16)

        if r == 1:
            q_src = q_ref
            kd_g[DIL_SPAN:DIL_SPAN + DIL_TT, :] = kc_ref[...]
            vd_g[DIL_SPAN:DIL_SPAN + DIL_TT, 0:LANES] = vc_ref[...]
        else:
            q_src = qd
            _gather_residues(q_ref, w32, qd, r, nb * DIL_SPAN, nb * DIL_SPAN, 0)
            _gather_residues(kc_ref, w32, kd_g, r, nb * DIL_SPAN, kpitch, DIL_SPAN)
            _gather_residues(vc_ref, w32, vd_g, r, nb * DIL_SPAN, kpitch, DIL_SPAN)

        for p in range(r):
            for ub in range(nb):
                rq = (p * nb + ub) * DIL_SPAN
                rk = p * kpitch + ub * DIL_SPAN
                bm = bm_ref[jnp.where(first_tile, 1, 0)] if ub == 0 else bm_ref[0]
                s = _dot_nt(q_src[rq:rq + DIL_SPAN, :], kd_g[rk:rk + 2 * DIL_SPAN, :]) + bm
                m = jnp.max(s, axis=-1, keepdims=True)
                e = jnp.exp2(s - m)
                ol = jnp.dot(e.astype(BF16), vd_g[rk:rk + 2 * DIL_SPAN, :], preferred_element_type=F32)
                l = ol[:, LANES:]
                o = ol[:, 0:LANES] * (1.0 / l)
                lse2 = m + jnp.log(l) * LOG2E
                row0 = ub * (DIL_SPAN * r) + p
                rows = pl.ds(row0, DIL_SPAN, stride=r) if r > 1 else pl.ds(row0, DIL_SPAN)
                o_scr[g][rows, :] = o
                lse_scr[g][rows, :] = lse2

        for p in range(r):
            last = p * kpitch + nb * DIL_SPAN
            kd_g[p * kpitch:p * kpitch + DIL_SPAN, :] = kd_g[last:last + DIL_SPAN, :]
            vd_g[p * kpitch:p * kpitch + DIL_SPAN, 0:LANES] = vd_g[last:last + DIL_SPAN, 0:LANES]

    def merge(i, carry):
        rows = pl.ds(pl.multiple_of(i * DIL_MERGE_CHUNK, DIL_MERGE_CHUNK), DIL_MERGE_CHUNK)
        l0, l1, l2 = lse_scr[0][rows, :], lse_scr[1][rows, :], lse_scr[2][rows, :]
        mx = jnp.maximum(jnp.maximum(l0, l1), l2)
        e0, e1, e2 = jnp.exp2(l0 - mx), jnp.exp2(l1 - mx), jnp.exp2(l2 - mx)
        om = (e0 * o_scr[0][rows, :] + e1 * o_scr[1][rows, :] + e2 * o_scr[2][rows, :]) / (e0 + e1 + e2)
        y_ref[rows, :] = (om * _silu(z_ref[rows, :].astype(F32))).astype(BF16)
        return carry

    lax.fori_loop(0, DIL_TT // DIL_MERGE_CHUNK, merge, 0, unroll=2)


def _dil(proj2d, batch, seq):
    tt = DIL_TT
    tiles = seq // tt

    def cur(col):
        return pl.BlockSpec((tt, LANES), lambda b, h, t: (b * tiles + t, col + h))

    in_specs = []
    for g in range(len(DIL_GROUPS)):
        in_specs += [cur((off + g * DIL_WIDTH) // LANES) for off in (OFF_QA, OFF_KA, OFF_VA)]
    in_specs.append(cur(OFF_ZA // LANES))
    k_scratch = [pltpu.VMEM((tt + DIL_SPAN * r, LANES), BF16) for _, r in DIL_GROUPS]
    v_scratch = [pltpu.VMEM((tt + DIL_SPAN * r, 2 * LANES), BF16) for _, r in DIL_GROUPS]
    return pl.pallas_call(
        _dil_kernel,
        grid=(batch, DIL_HEADS_PER_GROUP, tiles),
        in_specs=in_specs,
        out_specs=pl.BlockSpec((tt, LANES), lambda b, h, t: (b * tiles + t, h)),
        out_shape=jax.ShapeDtypeStruct((batch * seq, DIL_WIDTH), BF16),
        scratch_shapes=(
            [pltpu.VMEM((tt, LANES), BF16), pltpu.VMEM((tt // 2, LANES), jnp.uint32),
             pltpu.VMEM((2, DIL_SPAN, 2 * DIL_SPAN), F32)]
            + k_scratch + v_scratch + [pltpu.VMEM((tt, LANES), F32)] * 6),
        compiler_params=pltpu.CompilerParams(
            dimension_semantics=("parallel", "parallel", "arbitrary"),
            vmem_limit_bytes=VMEM_LIMIT_BYTES),
        name="dil",
    )(*([proj2d] * 10))


def _diff_kernel(q_ref, k_ref, v_ref, lamv_ref, gsub_ref, o_ref,
                 vt_ref, st_ref, pt_ref, cb_ref, m_ref, a_ref, acc_ref, *, slopes2):
    h = pl.program_id(1)
    qi = pl.program_id(2)
    t = q_ref.shape[0]
    nk = vt_ref.shape[0]
    slope2 = _select_by_index(h, slopes2)

    @pl.when(qi == 0)
    def _():
        for j in range(nk):
            vt_ref[j, 0:DIFF_HEAD_DIM, :] = v_ref[j * t:(j + 1) * t, :].astype(F32).T.astype(BF16)
            vt_ref[j, DIFF_HEAD_DIM:, :] = jnp.ones((DIFF_ONES_ROWS, t), BF16)

    cb_ref[...] = slope2 * lax.broadcasted_iota(jnp.int32, (t, LANES), 0).astype(F32)
    m_ref[...] = jnp.full(m_ref.shape, NEG_INF, F32)
    acc_ref[...] = jnp.zeros(acc_ref.shape, F32)

    def block(kj):
        k0 = pl.multiple_of(kj * t, t)
        cblk = slope2 * ((kj - qi) * t).astype(F32)
        for c in range(2):
            sl = slice(c * HEAD_DIM, (c + 1) * HEAD_DIM)
            st_ref[c] = _dot_nt(k_ref[pl.ds(k0, t), sl], q_ref[:, sl])
        for c in range(2):
            for cg in range(t // LANES):
                cols = slice(cg * LANES, (cg + 1) * LANES)
                x = st_ref[c, :, cols] + cb_ref[...]
                m_old = m_ref[c, :, cols]
                m_new = jnp.maximum(m_old, jnp.max(x, axis=0, keepdims=True) + cblk)
                m_ref[c, :, cols] = m_new
                a_ref[c, :, cols] = jnp.exp2(m_old - m_new)
                pt_ref[c, :, cols] = jnp.exp2((x - (m_new - cblk)).astype(BF16))
        vt = vt_ref[kj]
        for c in range(2):
            acc_ref[c] = acc_ref[c] * a_ref[c] + jnp.dot(vt, pt_ref[c], preferred_element_type=F32)

    def diagonal_block():
        k0 = pl.multiple_of(qi * t, t)
        w = DIFF_DIAG_W
        for c in range(2):
            sl = slice(c * HEAD_DIM, (c + 1) * HEAD_DIM)
            for qh in range(t // w):
                kend = (qh + 1) * w
                qs = slice(qh * w, kend)
                st_ref[c, 0:kend, qs] = _dot_nt(k_ref[pl.ds(k0, kend), sl], q_ref[qs, sl])
        for c in range(2):
            for cg in range(t // LANES):
                cols = slice(cg * LANES, (cg + 1) * LANES)
                kend = (cg * LANES // w + 1) * w
                key = lax.broadcasted_iota(jnp.int32, (kend, LANES), 0)
                qry = lax.broadcasted_iota(jnp.int32, (kend, LANES), 1) + cg * LANES
                x = jnp.where(key <= qry, st_ref[c, 0:kend, cols] + cb_ref[0:kend, :], NEG_INF)
                m_old = m_ref[c, :, cols]
                m_new = jnp.maximum(m_old, jnp.max(x, axis=0, keepdims=True))
                m_ref[c, :, cols] = m_new
                a_ref[c, :, cols] = jnp.exp2(m_old - m_new)
                pt_ref[c, 0:kend, cols] = jnp.exp2((x - m_new).astype(BF16))
        vt = vt_ref[qi]
        for c in range(2):
            for qh in range(t // w):
                kend = (qh + 1) * w
                qs = slice(qh * w, kend)
                acc_ref[c, :, qs] = (acc_ref[c, :, qs] * a_ref[c, :, qs]
                                     + jnp.dot(vt[:, 0:kend], pt_ref[c, 0:kend, qs],
                                               preferred_element_type=F32))

    def body(kj, carry):
        block(kj)
        return carry

    lax.fori_loop(0, qi, body, 0)
    diagonal_block()

    lamv = lamv_ref[...]
    lam = (jnp.exp(jnp.sum(lamv[0:1] * lamv[1:2], axis=-1, keepdims=True))
           - jnp.exp(jnp.sum(lamv[2:3] * lamv[3:4], axis=-1, keepdims=True)) + LAMBDA_INIT)
    d = DIFF_HEAD_DIM
    ot = (acc_ref[0, 0:d, :] * (1.0 / acc_ref[0, d:d + 1, :])
          - lam * (acc_ref[1, 0:d, :] * (1.0 / acc_ref[1, d:d + 1, :])))
    ms = jnp.mean(ot * ot, axis=0, keepdims=True)
    ot = ot * lax.rsqrt(ms + EPS)
    o_ref[...] = (ot.T * gsub_ref[...] * (1.0 - LAMBDA_INIT)).astype(o_ref.dtype)


def _diff(proj2d, lamv, g_subln, batch, seq):
    t = DIFF_T
    nq = seq // t
    wq, wk, wv = OFF_QB // DIFF_HEAD_DIM, OFF_KB // DIFF_HEAD_DIM, OFF_VB // DIFF_HEAD_DIM
    slopes2 = tuple(s * LOG2E for s in _alibi_slopes(DIFF_N_HEADS))
    vrows = DIFF_HEAD_DIM + DIFF_ONES_ROWS
    return pl.pallas_call(
        functools.partial(_diff_kernel, slopes2=slopes2),
        grid=(batch, DIFF_N_HEADS, nq),
        in_specs=[
            pl.BlockSpec((t, DIFF_HEAD_DIM), lambda b, h, q: (b * nq + q, wq + h)),
            pl.BlockSpec((seq, DIFF_HEAD_DIM), lambda b, h, q: (b, wk + h)),
            pl.BlockSpec((seq, DIFF_HEAD_DIM), lambda b, h, q: (b, wv + h)),
            pl.BlockSpec((4, HEAD_DIM), lambda b, h, q: (0, 0)),
            pl.BlockSpec((1, DIFF_HEAD_DIM), lambda b, h, q: (0, 0)),
        ],
        out_specs=pl.BlockSpec((t, DIFF_HEAD_DIM), lambda b, h, q: (b * nq + q, h)),
        out_shape=jax.ShapeDtypeStruct((batch * seq, DIFF_WIDTH), BF16),
        scratch_shapes=[
            pltpu.VMEM((nq, vrows, t), BF16),
            pltpu.VMEM((2, t, t), F32),
            pltpu.VMEM((2, t, t), BF16),
            pltpu.VMEM((t, LANES), F32),
            pltpu.VMEM((2, 1, t), F32),
            pltpu.VMEM((2, 1, t), F32),
            pltpu.VMEM((2, vrows, t), F32),
        ],
        compiler_params=pltpu.CompilerParams(
            dimension_semantics=("parallel", "parallel", "arbitrary"),
            vmem_limit_bytes=VMEM_LIMIT_BYTES),
        name="diff",
    )(proj2d, proj2d, proj2d, lamv, g_subln)


def _out_kernel(ya_ref, yb_ref, zb0_ref, zb1_ref, zb2_ref, gla_ref, glb_ref, x_ref, mod_ref,
                gf_ref, wd_ref, wf_ref, wo_ref, out_ref, ybz_ref):
    zb_refs = (zb0_ref, zb1_ref, zb2_ref)
    wz = zb0_ref.shape[1]
    for i in range(3):
        sl = slice(i * wz, (i + 1) * wz)
        ybz_ref[:, sl] = (yb_ref[:, sl].astype(F32) * _silu(zb_refs[i][...].astype(F32))).astype(BF16)
    pa = jnp.dot(ya_ref[...], wd_ref[...], preferred_element_type=F32)
    pb = jnp.dot(ybz_ref[...], wf_ref[...], preferred_element_type=F32)
    merged = (_sigmoid(gla_ref[...].astype(F32)) * pa + _sigmoid(glb_ref[...].astype(F32)) * pb)
    delta = jnp.dot(merged.astype(BF16), wo_ref[...], preferred_element_type=F32)
    y = x_ref[...] + mod_ref[0, 2:3, :] * delta
    ms = jnp.mean(y * y, axis=-1, keepdims=True)
    out_ref[...] = y * lax.rsqrt(ms + EPS) * gf_ref[...]


def _const_spec(shape):
    return pl.BlockSpec(shape, lambda i: (0,) * len(shape), pipeline_mode=pl.Buffered(1))


def _out(ya, yb, proj2d, x2d, mod, g_final, wd, wf, wo, seq):
    t, d = x2d.shape
    tm = OUT_TM
    tiles_per_batch = seq // tm
    wz = 512
    zb_blk = OFF_ZB // wz
    row = lambda w: pl.BlockSpec((tm, w), lambda i: (i, 0))
    in_specs = (
        [row(DIL_WIDTH), row(DIFF_WIDTH)]
        + [pl.BlockSpec((tm, wz), functools.partial(lambda i, k: (i, zb_blk + k), k=k))
           for k in range(3)]
        + [pl.BlockSpec((tm, d), lambda i: (i, OFF_GLA // D_MODEL)),
           pl.BlockSpec((tm, d), lambda i: (i, OFF_GLB // D_MODEL)),
           row(d),
           pl.BlockSpec((1, 3, d), lambda i: (i // tiles_per_batch, 0, 0)),
           _const_spec((1, d)),
           _const_spec(wd.shape), _const_spec(wf.shape), _const_spec(wo.shape)])
    return pl.pallas_call(
        _out_kernel,
        grid=(t // tm,),
        in_specs=in_specs,
        out_specs=row(d),
        out_shape=jax.ShapeDtypeStruct((t, d), F32),
        scratch_shapes=[pltpu.VMEM((tm, DIFF_WIDTH), BF16)],
        compiler_params=pltpu.CompilerParams(
            dimension_semantics=("parallel",),
            vmem_limit_bytes=VMEM_LIMIT_BYTES),
        name="out",
    )(ya, yb, proj2d, proj2d, proj2d, proj2d, proj2d, x2d, mod, g_final, wd, wf, wo)


def _q_column_scale():
    cs = np.ones((1, N_IN), np.float32)
    cs[:, OFF_QA:OFF_QA + DIL_QKV] = Q_PRESCALE
    cs[:, OFF_QB:OFF_QB + DIFF_WIDTH] = Q_PRESCALE
    return jnp.asarray(cs)


def kernel(x, c, w_ada, b_ada, g_norm, w_in, w_o_dil, w_o_diff, w_out,
           lambda_q1, lambda_k1, lambda_q2, lambda_k2, g_subln, g_final):
    batch, seq, d = x.shape
    assert w_ada.shape[0] == 1 and d == D_MODEL and w_in.shape[2] == N_IN
    assert seq % DIL_TT == 0 and seq % PROJ_TM == 0 and seq % DIFF_T == 0
    rows = -(-batch // 8) * 8
    c_pad = jnp.pad(c, ((0, rows - batch), (0, 0)))
    x2d = x.reshape(batch * seq, d)
    ada = _ada(c_pad, w_ada[0], b_ada[0].reshape(1, 3 * d))[:batch]
    mod = ada.reshape(batch, 3, d)
    proj2d = _proj(x2d, mod, g_norm[0].reshape(1, d), w_in[0].astype(BF16), _q_column_scale(), seq)
    ya = _dil(proj2d, batch, seq)
    lamv = jnp.stack([lambda_q1[0], lambda_k1[0], lambda_q2[0], lambda_k2[0]])
    yb = _diff(proj2d, lamv, g_subln[0].reshape(1, DIFF_HEAD_DIM), batch, seq)
    out = _out(ya, yb, proj2d, x2d, mod, g_final.reshape(1, d), w_o_dil[0].astype(BF16),
               w_o_diff[0].astype(BF16), w_out[0].astype(BF16), seq)
    return out.reshape(batch, seq, d)
```

```python
import functools
import math

import numpy as np
import jax
import jax.numpy as jnp
from jax import lax
from jax.experimental import pallas as pl
from jax.experimental.pallas import tpu as pltpu

F32 = jnp.float32
BF16 = jnp.bfloat16

D_MODEL = 2048
HEAD_DIM = 128
LANES = 128
DIL_GROUPS = ((128, 1), (512, 4), (2048, 16))
DIL_HEADS_PER_GROUP = 8
DIL_N_HEADS = DIL_HEADS_PER_GROUP * len(DIL_GROUPS)
DIL_QKV = DIL_N_HEADS * HEAD_DIM
DIL_WIDTH = DIL_HEADS_PER_GROUP * HEAD_DIM
DIL_SPAN = 128
DIFF_N_HEADS = 6
DIFF_HEAD_DIM = 2 * HEAD_DIM
DIFF_WIDTH = DIFF_N_HEADS * DIFF_HEAD_DIM
EPS = 1e-6
NEG_INF = -1e30
IN_SIZES = (DIL_QKV, DIL_QKV, DIL_QKV, DIL_WIDTH,
            DIFF_WIDTH, DIFF_WIDTH, DIFF_WIDTH, DIFF_WIDTH, D_MODEL, D_MODEL)
N_IN = sum(IN_SIZES)
(OFF_QA, OFF_KA, OFF_VA, OFF_ZA, OFF_QB, OFF_KB, OFF_VB, OFF_ZB, OFF_GLA, OFF_GLB) = (
    int(v) for v in np.concatenate([[0], np.cumsum(IN_SIZES)[:-1]]))
LOG2E = 1.4426950408889634
Q_PRESCALE = HEAD_DIM ** -0.5 * LOG2E
LAMBDA_INIT = 0.8 - 0.6 * math.exp(-0.3 * 0)

VMEM_LIMIT_BYTES = 56 * 1024 * 1024

PROJ_TM = 1024
PROJ_TN = 2048
PROJ_NORM_CHUNK = 32
DIL_TT = DIL_SPAN * DIL_GROUPS[-1][1]
DIL_UNITS = DIL_TT // DIL_SPAN
DIL_STAGE_ROWS = 256
DIL_MERGE_CHUNK = 64
DIFF_T = 1024
DIFF_DIAG_W = 256
DIFF_ONES_ROWS = 16
OUT_TM = 256


def _alibi_slopes(n):
    return [float(2.0 ** (-8.0 * (i + 1) / n)) for i in range(n)]


def _dot_nt(a, b):
    return lax.dot_general(a, b, (((1,), (1,)), ((), ())), preferred_element_type=F32)


def _silu(z):
    return z / (1.0 + jnp.exp(-z))


def _sigmoid(z):
    return 1.0 / (1.0 + jnp.exp(-z))


def _select_by_index(idx, values):
    out = jnp.float32(values[0])
    for i in range(1, len(values)):
        out = jnp.where(idx == i, jnp.float32(values[i]), out)
    return out


def _ada_kernel(c_ref, w_ref, b_ref, o_ref):
    c = c_ref[...]
    o_ref[...] = jnp.dot(_silu(c), w_ref[...], preferred_element_type=F32,
                         precision=lax.Precision.HIGHEST) + b_ref[...]


def _ada(c_pad, w_ada, b_ada):
    rows, d = c_pad.shape
    n = w_ada.shape[1]
    tn = 512
    return pl.pallas_call(
        _ada_kernel,
        grid=(n // tn,),
        in_specs=[pl.BlockSpec((rows, d), lambda j: (0, 0)),
                  pl.BlockSpec((d, tn), lambda j: (0, j)),
                  pl.BlockSpec((1, tn), lambda j: (0, j))],
        out_specs=pl.BlockSpec((rows, tn), lambda j: (0, j)),
        out_shape=jax.ShapeDtypeStruct((rows, n), F32),
        name="ada",
    )(c_pad, w_ada, b_ada)


def _proj_kernel(x_ref, mod_ref, g_ref, w_ref, cs_ref, o_ref, hn_ref):
    @pl.when(pl.program_id(1) == 0)
    def _():
        shift = mod_ref[0, 0:1, :]
        mult = g_ref[...] * (1.0 + mod_ref[0, 1:2, :])

        def body(i, carry):
            r = pl.multiple_of(i * PROJ_NORM_CHUNK, PROJ_NORM_CHUNK)
            xx = x_ref[pl.ds(r, PROJ_NORM_CHUNK), :]
            ms = jnp.mean(xx * xx, axis=-1, keepdims=True)
            hn_ref[pl.ds(r, PROJ_NORM_CHUNK), :] = (
                xx * lax.rsqrt(ms + EPS) * mult + shift).astype(BF16)
            return carry

        lax.fori_loop(0, x_ref.shape[0] // PROJ_NORM_CHUNK, body, 0)

    acc = jnp.dot(hn_ref[...], w_ref[...], preferred_element_type=F32)
    o_ref[...] = (acc * cs_ref[...]).astype(BF16)


def _proj(x2d, mod, g_norm, w_in_bf16, col_scale, seq):
    t, d = x2d.shape
    n = w_in_bf16.shape[1]
    tm, tn = PROJ_TM, PROJ_TN
    tiles_per_batch = seq // tm
    return pl.pallas_call(
        _proj_kernel,
        grid=(t // tm, n // tn),
        in_specs=[pl.BlockSpec((tm, d), lambda i, j: (i, 0)),
                  pl.BlockSpec((1, 3, d), lambda i, j: (i // tiles_per_batch, 0, 0)),
                  pl.BlockSpec((1, d), lambda i, j: (0, 0)),
                  pl.BlockSpec((d, tn), lambda i, j: (0, j)),
                  pl.BlockSpec((1, tn), lambda i, j: (0, j))],
        out_specs=pl.BlockSpec((tm, tn), lambda i, j: (i, j)),
        out_shape=jax.ShapeDtypeStruct((t, n), BF16),
        scratch_shapes=[pltpu.VMEM((tm, d), BF16)],
        compiler_params=pltpu.CompilerParams(
            dimension_semantics=("parallel", "arbitrary"),
            vmem_limit_bytes=VMEM_LIMIT_BYTES),
        name="proj",
    )(x2d, mod, g_norm, w_in_bf16, col_scale)


def _gather_residues(src_ref, src32, dst_ref, r, n_u, dst_pitch, dst_off):
    n32 = n_u * r // 2
    for c0 in range(0, n32, DIL_STAGE_ROWS):
        src32[c0:c0 + DIL_STAGE_ROWS, :] = pltpu.bitcast(
            src_ref[2 * c0:2 * (c0 + DIL_STAGE_ROWS), :], jnp.uint32)
    for p in range(r):
        for c0 in range(0, n_u, DIL_SPAN):
            w = src32[pl.ds(p // 2 + (r // 2) * c0, DIL_SPAN, stride=r // 2), :]
            w = (w << 16) if p % 2 == 0 else (w & jnp.uint32(0xFFFF0000))
            row = p * dst_pitch + dst_off + c0
            dst_ref[row:row + DIL_SPAN, :] = lax.bitcast_convert_type(w, F32).astype(BF16)


def _dil_kernel(*refs):
    ins, y_ref = refs[:10], refs[10]
    qd, w32, bm_ref = refs[11:14]
    kd, vd = refs[14:17], refs[17:20]
    o_scr, lse_scr = refs[20:23], refs[23:26]
    z_ref = ins[9]
    h = pl.program_id(1)
    first_tile = pl.program_id(2) == 0

    iq = lax.broadcasted_iota(jnp.int32, (DIL_SPAN, 2 * DIL_SPAN), 0)
    jk = lax.broadcasted_iota(jnp.int32, (DIL_SPAN, 2 * DIL_SPAN), 1)
    dist = (iq - jk + DIL_SPAN).astype(F32)
    valid = jnp.logical_and(jk >= iq, jk <= iq + DIL_SPAN)
    valid_first = jnp.logical_and(valid, jk >= DIL_SPAN)

    slopes = _alibi_slopes(DIL_N_HEADS)
    for g, (_, r) in enumerate(DIL_GROUPS):
        q_ref, kc_ref, vc_ref = ins[3 * g:3 * g + 3]
        kd_g, vd_g = kd[g], vd[g]
        nb = DIL_UNITS // r
        kpitch = (nb + 1) * DIL_SPAN
        coef = _select_by_index(
            h, [slopes[g * DIL_HEADS_PER_GROUP + i] * r * LOG2E for i in range(DIL_HEADS_PER_GROUP)])
        bias = -coef * dist
        bm_ref[0] = jnp.where(valid, bias, NEG_INF)
        bm_ref[1] = jnp.where(valid_first, bias, NEG_INF)

        @pl.when(first_tile)
        def _(r=r, kpitch=kpitch, kd_g=kd_g, vd_g=vd_g):
            for p in range(r):
                kd_g[p * kpitch:p * kpitch + DIL_SPAN, :] = jnp.zeros((DIL_SPAN, LANES), BF16)
                vd_g[p * kpitch:p * kpitch + DIL_SPAN, :] = jnp.zeros((DIL_SPAN, LANES), BF16)

        if r == 1:
            q_src = q_ref
            kd_g[DIL_SPAN:DIL_SPAN + DIL_TT, :] = kc_ref[...]
            vd_g[DIL_SPAN:DIL_SPAN + DIL_TT, :] = vc_ref[...]
        else:
            q_src = qd
            _gather_residues(q_ref, w32, qd, r, nb * DIL_SPAN, nb * DIL_SPAN, 0)
            _gather_residues(kc_ref, w32, kd_g, r, nb * DIL_SPAN, kpitch, DIL_SPAN)
            _gather_residues(vc_ref, w32, vd_g, r, nb * DIL_SPAN, kpitch, DIL_SPAN)

        for p in range(r):
            for ub in range(nb):
                rq = (p * nb + ub) * DIL_SPAN
                rk = p * kpitch + ub * DIL_SPAN
                bm = bm_ref[jnp.where(first_tile, 1, 0)] if ub == 0 else bm_ref[0]
                s = _dot_nt(q_src[rq:rq + DIL_SPAN, :], kd_g[rk:rk + 2 * DIL_SPAN, :]) + bm
                m = jnp.max(s, axis=-1, keepdims=True)
                e = jnp.exp2(s - m)
                l = jnp.sum(e, axis=-1, keepdims=True)
                o = jnp.dot(e.astype(BF16), vd_g[rk:rk + 2 * DIL_SPAN, :],
                            preferred_element_type=F32) * (1.0 / l)
                lse2 = jnp.broadcast_to(m + jnp.log(l) * LOG2E, (DIL_SPAN, LANES))
                row0 = ub * (DIL_SPAN * r) + p
                rows = pl.ds(row0, DIL_SPAN, stride=r) if r > 1 else pl.ds(row0, DIL_SPAN)
                o_scr[g][rows, :] = o
                lse_scr[g][rows, :] = lse2

        for p in range(r):
            last = p * kpitch + nb * DIL_SPAN
            kd_g[p * kpitch:p * kpitch + DIL_SPAN, :] = kd_g[last:last + DIL_SPAN, :]
            vd_g[p * kpitch:p * kpitch + DIL_SPAN, :] = vd_g[last:last + DIL_SPAN, :]

    def merge(i, carry):
        rows = pl.ds(pl.multiple_of(i * DIL_MERGE_CHUNK, DIL_MERGE_CHUNK), DIL_MERGE_CHUNK)
        l0, l1, l2 = lse_scr[0][rows, :], lse_scr[1][rows, :], lse_scr[2][rows, :]
        mx = jnp.maximum(jnp.maximum(l0, l1), l2)
        e0, e1, e2 = jnp.exp2(l0 - mx), jnp.exp2(l1 - mx), jnp.exp2(l2 - mx)
        om = (e0 * o_scr[0][rows, :] + e1 * o_scr[1][rows, :] + e2 * o_scr[2][rows, :]) / (e0 + e1 + e2)
        y_ref[rows, :] = (om * _silu(z_ref[rows, :].astype(F32))).astype(BF16)
        return carry

    lax.fori_loop(0, DIL_TT // DIL_MERGE_CHUNK, merge, 0, unroll=2)


def _dil(proj2d, batch, seq):
    tt = DIL_TT
    tiles = seq // tt

    def cur(col):
        return pl.BlockSpec((tt, LANES), lambda b, h, t: (b * tiles + t, col + h))

    in_specs = []
    for g in range(len(DIL_GROUPS)):
        in_specs += [cur((off + g * DIL_WIDTH) // LANES) for off in (OFF_QA, OFF_KA, OFF_VA)]
    in_specs.append(cur(OFF_ZA // LANES))
    kv_scratch = [pltpu.VMEM((tt + DIL_SPAN * r, LANES), BF16) for _, r in DIL_GROUPS]
    return pl.pallas_call(
        _dil_kernel,
        grid=(batch, DIL_HEADS_PER_GROUP, tiles),
        in_specs=in_specs,
        out_specs=pl.BlockSpec((tt, LANES), lambda b, h, t: (b * tiles + t, h)),
        out_shape=jax.ShapeDtypeStruct((batch * seq, DIL_WIDTH), BF16),
        scratch_shapes=(
            [pltpu.VMEM((tt, LANES), BF16), pltpu.VMEM((tt // 2, LANES), jnp.uint32),
             pltpu.VMEM((2, DIL_SPAN, 2 * DIL_SPAN), F32)]
            + kv_scratch + kv_scratch + [pltpu.VMEM((tt, LANES), F32)] * 6),
        compiler_params=pltpu.CompilerParams(
            dimension_semantics=("parallel", "parallel", "arbitrary"),
            vmem_limit_bytes=VMEM_LIMIT_BYTES),
        name="dil",
    )(*([proj2d] * 10))


def _diff_kernel(q_ref, k_ref, v_ref, lamv_ref, gsub_ref, o_ref,
                 vt_ref, st_ref, pt_ref, cb_ref, m_ref, a_ref, acc_ref, *, slopes2):
    h = pl.program_id(1)
    qi = pl.program_id(2)
    t = q_ref.shape[0]
    nk = vt_ref.shape[0]
    slope2 = _select_by_index(h, slopes2)

    @pl.when(qi == 0)
    def _():
        for j in range(nk):
            vt_ref[j, 0:DIFF_HEAD_DIM, :] = v_ref[j * t:(j + 1) * t, :].astype(F32).T.astype(BF16)
            vt_ref[j, DIFF_HEAD_DIM:, :] = jnp.ones((DIFF_ONES_ROWS, t), BF16)

    cb_ref[...] = slope2 * lax.broadcasted_iota(jnp.int32, (t, LANES), 0).astype(F32)
    m_ref[...] = jnp.full(m_ref.shape, NEG_INF, F32)
    acc_ref[...] = jnp.zeros(acc_ref.shape, F32)

    def block(kj):
        k0 = pl.multiple_of(kj * t, t)
        cblk = slope2 * ((kj - qi) * t).astype(F32)
        for c in range(2):
            sl = slice(c * HEAD_DIM, (c + 1) * HEAD_DIM)
            st_ref[c] = _dot_nt(k_ref[pl.ds(k0, t), sl], q_ref[:, sl])
        for c in range(2):
            for cg in range(t // LANES):
                cols = slice(cg * LANES, (cg + 1) * LANES)
                x = st_ref[c, :, cols] + cb_ref[...]
                m_old = m_ref[c, :, cols]
                m_new = jnp.maximum(m_old, jnp.max(x, axis=0, keepdims=True) + cblk)
                m_ref[c, :, cols] = m_new
                a_ref[c, :, cols] = jnp.exp2(m_old - m_new)
                pt_ref[c, :, cols] = jnp.exp2((x - (m_new - cblk)).astype(BF16))
        vt = vt_ref[kj]
        for c in range(2):
            acc_ref[c] = acc_ref[c] * a_ref[c] + jnp.dot(vt, pt_ref[c], preferred_element_type=F32)

    def diagonal_block():
        k0 = pl.multiple_of(qi * t, t)
        w = DIFF_DIAG_W
        for c in range(2):
            sl = slice(c * HEAD_DIM, (c + 1) * HEAD_DIM)
            for qh in range(t // w):
                kend = (qh + 1) * w
                qs = slice(qh * w, kend)
                st_ref[c, 0:kend, qs] = _dot_nt(k_ref[pl.ds(k0, kend), sl], q_ref[qs, sl])
        for c in range(2):
            for cg in range(t // LANES):
                cols = slice(cg * LANES, (cg + 1) * LANES)
                kend = (cg * LANES // w + 1) * w
                key = lax.broadcasted_iota(jnp.int32, (kend, LANES), 0)
                qry = lax.broadcasted_iota(jnp.int32, (kend, LANES), 1) + cg * LANES
                x = jnp.where(key <= qry, st_ref[c, 0:kend, cols] + cb_ref[0:kend, :], NEG_INF)
                m_old = m_ref[c, :, cols]
                m_new = jnp.maximum(m_old, jnp.max(x, axis=0, keepdims=True))
                m_ref[c, :, cols] = m_new
                a_ref[c, :, cols] = jnp.exp2(m_old - m_new)
                pt_ref[c, 0:kend, cols] = jnp.exp2((x - m_new).astype(BF16))
        vt = vt_ref[qi]
        for c in range(2):
            for qh in range(t // w):
                kend = (qh + 1) * w
                qs = slice(qh * w, kend)
                acc_ref[c, :, qs] = (acc_ref[c, :, qs] * a_ref[c, :, qs]
                                     + jnp.dot(vt[:, 0:kend], pt_ref[c, 0:kend, qs],
                                               preferred_element_type=F32))

    def body(kj, carry):
        block(kj)
        return carry

    lax.fori_loop(0, qi, body, 0)
    diagonal_block()

    lamv = lamv_ref[...]
    lam = (jnp.exp(jnp.sum(lamv[0:1] * lamv[1:2], axis=-1, keepdims=True))
           - jnp.exp(jnp.sum(lamv[2:3] * lamv[3:4], axis=-1, keepdims=True)) + LAMBDA_INIT)
    d = DIFF_HEAD_DIM
    ot = (acc_ref[0, 0:d, :] * (1.0 / acc_ref[0, d:d + 1, :])
          - lam * (acc_ref[1, 0:d, :] * (1.0 / acc_ref[1, d:d + 1, :])))
    ms = jnp.mean(ot * ot, axis=0, keepdims=True)
    ot = ot * lax.rsqrt(ms + EPS)
    o_ref[...] = (ot.T * gsub_ref[...] * (1.0 - LAMBDA_INIT)).astype(o_ref.dtype)


def _diff(proj2d, lamv, g_subln, batch, seq):
    t = DIFF_T
    nq = seq // t
    wq, wk, wv = OFF_QB // DIFF_HEAD_DIM, OFF_KB // DIFF_HEAD_DIM, OFF_VB // DIFF_HEAD_DIM
    slopes2 = tuple(s * LOG2E for s in _alibi_slopes(DIFF_N_HEADS))
    vrows = DIFF_HEAD_DIM + DIFF_ONES_ROWS
    return pl.pallas_call(
        functools.partial(_diff_kernel, slopes2=slopes2),
        grid=(batch, DIFF_N_HEADS, nq),
        in_specs=[
            pl.BlockSpec((t, DIFF_HEAD_DIM), lambda b, h, q: (b * nq + q, wq + h)),
            pl.BlockSpec((seq, DIFF_HEAD_DIM), lambda b, h, q: (b, wk + h)),
            pl.BlockSpec((seq, DIFF_HEAD_DIM), lambda b, h, q: (b, wv + h)),
            pl.BlockSpec((4, HEAD_DIM), lambda b, h, q: (0, 0)),
            pl.BlockSpec((1, DIFF_HEAD_DIM), lambda b, h, q: (0, 0)),
        ],
        out_specs=pl.BlockSpec((t, DIFF_HEAD_DIM), lambda b, h, q: (b * nq + q, h)),
        out_shape=jax.ShapeDtypeStruct((batch * seq, DIFF_WIDTH), BF16),
        scratch_shapes=[
            pltpu.VMEM((nq, vrows, t), BF16),
            pltpu.VMEM((2, t, t), F32),
            pltpu.VMEM((2, t, t), BF16),
            pltpu.VMEM((t, LANES), F32),
            pltpu.VMEM((2, 1, t), F32),
            pltpu.VMEM((2, 1, t), F32),
            pltpu.VMEM((2, vrows, t), F32),
        ],
        compiler_params=pltpu.CompilerParams(
            dimension_semantics=("parallel", "parallel", "arbitrary"),
            vmem_limit_bytes=VMEM_LIMIT_BYTES),
        name="diff",
    )(proj2d, proj2d, proj2d, lamv, g_subln)


def _out_kernel(ya_ref, yb_ref, zb0_ref, zb1_ref, zb2_ref, gla_ref, glb_ref, x_ref, mod_ref,
                gf_ref, wd_ref, wf_ref, wo_ref, out_ref, ybz_ref):
    zb_refs = (zb0_ref, zb1_ref, zb2_ref)
    wz = zb0_ref.shape[1]
    for i in range(3):
        sl = slice(i * wz, (i + 1) * wz)
        ybz_ref[:, sl] = (yb_ref[:, sl].astype(F32) * _silu(zb_refs[i][...].astype(F32))).astype(BF16)
    pa = jnp.dot(ya_ref[...], wd_ref[...], preferred_element_type=F32)
    pb = jnp.dot(ybz_ref[...], wf_ref[...], preferred_element_type=F32)
    merged = (_sigmoid(gla_ref[...].astype(F32)) * pa + _sigmoid(glb_ref[...].astype(F32)) * pb)
    delta = jnp.dot(merged.astype(BF16), wo_ref[...], preferred_element_type=F32)
    y = x_ref[...] + mod_ref[0, 2:3, :] * delta
    ms = jnp.mean(y * y, axis=-1, keepdims=True)
    out_ref[...] = y * lax.rsqrt(ms + EPS) * gf_ref[...]


def _const_spec(shape):
    return pl.BlockSpec(shape, lambda i: (0,) * len(shape), pipeline_mode=pl.Buffered(1))


def _out(ya, yb, proj2d, x2d, mod, g_final, wd, wf, wo, seq):
    t, d = x2d.shape
    tm = OUT_TM
    tiles_per_batch = seq // tm
    wz = 512
    zb_blk = OFF_ZB // wz
    row = lambda w: pl.BlockSpec((tm, w), lambda i: (i, 0))
    in_specs = (
        [row(DIL_WIDTH), row(DIFF_WIDTH)]
        + [pl.BlockSpec((tm, wz), functools.partial(lambda i, k: (i, zb_blk + k), k=k))
           for k in range(3)]
        + [pl.BlockSpec((tm, d), lambda i: (i, OFF_GLA // D_MODEL)),
           pl.BlockSpec((tm, d), lambda i: (i, OFF_GLB // D_MODEL)),
           row(d),
           pl.BlockSpec((1, 3, d), lambda i: (i // tiles_per_batch, 0, 0)),
           _const_spec((1, d)),
           _const_spec(wd.shape), _const_spec(wf.shape), _const_spec(wo.shape)])
    return pl.pallas_call(
        _out_kernel,
        grid=(t // tm,),
        in_specs=in_specs,
        out_specs=row(d),
        out_shape=jax.ShapeDtypeStruct((t, d), F32),
        scratch_shapes=[pltpu.VMEM((tm, DIFF_WIDTH), BF16)],
        compiler_params=pltpu.CompilerParams(
            dimension_semantics=("parallel",),
            vmem_limit_bytes=VMEM_LIMIT_BYTES),
        name="out",
    )(ya, yb, proj2d, proj2d, proj2d, proj2d, proj2d, x2d, mod, g_final, wd, wf, wo)


def _q_column_scale():
    cs = np.ones((1, N_IN), np.float32)
    cs[:, OFF_QA:OFF_QA + DIL_QKV] = Q_PRESCALE
    cs[:, OFF_QB:OFF_QB + DIFF_WIDTH] = Q_PRESCALE
    return jnp.asarray(cs)


def kernel(x, c, w_ada, b_ada, g_norm, w_in, w_o_dil, w_o_diff, w_out,
           lambda_q1, lambda_k1, lambda_q2, lambda_k2, g_subln, g_final):
    batch, seq, d = x.shape
    assert w_ada.shape[0] == 1 and d == D_MODEL and w_in.shape[2] == N_IN
    assert seq % DIL_TT == 0 and seq % PROJ_TM == 0 and seq % DIFF_T == 0
    rows = -(-batch // 8) * 8
    c_pad = jnp.pad(c, ((0, rows - batch), (0, 0)))
    x2d = x.reshape(batch * seq, d)
    ada = _ada(c_pad, w_ada[0], b_ada[0].reshape(1, 3 * d))[:batch]
    mod = ada.reshape(batch, 3, d)
    proj2d = _proj(x2d, mod, g_norm[0].reshape(1, d), w_in[0].astype(BF16), _q_column_scale(), seq)
    ya = _dil(proj2d, batch, seq)
    lamv = jnp.stack([lambda_q1[0], lambda_k1[0], lambda_q2[0], lambda_k2[0]])
    yb = _diff(proj2d, lamv, g_subln[0].reshape(1, DIFF_HEAD_DIM), batch, seq)
    out = _out(ya, yb, proj2d, x2d, mod, g_final.reshape(1, d), w_o_dil[0].astype(BF16),
               w_o_diff[0].astype(BF16), w_out[0].astype(BF16), seq)
    return out.reshape(batch, seq, d)
```

```python
import functools
import math

import numpy as np
import jax
import jax.numpy as jnp
from jax import lax
from jax.experimental import pallas as pl
from jax.experimental.pallas import tpu as pltpu

F32 = jnp.float32
BF16 = jnp.bfloat16

D_MODEL = 2048
HEAD_DIM = 128
LANES = 128
DIL_GROUPS = ((128, 1), (512, 4), (2048, 16))
DIL_HEADS_PER_GROUP = 8
DIL_N_HEADS = DIL_HEADS_PER_GROUP * len(DIL_GROUPS)
DIL_QKV = DIL_N_HEADS * HEAD_DIM
DIL_WIDTH = DIL_HEADS_PER_GROUP * HEAD_DIM
DIL_SPAN = 128
DIFF_N_HEADS = 6
DIFF_HEAD_DIM = 2 * HEAD_DIM
DIFF_WIDTH = DIFF_N_HEADS * DIFF_HEAD_DIM
EPS = 1e-6
NEG_INF = -1e30
IN_SIZES = (DIL_QKV, DIL_QKV, DIL_QKV, DIL_WIDTH,
            DIFF_WIDTH, DIFF_WIDTH, DIFF_WIDTH, DIFF_WIDTH, D_MODEL, D_MODEL)
N_IN = sum(IN_SIZES)
(OFF_QA, OFF_KA, OFF_VA, OFF_ZA, OFF_QB, OFF_KB, OFF_VB, OFF_ZB, OFF_GLA, OFF_GLB) = (
    int(v) for v in np.concatenate([[0], np.cumsum(IN_SIZES)[:-1]]))
LOG2E = 1.4426950408889634
Q_PRESCALE = HEAD_DIM ** -0.5 * LOG2E
LAMBDA_INIT = 0.8 - 0.6 * math.exp(-0.3 * 0)

VMEM_LIMIT_BYTES = 56 * 1024 * 1024

PROJ_TM = 1024
PROJ_TN = 2048
PROJ_NORM_CHUNK = 32
DIL_TT = DIL_SPAN * DIL_GROUPS[-1][1]
DIL_UNITS = DIL_TT // DIL_SPAN
DIL_STAGE_ROWS = 256
DIL_MERGE_CHUNK = 64
DIFF_T = 1024
DIFF_DIAG_W = 256
DIFF_ONES_ROWS = 16
OUT_TM = 256


def _alibi_slopes(n):
    return [float(2.0 ** (-8.0 * (i + 1) / n)) for i in range(n)]


def _dot_nt(a, b):
    return lax.dot_general(a, b, (((1,), (1,)), ((), ())), preferred_element_type=F32)


def _silu(z):
    return z / (1.0 + jnp.exp(-z))


def _sigmoid(z):
    return 1.0 / (1.0 + jnp.exp(-z))


def _select_by_index(idx, values):
    out = jnp.float32(values[0])
    for i in range(1, len(values)):
        out = jnp.where(idx == i, jnp.float32(values[i]), out)
    return out


def _ada_kernel(c_ref, w_ref, b_ref, o_ref):
    c = c_ref[...]
    o_ref[...] = jnp.dot(_silu(c), w_ref[...], preferred_element_type=F32,
                         precision=lax.Precision.HIGHEST) + b_ref[...]


def _ada(c_pad, w_ada, b_ada):
    rows, d = c_pad.shape
    n = w_ada.shape[1]
    tn = 512
    return pl.pallas_call(
        _ada_kernel,
        grid=(n // tn,),
        in_specs=[pl.BlockSpec((rows, d), lambda j: (0, 0)),
                  pl.BlockSpec((d, tn), lambda j: (0, j)),
                  pl.BlockSpec((1, tn), lambda j: (0, j))],
        out_specs=pl.BlockSpec((rows, tn), lambda j: (0, j)),
        out_shape=jax.ShapeDtypeStruct((rows, n), F32),
        name="ada",
    )(c_pad, w_ada, b_ada)


def _proj_kernel(x_ref, mod_ref, g_ref, w_ref, cs_ref, o_ref, hn_ref):
    @pl.when(pl.program_id(1) == 0)
    def _():
        shift = mod_ref[0, 0:1, :]
        mult = g_ref[...] * (1.0 + mod_ref[0, 1:2, :])

        def body(i, carry):
            r = pl.multiple_of(i * PROJ_NORM_CHUNK, PROJ_NORM_CHUNK)
            xx = x_ref[pl.ds(r, PROJ_NORM_CHUNK), :]
            ms = jnp.mean(xx * xx, axis=-1, keepdims=True)
            hn_ref[pl.ds(r, PROJ_NORM_CHUNK), :] = (
                xx * lax.rsqrt(ms + EPS) * mult + shift).astype(BF16)
            return carry

        lax.fori_loop(0, x_ref.shape[0] // PROJ_NORM_CHUNK, body, 0)

    acc = jnp.dot(hn_ref[...], w_ref[...], preferred_element_type=F32)
    o_ref[...] = (acc * cs_ref[...]).astype(BF16)


def _proj(x2d, mod, g_norm, w_in_bf16, col_scale, seq):
    t, d = x2d.shape
    n = w_in_bf16.shape[1]
    tm, tn = PROJ_TM, PROJ_TN
    tiles_per_batch = seq // tm
    return pl.pallas_call(
        _proj_kernel,
        grid=(t // tm, n // tn),
        in_specs=[pl.BlockSpec((tm, d), lambda i, j: (i, 0)),
                  pl.BlockSpec((1, 3, d), lambda i, j: (i // tiles_per_batch, 0, 0)),
                  pl.BlockSpec((1, d), lambda i, j: (0, 0)),
                  pl.BlockSpec((d, tn), lambda i, j: (0, j)),
                  pl.BlockSpec((1, tn), lambda i, j: (0, j))],
        out_specs=pl.BlockSpec((tm, tn), lambda i, j: (i, j)),
        out_shape=jax.ShapeDtypeStruct((t, n), BF16),
        scratch_shapes=[pltpu.VMEM((tm, d), BF16)],
        compiler_params=pltpu.CompilerParams(
            dimension_semantics=("parallel", "arbitrary"),
            vmem_limit_bytes=VMEM_LIMIT_BYTES),
        name="proj",
    )(x2d, mod, g_norm, w_in_bf16, col_scale)


def _gather_residues(src_ref, src32, dst_ref, r, n_u, dst_pitch, dst_off):
    n32 = n_u * r // 2
    for c0 in range(0, n32, DIL_STAGE_ROWS):
        src32[c0:c0 + DIL_STAGE_ROWS, :] = pltpu.bitcast(
            src_ref[2 * c0:2 * (c0 + DIL_STAGE_ROWS), :], jnp.uint32)
    for pair in range(r // 2):
        for c0 in range(0, n_u, DIL_SPAN):
            w = src32[pl.ds(pair + (r // 2) * c0, DIL_SPAN, stride=r // 2), :]
            for p, half in ((2 * pair, w << 16), (2 * pair + 1, w & jnp.uint32(0xFFFF0000))):
                row = p * dst_pitch + dst_off + c0
                dst_ref[row:row + DIL_SPAN, :] = lax.bitcast_convert_type(half, F32).astype(BF16)


def _dil_kernel(*refs):
    ins, y_ref = refs[:10], refs[10]
    qd, w32, bm_ref = refs[11:14]
    kd, vd = refs[14:17], refs[17:20]
    o_scr, lse_scr = refs[20:23], refs[23:26]
    z_ref = ins[9]
    h = pl.program_id(1)
    first_tile = pl.program_id(2) == 0

    iq = lax.broadcasted_iota(jnp.int32, (DIL_SPAN, 2 * DIL_SPAN), 0)
    jk = lax.broadcasted_iota(jnp.int32, (DIL_SPAN, 2 * DIL_SPAN), 1)
    dist = (iq - jk + DIL_SPAN).astype(F32)
    valid = jnp.logical_and(jk >= iq, jk <= iq + DIL_SPAN)
    valid_first = jnp.logical_and(valid, jk >= DIL_SPAN)

    slopes = _alibi_slopes(DIL_N_HEADS)
    for g, (_, r) in enumerate(DIL_GROUPS):
        q_ref, kc_ref, vc_ref = ins[3 * g:3 * g + 3]
        kd_g, vd_g = kd[g], vd[g]
        nb = DIL_UNITS // r
        kpitch = (nb + 1) * DIL_SPAN
        coef = _select_by_index(
            h, [slopes[g * DIL_HEADS_PER_GROUP + i] * r * LOG2E for i in range(DIL_HEADS_PER_GROUP)])
        bias = -coef * dist
        bm_ref[0] = jnp.where(valid, bias, NEG_INF)
        bm_ref[1] = jnp.where(valid_first, bias, NEG_INF)

        @pl.when(first_tile)
        def _(r=r, kpitch=kpitch, kd_g=kd_g, vd_g=vd_g):
            for p in range(r):
                kd_g[p * kpitch:p * kpitch + DIL_SPAN, :] = jnp.zeros((DIL_SPAN, LANES), BF16)
                vd_g[p * kpitch:p * kpitch + DIL_SPAN, :] = jnp.zeros((DIL_SPAN, LANES), BF16)

        if r == 1:
            q_src = q_ref
            kd_g[DIL_SPAN:DIL_SPAN + DIL_TT, :] = kc_ref[...]
            vd_g[DIL_SPAN:DIL_SPAN + DIL_TT, :] = vc_ref[...]
        else:
            q_src = qd
            _gather_residues(q_ref, w32, qd, r, nb * DIL_SPAN, nb * DIL_SPAN, 0)
            _gather_residues(kc_ref, w32, kd_g, r, nb * DIL_SPAN, kpitch, DIL_SPAN)
            _gather_residues(vc_ref, w32, vd_g, r, nb * DIL_SPAN, kpitch, DIL_SPAN)

        for p in range(r):
            for ub in range(nb):
                rq = (p * nb + ub) * DIL_SPAN
                rk = p * kpitch + ub * DIL_SPAN
                bm = bm_ref[jnp.where(first_tile, 1, 0)] if ub == 0 else bm_ref[0]
                s = _dot_nt(q_src[rq:rq + DIL_SPAN, :], kd_g[rk:rk + 2 * DIL_SPAN, :]) + bm
                m = jnp.max(s, axis=-1, keepdims=True)
                e = jnp.exp2(s - m)
                l = jnp.sum(e, axis=-1, keepdims=True)
                o = jnp.dot(e.astype(BF16), vd_g[rk:rk + 2 * DIL_SPAN, :],
                            preferred_element_type=F32) * (1.0 / l)
                lse2 = jnp.broadcast_to(m + jnp.log(l) * LOG2E, (DIL_SPAN, LANES))
                row0 = ub * (DIL_SPAN * r) + p
                rows = pl.ds(row0, DIL_SPAN, stride=r) if r > 1 else pl.ds(row0, DIL_SPAN)
                o_scr[g][rows, :] = o
                lse_scr[g][rows, :] = lse2

        for p in range(r):
            last = p * kpitch + nb * DIL_SPAN
            kd_g[p * kpitch:p * kpitch + DIL_SPAN, :] = kd_g[last:last + DIL_SPAN, :]
            vd_g[p * kpitch:p * kpitch + DIL_SPAN, :] = vd_g[last:last + DIL_SPAN, :]

    def merge(i, carry):
        rows = pl.ds(pl.multiple_of(i * DIL_MERGE_CHUNK, DIL_MERGE_CHUNK), DIL_MERGE_CHUNK)
        l0, l1, l2 = lse_scr[0][rows, :], lse_scr[1][rows, :], lse_scr[2][rows, :]
        mx = jnp.maximum(jnp.maximum(l0, l1), l2)
        e0, e1, e2 = jnp.exp2(l0 - mx), jnp.exp2(l1 - mx), jnp.exp2(l2 - mx)
        om = (e0 * o_scr[0][rows, :] + e1 * o_scr[1][rows, :] + e2 * o_scr[2][rows, :]) / (e0 + e1 + e2)
        y_ref[rows, :] = (om * _silu(z_ref[rows, :].astype(F32))).astype(BF16)
        return carry

    lax.fori_loop(0, DIL_TT // DIL_MERGE_CHUNK, merge, 0, unroll=2)


def _dil(proj2d, batch, seq):
    tt = DIL_TT
    tiles = seq // tt

    def cur(col):
        return pl.BlockSpec((tt, LANES), lambda b, h, t: (b * tiles + t, col + h))

    in_specs = []
    for g in range(len(DIL_GROUPS)):
        in_specs += [cur((off + g * DIL_WIDTH) // LANES) for off in (OFF_QA, OFF_KA, OFF_VA)]
    in_specs.append(cur(OFF_ZA // LANES))
    kv_scratch = [pltpu.VMEM((tt + DIL_SPAN * r, LANES), BF16) for _, r in DIL_GROUPS]
    return pl.pallas_call(
        _dil_kernel,
        grid=(batch, DIL_HEADS_PER_GROUP, tiles),
        in_specs=in_specs,
        out_specs=pl.BlockSpec((tt, LANES), lambda b, h, t: (b * tiles + t, h)),
        out_shape=jax.ShapeDtypeStruct((batch * seq, DIL_WIDTH), BF16),
        scratch_shapes=(
            [pltpu.VMEM((tt, LANES), BF16), pltpu.VMEM((tt // 2, LANES), jnp.uint32),
             pltpu.VMEM((2, DIL_SPAN, 2 * DIL_SPAN), F32)]
            + kv_scratch + kv_scratch + [pltpu.VMEM((tt, LANES), F32)] * 6),
        compiler_params=pltpu.CompilerParams(
            dimension_semantics=("parallel", "parallel", "arbitrary"),
            vmem_limit_bytes=VMEM_LIMIT_BYTES),
        name="dil",
    )(*([proj2d] * 10))


def _diff_kernel(q_ref, k_ref, v_ref, lamv_ref, gsub_ref, o_ref,
                 vt_ref, st_ref, pt_ref, cb_ref, m_ref, a_ref, acc_ref, *, slopes2):
    h = pl.program_id(1)
    qi = pl.program_id(2)
    t = q_ref.shape[0]
    nk = vt_ref.shape[0]
    slope2 = _select_by_index(h, slopes2)

    @pl.when(qi == 0)
    def _():
        for j in range(nk):
            vt_ref[j, 0:DIFF_HEAD_DIM, :] = v_ref[j * t:(j + 1) * t, :].astype(F32).T.astype(BF16)
            vt_ref[j, DIFF_HEAD_DIM:, :] = jnp.ones((DIFF_ONES_ROWS, t), BF16)

    cb_ref[...] = slope2 * lax.broadcasted_iota(jnp.int32, (t, LANES), 0).astype(F32)
    m_ref[...] = jnp.full(m_ref.shape, NEG_INF, F32)
    acc_ref[...] = jnp.zeros(acc_ref.shape, F32)

    def block(kj):
        k0 = pl.multiple_of(kj * t, t)
        cblk = slope2 * ((kj - qi) * t).astype(F32)
        for c in range(2):
            sl = slice(c * HEAD_DIM, (c + 1) * HEAD_DIM)
            st_ref[c] = _dot_nt(k_ref[pl.ds(k0, t), sl], q_ref[:, sl])
        for c in range(2):
            for cg in range(t // LANES):
                cols = slice(cg * LANES, (cg + 1) * LANES)
                x = st_ref[c, :, cols] + cb_ref[...]
                m_old = m_ref[c, :, cols]
                m_new = jnp.maximum(m_old, jnp.max(x, axis=0, keepdims=True) + cblk)
                m_ref[c, :, cols] = m_new
                a_ref[c, :, cols] = jnp.exp2(m_old - m_new)
                pt_ref[c, :, cols] = jnp.exp2((x - (m_new - cblk)).astype(BF16))
        vt = vt_ref[kj]
        for c in range(2):
            acc_ref[c] = acc_ref[c] * a_ref[c] + jnp.dot(vt, pt_ref[c], preferred_element_type=F32)

    def diagonal_block():
        k0 = pl.multiple_of(qi * t, t)
        w = DIFF_DIAG_W
        for c in range(2):
            sl = slice(c * HEAD_DIM, (c + 1) * HEAD_DIM)
            for qh in range(t // w):
                kend = (qh + 1) * w
                qs = slice(qh * w, kend)
                st_ref[c, 0:kend, qs] = _dot_nt(k_ref[pl.ds(k0, kend), sl], q_ref[qs, sl])
        for c in range(2):
            for cg in range(t // LANES):
                cols = slice(cg * LANES, (cg + 1) * LANES)
                kend = (cg * LANES // w + 1) * w
                key = lax.broadcasted_iota(jnp.int32, (kend, LANES), 0)
                qry = lax.broadcasted_iota(jnp.int32, (kend, LANES), 1) + cg * LANES
                x = jnp.where(key <= qry, st_ref[c, 0:kend, cols] + cb_ref[0:kend, :], NEG_INF)
                m_old = m_ref[c, :, cols]
                m_new = jnp.maximum(m_old, jnp.max(x, axis=0, keepdims=True))
                m_ref[c, :, cols] = m_new
                a_ref[c, :, cols] = jnp.exp2(m_old - m_new)
                pt_ref[c, 0:kend, cols] = jnp.exp2((x - m_new).astype(BF16))
        vt = vt_ref[qi]
        for c in range(2):
            for qh in range(t // w):
                kend = (qh + 1) * w
                qs = slice(qh * w, kend)
                acc_ref[c, :, qs] = (acc_ref[c, :, qs] * a_ref[c, :, qs]
                                     + jnp.dot(vt[:, 0:kend], pt_ref[c, 0:kend, qs],
                                               preferred_element_type=F32))

    def body(kj, carry):
        block(kj)
        return carry

    lax.fori_loop(0, qi, body, 0)
    diagonal_block()

    lamv = lamv_ref[...]
    lam = (jnp.exp(jnp.sum(lamv[0:1] * lamv[1:2], axis=-1, keepdims=True))
           - jnp.exp(jnp.sum(lamv[2:3] * lamv[3:4], axis=-1, keepdims=True)) + LAMBDA_INIT)
    d = DIFF_HEAD_DIM
    ot = (acc_ref[0, 0:d, :] * (1.0 / acc_ref[0, d:d + 1, :])
          - lam * (acc_ref[1, 0:d, :] * (1.0 / acc_ref[1, d:d + 1, :])))
    ms = jnp.mean(ot * ot, axis=0, keepdims=True)
    ot = ot * lax.rsqrt(ms + EPS)
    o_ref[...] = (ot.T * gsub_ref[...] * (1.0 - LAMBDA_INIT)).astype(o_ref.dtype)


def _diff(proj2d, lamv, g_subln, batch, seq):
    t = DIFF_T
    nq = seq // t
    wq, wk, wv = OFF_QB // DIFF_HEAD_DIM, OFF_KB // DIFF_HEAD_DIM, OFF_VB // DIFF_HEAD_DIM
    slopes2 = tuple(s * LOG2E for s in _alibi_slopes(DIFF_N_HEADS))
    vrows = DIFF_HEAD_DIM + DIFF_ONES_ROWS
    return pl.pallas_call(
        functools.partial(_diff_kernel, slopes2=slopes2),
        grid=(batch, DIFF_N_HEADS, nq),
        in_specs=[
            pl.BlockSpec((t, DIFF_HEAD_DIM), lambda b, h, q: (b * nq + q, wq + h)),
            pl.BlockSpec((seq, DIFF_HEAD_DIM), lambda b, h, q: (b, wk + h)),
            pl.BlockSpec((seq, DIFF_HEAD_DIM), lambda b, h, q: (b, wv + h)),
            pl.BlockSpec((4, HEAD_DIM), lambda b, h, q: (0, 0)),
            pl.BlockSpec((1, DIFF_HEAD_DIM), lambda b, h, q: (0, 0)),
        ],
        out_specs=pl.BlockSpec((t, DIFF_HEAD_DIM), lambda b, h, q: (b * nq + q, h)),
        out_shape=jax.ShapeDtypeStruct((batch * seq, DIFF_WIDTH), BF16),
        scratch_shapes=[
            pltpu.VMEM((nq, vrows, t), BF16),
            pltpu.VMEM((2, t, t), F32),
            pltpu.VMEM((2, t, t), BF16),
            pltpu.VMEM((t, LANES), F32),
            pltpu.VMEM((2, 1, t), F32),
            pltpu.VMEM((2, 1, t), F32),
            pltpu.VMEM((2, vrows, t), F32),
        ],
        compiler_params=pltpu.CompilerParams(
            dimension_semantics=("parallel", "parallel", "arbitrary"),
            vmem_limit_bytes=VMEM_LIMIT_BYTES),
        name="diff",
    )(proj2d, proj2d, proj2d, lamv, g_subln)


def _out_kernel(ya_ref, yb_ref, zb0_ref, zb1_ref, zb2_ref, gla_ref, glb_ref, x_ref, mod_ref,
                gf_ref, wd_ref, wf_ref, wo_ref, out_ref, ybz_ref):
    zb_refs = (zb0_ref, zb1_ref, zb2_ref)
    wz = zb0_ref.shape[1]
    for i in range(3):
        sl = slice(i * wz, (i + 1) * wz)
        ybz_ref[:, sl] = (yb_ref[:, sl].astype(F32) * _silu(zb_refs[i][...].astype(F32))).astype(BF16)
    pa = jnp.dot(ya_ref[...], wd_ref[...], preferred_element_type=F32)
    pb = jnp.dot(ybz_ref[...], wf_ref[...], preferred_element_type=F32)
    merged = (_sigmoid(gla_ref[...].astype(F32)) * pa + _sigmoid(glb_ref[...].astype(F32)) * pb)
    delta = jnp.dot(merged.astype(BF16), wo_ref[...], preferred_element_type=F32)
    y = x_ref[...] + mod_ref[0, 2:3, :] * delta
    ms = jnp.mean(y * y, axis=-1, keepdims=True)
    out_ref[...] = y * lax.rsqrt(ms + EPS) * gf_ref[...]


def _const_spec(shape):
    return pl.BlockSpec(shape, lambda i: (0,) * len(shape), pipeline_mode=pl.Buffered(1))


def _out(ya, yb, proj2d, x2d, mod, g_final, wd, wf, wo, seq):
    t, d = x2d.shape
    tm = OUT_TM
    tiles_per_batch = seq // tm
    wz = 512
    zb_blk = OFF_ZB // wz
    row = lambda w: pl.BlockSpec((tm, w), lambda i: (i, 0))
    in_specs = (
        [row(DIL_WIDTH), row(DIFF_WIDTH)]
        + [pl.BlockSpec((tm, wz), functools.partial(lambda i, k: (i, zb_blk + k), k=k))
           for k in range(3)]
        + [pl.BlockSpec((tm, d), lambda i: (i, OFF_GLA // D_MODEL)),
           pl.BlockSpec((tm, d), lambda i: (i, OFF_GLB // D_MODEL)),
           row(d),
           pl.BlockSpec((1, 3, d), lambda i: (i // tiles_per_batch, 0, 0)),
           _const_spec((1, d)),
           _const_spec(wd.shape), _const_spec(wf.shape), _const_spec(wo.shape)])
    return pl.pallas_call(
        _out_kernel,
        grid=(t // tm,),
        in_specs=in_specs,
        out_specs=row(d),
        out_shape=jax.ShapeDtypeStruct((t, d), F32),
        scratch_shapes=[pltpu.VMEM((tm, DIFF_WIDTH), BF16)],
        compiler_params=pltpu.CompilerParams(
            dimension_semantics=("parallel",),
            vmem_limit_bytes=VMEM_LIMIT_BYTES),
        name="out",
    )(ya, yb, proj2d, proj2d, proj2d, proj2d, proj2d, x2d, mod, g_final, wd, wf, wo)


def _q_column_scale():
    cs = np.ones((1, N_IN), np.float32)
    cs[:, OFF_QA:OFF_QA + DIL_QKV] = Q_PRESCALE
    cs[:, OFF_QB:OFF_QB + DIFF_WIDTH] = Q_PRESCALE
    return jnp.asarray(cs)


def kernel(x, c, w_ada, b_ada, g_norm, w_in, w_o_dil, w_o_diff, w_out,
           lambda_q1, lambda_k1, lambda_q2, lambda_k2, g_subln, g_final):
    batch, seq, d = x.shape
    assert w_ada.shape[0] == 1 and d == D_MODEL and w_in.shape[2] == N_IN
    assert seq % DIL_TT == 0 and seq % PROJ_TM == 0 and seq % DIFF_T == 0
    rows = -(-batch // 8) * 8
    c_pad = jnp.pad(c, ((0, rows - batch), (0, 0)))
    x2d = x.reshape(batch * seq, d)
    ada = _ada(c_pad, w_ada[0], b_ada[0].reshape(1, 3 * d))[:batch]
    mod = ada.reshape(batch, 3, d)
    proj2d = _proj(x2d, mod, g_norm[0].reshape(1, d), w_in[0].astype(BF16), _q_column_scale(), seq)
    ya = _dil(proj2d, batch, seq)
    lamv = jnp.stack([lambda_q1[0], lambda_k1[0], lambda_q2[0], lambda_k2[0]])
    yb = _diff(proj2d, lamv, g_subln[0].reshape(1, DIFF_HEAD_DIM), batch, seq)
    out = _out(ya, yb, proj2d, x2d, mod, g_final.reshape(1, d), w_o_dil[0].astype(BF16),
               w_o_diff[0].astype(BF16), w_out[0].astype(BF16), seq)
    return out.reshape(batch, seq, d)
```

```python
import functools
import math

import numpy as np
import jax
import jax.numpy as jnp
from jax import lax
from jax.experimental import pallas as pl
from jax.experimental.pallas import tpu as pltpu

F32 = jnp.float32
BF16 = jnp.bfloat16

D_MODEL = 2048
HEAD_DIM = 128
LANES = 128
DIL_GROUPS = ((128, 1), (512, 4), (2048, 16))
DIL_HEADS_PER_GROUP = 8
DIL_N_HEADS = DIL_HEADS_PER_GROUP * len(DIL_GROUPS)
DIL_QKV = DIL_N_HEADS * HEAD_DIM
DIL_WIDTH = DIL_HEADS_PER_GROUP * HEAD_DIM
DIL_SPAN = 128
DIFF_N_HEADS = 6
DIFF_HEAD_DIM = 2 * HEAD_DIM
DIFF_WIDTH = DIFF_N_HEADS * DIFF_HEAD_DIM
EPS = 1e-6
NEG_INF = -1e30
IN_SIZES = (DIL_QKV, DIL_QKV, DIL_QKV, DIL_WIDTH,
            DIFF_WIDTH, DIFF_WIDTH, DIFF_WIDTH, DIFF_WIDTH, D_MODEL, D_MODEL)
N_IN = sum(IN_SIZES)
(OFF_QA, OFF_KA, OFF_VA, OFF_ZA, OFF_QB, OFF_KB, OFF_VB, OFF_ZB, OFF_GLA, OFF_GLB) = (
    int(v) for v in np.concatenate([[0], np.cumsum(IN_SIZES)[:-1]]))
LOG2E = 1.4426950408889634
Q_PRESCALE = HEAD_DIM ** -0.5 * LOG2E
LAMBDA_INIT = 0.8 - 0.6 * math.exp(-0.3 * 0)

VMEM_LIMIT_BYTES = 56 * 1024 * 1024

PROJ_TM = 1024
PROJ_TN = 2048
PROJ_NORM_CHUNK = 32
PROJ_NORM_ROWS = 128
DIL_TT = DIL_SPAN * DIL_GROUPS[-1][1]
DIL_UNITS = DIL_TT // DIL_SPAN
DIL_STAGE_ROWS = 256
DIL_MERGE_CHUNK = 64
DIFF_T = 1024
DIFF_DIAG_W = 256
DIFF_ONES_ROWS = 16
OUT_TM = 256


def _alibi_slopes(n):
    return [float(2.0 ** (-8.0 * (i + 1) / n)) for i in range(n)]


def _dot_nt(a, b):
    return lax.dot_general(a, b, (((1,), (1,)), ((), ())), preferred_element_type=F32)


def _silu(z):
    return z / (1.0 + jnp.exp(-z))


def _sigmoid(z):
    return 1.0 / (1.0 + jnp.exp(-z))


def _select_by_index(idx, values):
    out = jnp.float32(values[0])
    for i in range(1, len(values)):
        out = jnp.where(idx == i, jnp.float32(values[i]), out)
    return out


def _ada_kernel(c_ref, w_ref, b_ref, o_ref):
    c = c_ref[...]
    o_ref[...] = jnp.dot(_silu(c), w_ref[...], preferred_element_type=F32,
                         precision=lax.Precision.HIGHEST) + b_ref[...]


def _ada(c_pad, w_ada, b_ada):
    rows, d = c_pad.shape
    n = w_ada.shape[1]
    tn = 512
    return pl.pallas_call(
        _ada_kernel,
        grid=(n // tn,),
        in_specs=[pl.BlockSpec((rows, d), lambda j: (0, 0)),
                  pl.BlockSpec((d, tn), lambda j: (0, j)),
                  pl.BlockSpec((1, tn), lambda j: (0, j))],
        out_specs=pl.BlockSpec((rows, tn), lambda j: (0, j)),
        out_shape=jax.ShapeDtypeStruct((rows, n), F32),
        name="ada",
    )(c_pad, w_ada, b_ada)


def _proj_kernel(x_ref, mod_ref, g_ref, w_ref, cs_ref, o_ref, hn_a, hn_b):
    i = pl.program_id(0)
    j = pl.program_id(1)
    tm = x_ref.shape[0]
    shift = mod_ref[0, 0:1, :]
    mult = g_ref[...] * (1.0 + mod_ref[0, 1:2, :])

    def normalize(dst_ref, r0, rows):
        xx = x_ref[pl.ds(r0, rows), :]
        ms = jnp.mean(xx * xx, axis=-1, keepdims=True)
        dst_ref[pl.ds(r0, rows), :] = (xx * lax.rsqrt(ms + EPS) * mult + shift).astype(BF16)

    @pl.when(jnp.logical_and(i == 0, j == 0))
    def _():
        def body(c, carry):
            normalize(hn_a, pl.multiple_of(c * PROJ_NORM_CHUNK, PROJ_NORM_CHUNK), PROJ_NORM_CHUNK)
            return carry

        lax.fori_loop(0, tm // PROJ_NORM_CHUNK, body, 0)

    r0 = pl.multiple_of(jnp.clip((j - 1) * PROJ_NORM_ROWS, 0, tm - PROJ_NORM_ROWS), PROJ_NORM_ROWS)

    def step(cur_ref, nxt_ref):
        normalize(nxt_ref, r0, PROJ_NORM_ROWS)
        acc = jnp.dot(cur_ref[...], w_ref[...], preferred_element_type=F32)
        o_ref[...] = (acc * cs_ref[...]).astype(BF16)

    @pl.when(i % 2 == 0)
    def _():
        step(hn_a, hn_b)

    @pl.when(i % 2 == 1)
    def _():
        step(hn_b, hn_a)


def _proj(x2d, mod, g_norm, w_in_bf16, col_scale, seq):
    t, d = x2d.shape
    n = w_in_bf16.shape[1]
    tm, tn = PROJ_TM, PROJ_TN
    n_i, n_j = t // tm, n // tn
    tiles_per_batch = seq // tm
    assert (n_j - 1) * PROJ_NORM_ROWS >= tm and tm % PROJ_NORM_ROWS == 0

    def next_tile(i, j):
        return jnp.where(jnp.logical_and(i == 0, j == 0), 0, jnp.minimum(i + 1, n_i - 1))

    return pl.pallas_call(
        _proj_kernel,
        grid=(n_i, n_j),
        in_specs=[pl.BlockSpec((tm, d), lambda i, j: (next_tile(i, j), 0)),
                  pl.BlockSpec((1, 3, d), lambda i, j: (next_tile(i, j) // tiles_per_batch, 0, 0)),
                  pl.BlockSpec((1, d), lambda i, j: (0, 0)),
                  pl.BlockSpec((d, tn), lambda i, j: (0, j)),
                  pl.BlockSpec((1, tn), lambda i, j: (0, j))],
        out_specs=pl.BlockSpec((tm, tn), lambda i, j: (i, j)),
        out_shape=jax.ShapeDtypeStruct((t, n), BF16),
        scratch_shapes=[pltpu.VMEM((tm, d), BF16), pltpu.VMEM((tm, d), BF16)],
        compiler_params=pltpu.CompilerParams(
            dimension_semantics=("arbitrary", "arbitrary"),
            vmem_limit_bytes=VMEM_LIMIT_BYTES),
        name="proj",
    )(x2d, mod, g_norm, w_in_bf16, col_scale)


def _gather_residues(src_ref, src32, dst_ref, r, n_u, dst_pitch, dst_off):
    n32 = n_u * r // 2
    for c0 in range(0, n32, DIL_STAGE_ROWS):
        src32[c0:c0 + DIL_STAGE_ROWS, :] = pltpu.bitcast(
            src_ref[2 * c0:2 * (c0 + DIL_STAGE_ROWS), :], jnp.uint32)
    for pair in range(r // 2):
        for c0 in range(0, n_u, DIL_SPAN):
            w = src32[pl.ds(pair + (r // 2) * c0, DIL_SPAN, stride=r // 2), :]
            for p, half in ((2 * pair, w << 16), (2 * pair + 1, w & jnp.uint32(0xFFFF0000))):
                row = p * dst_pitch + dst_off + c0
                dst_ref[row:row + DIL_SPAN, :] = lax.bitcast_convert_type(half, F32).astype(BF16)


def _dil_kernel(*refs):
    ins, y_ref = refs[:10], refs[10]
    qd, w32, bm_ref = refs[11:14]
    kd, vd = refs[14:17], refs[17:20]
    o_scr, lse_scr = refs[20:23], refs[23:26]
    z_ref = ins[9]
    h = pl.program_id(1)
    first_tile = pl.program_id(2) == 0

    iq = lax.broadcasted_iota(jnp.int32, (DIL_SPAN, 2 * DIL_SPAN), 0)
    jk = lax.broadcasted_iota(jnp.int32, (DIL_SPAN, 2 * DIL_SPAN), 1)
    dist = (iq - jk + DIL_SPAN).astype(F32)
    valid = jnp.logical_and(jk >= iq, jk <= iq + DIL_SPAN)
    valid_first = jnp.logical_and(valid, jk >= DIL_SPAN)

    slopes = _alibi_slopes(DIL_N_HEADS)
    for g, (_, r) in enumerate(DIL_GROUPS):
        q_ref, kc_ref, vc_ref = ins[3 * g:3 * g + 3]
        kd_g, vd_g = kd[g], vd[g]
        nb = DIL_UNITS // r
        kpitch = (nb + 1) * DIL_SPAN
        coef = _select_by_index(
            h, [slopes[g * DIL_HEADS_PER_GROUP + i] * r * LOG2E for i in range(DIL_HEADS_PER_GROUP)])
        bias = -coef * dist
        bm_ref[0] = jnp.where(valid, bias, NEG_INF)
        bm_ref[1] = jnp.where(valid_first, bias, NEG_INF)

        @pl.when(first_tile)
        def _(r=r, kpitch=kpitch, kd_g=kd_g, vd_g=vd_g):
            for p in range(r):
                kd_g[p * kpitch:p * kpitch + DIL_SPAN, :] = jnp.zeros((DIL_SPAN, LANES), BF16)
                vd_g[p * kpitch:p * kpitch + DIL_SPAN, :] = jnp.zeros((DIL_SPAN, LANES), BF16)

        if r == 1:
            q_src = q_ref
            kd_g[DIL_SPAN:DIL_SPAN + DIL_TT, :] = kc_ref[...]
            vd_g[DIL_SPAN:DIL_SPAN + DIL_TT, :] = vc_ref[...]
        else:
            q_src = qd
            _gather_residues(q_ref, w32, qd, r, nb * DIL_SPAN, nb * DIL_SPAN, 0)
            _gather_residues(kc_ref, w32, kd_g, r, nb * DIL_SPAN, kpitch, DIL_SPAN)
            _gather_residues(vc_ref, w32, vd_g, r, nb * DIL_SPAN, kpitch, DIL_SPAN)

        for p in range(r):
            for ub in range(nb):
                rq = (p * nb + ub) * DIL_SPAN
                rk = p * kpitch + ub * DIL_SPAN
                bm = bm_ref[jnp.where(first_tile, 1, 0)] if ub == 0 else bm_ref[0]
                s = _dot_nt(q_src[rq:rq + DIL_SPAN, :], kd_g[rk:rk + 2 * DIL_SPAN, :]) + bm
                m = jnp.max(s, axis=-1, keepdims=True)
                e = jnp.exp2(s - m)
                l = jnp.sum(e, axis=-1, keepdims=True)
                o = jnp.dot(e.astype(BF16), vd_g[rk:rk + 2 * DIL_SPAN, :],
                            preferred_element_type=F32) * (1.0 / l)
                lse2 = jnp.broadcast_to(m + jnp.log(l) * LOG2E, (DIL_SPAN, LANES))
                row0 = ub * (DIL_SPAN * r) + p
                rows = pl.ds(row0, DIL_SPAN, stride=r) if r > 1 else pl.ds(row0, DIL_SPAN)
                o_scr[g][rows, :] = o
                lse_scr[g][rows, :] = lse2

        for p in range(r):
            last = p * kpitch + nb * DIL_SPAN
            kd_g[p * kpitch:p * kpitch + DIL_SPAN, :] = kd_g[last:last + DIL_SPAN, :]
            vd_g[p * kpitch:p * kpitch + DIL_SPAN, :] = vd_g[last:last + DIL_SPAN, :]

    def merge(i, carry):
        rows = pl.ds(pl.multiple_of(i * DIL_MERGE_CHUNK, DIL_MERGE_CHUNK), DIL_MERGE_CHUNK)
        l0, l1, l2 = lse_scr[0][rows, :], lse_scr[1][rows, :], lse_scr[2][rows, :]
        mx = jnp.maximum(jnp.maximum(l0, l1), l2)
        e0, e1, e2 = jnp.exp2(l0 - mx), jnp.exp2(l1 - mx), jnp.exp2(l2 - mx)
        om = (e0 * o_scr[0][rows, :] + e1 * o_scr[1][rows, :] + e2 * o_scr[2][rows, :]) / (e0 + e1 + e2)
        y_ref[rows, :] = (om * _silu(z_ref[rows, :].astype(F32))).astype(BF16)
        return carry

    lax.fori_loop(0, DIL_TT // DIL_MERGE_CHUNK, merge, 0, unroll=2)


def _dil(proj2d, batch, seq):
    tt = DIL_TT
    tiles = seq // tt

    def cur(col):
        return pl.BlockSpec((tt, LANES), lambda b, h, t: (b * tiles + t, col + h))

    in_specs = []
    for g in range(len(DIL_GROUPS)):
        in_specs += [cur((off + g * DIL_WIDTH) // LANES) for off in (OFF_QA, OFF_KA, OFF_VA)]
    in_specs.append(cur(OFF_ZA // LANES))
    kv_scratch = [pltpu.VMEM((tt + DIL_SPAN * r, LANES), BF16) for _, r in DIL_GROUPS]
    return pl.pallas_call(
        _dil_kernel,
        grid=(batch, DIL_HEADS_PER_GROUP, tiles),
        in_specs=in_specs,
        out_specs=pl.BlockSpec((tt, LANES), lambda b, h, t: (b * tiles + t, h)),
        out_shape=jax.ShapeDtypeStruct((batch * seq, DIL_WIDTH), BF16),
        scratch_shapes=(
            [pltpu.VMEM((tt, LANES), BF16), pltpu.VMEM((tt // 2, LANES), jnp.uint32),
             pltpu.VMEM((2, DIL_SPAN, 2 * DIL_SPAN), F32)]
            + kv_scratch + kv_scratch + [pltpu.VMEM((tt, LANES), F32)] * 6),
        compiler_params=pltpu.CompilerParams(
            dimension_semantics=("parallel", "parallel", "arbitrary"),
            vmem_limit_bytes=VMEM_LIMIT_BYTES),
        name="dil",
    )(*([proj2d] * 10))


def _diff_kernel(q_ref, k_ref, v_ref, lamv_ref, gsub_ref, o_ref,
                 vt_ref, st_ref, pt_ref, cb_ref, m_ref, a_ref, acc_ref, *, slopes2):
    h = pl.program_id(1)
    qi = pl.program_id(2)
    t = q_ref.shape[0]
    nk = vt_ref.shape[0]
    slope2 = _select_by_index(h, slopes2)

    @pl.when(qi == 0)
    def _():
        for j in range(nk):
            vt_ref[j, 0:DIFF_HEAD_DIM, :] = v_ref[j * t:(j + 1) * t, :].astype(F32).T.astype(BF16)
            vt_ref[j, DIFF_HEAD_DIM:, :] = jnp.ones((DIFF_ONES_ROWS, t), BF16)

    cb_ref[...] = slope2 * lax.broadcasted_iota(jnp.int32, (t, LANES), 0).astype(F32)
    m_ref[...] = jnp.full(m_ref.shape, NEG_INF, F32)
    acc_ref[...] = jnp.zeros(acc_ref.shape, F32)

    def block(kj):
        k0 = pl.multiple_of(kj * t, t)
        cblk = slope2 * ((kj - qi) * t).astype(F32)
        for c in range(2):
            sl = slice(c * HEAD_DIM, (c + 1) * HEAD_DIM)
            st_ref[c] = _dot_nt(k_ref[pl.ds(k0, t), sl], q_ref[:, sl])
        for c in range(2):
            for cg in range(t // LANES):
                cols = slice(cg * LANES, (cg + 1) * LANES)
                x = st_ref[c, :, cols] + cb_ref[...]
                m_old = m_ref[c, :, cols]
                m_new = jnp.maximum(m_old, jnp.max(x, axis=0, keepdims=True) + cblk)
                m_ref[c, :, cols] = m_new
                a_ref[c, :, cols] = jnp.exp2(m_old - m_new)
                pt_ref[c, :, cols] = jnp.exp2((x - (m_new - cblk)).astype(BF16))
        vt = vt_ref[kj]
        for c in range(2):
            acc_ref[c] = acc_ref[c] * a_ref[c] + jnp.dot(vt, pt_ref[c], preferred_element_type=F32)

    def diagonal_block():
        k0 = pl.multiple_of(qi * t, t)
        w = DIFF_DIAG_W
        for c in range(2):
            sl = slice(c * HEAD_DIM, (c + 1) * HEAD_DIM)
            for qh in range(t // w):
                kend = (qh + 1) * w
                qs = slice(qh * w, kend)
                st_ref[c, 0:kend, qs] = _dot_nt(k_ref[pl.ds(k0, kend), sl], q_ref[qs, sl])
        for c in range(2):
            for cg in range(t // LANES):
                cols = slice(cg * LANES, (cg + 1) * LANES)
                kend = (cg * LANES // w + 1) * w
                key = lax.broadcasted_iota(jnp.int32, (kend, LANES), 0)
                qry = lax.broadcasted_iota(jnp.int32, (kend, LANES), 1) + cg * LANES
                x = jnp.where(key <= qry, st_ref[c, 0:kend, cols] + cb_ref[0:kend, :], NEG_INF)
                m_old = m_ref[c, :, cols]
                m_new = jnp.maximum(m_old, jnp.max(x, axis=0, keepdims=True))
                m_ref[c, :, cols] = m_new
                a_ref[c, :, cols] = jnp.exp2(m_old - m_new)
                pt_ref[c, 0:kend, cols] = jnp.exp2((x - m_new).astype(BF16))
        vt = vt_ref[qi]
        for c in range(2):
            for qh in range(t // w):
                kend = (qh + 1) * w
                qs = slice(qh * w, kend)
                acc_ref[c, :, qs] = (acc_ref[c, :, qs] * a_ref[c, :, qs]
                                     + jnp.dot(vt[:, 0:kend], pt_ref[c, 0:kend, qs],
                                               preferred_element_type=F32))

    def body(kj, carry):
        block(kj)
        return carry

    lax.fori_loop(0, qi, body, 0)
    diagonal_block()

    lamv = lamv_ref[...]
    lam = (jnp.exp(jnp.sum(lamv[0:1] * lamv[1:2], axis=-1, keepdims=True))
           - jnp.exp(jnp.sum(lamv[2:3] * lamv[3:4], axis=-1, keepdims=True)) + LAMBDA_INIT)
    d = DIFF_HEAD_DIM
    ot = (acc_ref[0, 0:d, :] * (1.0 / acc_ref[0, d:d + 1, :])
          - lam * (acc_ref[1, 0:d, :] * (1.0 / acc_ref[1, d:d + 1, :])))
    ms = jnp.mean(ot * ot, axis=0, keepdims=True)
    ot = ot * lax.rsqrt(ms + EPS)
    o_ref[...] = (ot.T * gsub_ref[...] * (1.0 - LAMBDA_INIT)).astype(o_ref.dtype)


def _diff(proj2d, lamv, g_subln, batch, seq):
    t = DIFF_T
    nq = seq // t
    wq, wk, wv = OFF_QB // DIFF_HEAD_DIM, OFF_KB // DIFF_HEAD_DIM, OFF_VB // DIFF_HEAD_DIM
    slopes2 = tuple(s * LOG2E for s in _alibi_slopes(DIFF_N_HEADS))
    vrows = DIFF_HEAD_DIM + DIFF_ONES_ROWS
    return pl.pallas_call(
        functools.partial(_diff_kernel, slopes2=slopes2),
        grid=(batch, DIFF_N_HEADS, nq),
        in_specs=[
            pl.BlockSpec((t, DIFF_HEAD_DIM), lambda b, h, q: (b * nq + q, wq + h)),
            pl.BlockSpec((seq, DIFF_HEAD_DIM), lambda b, h, q: (b, wk + h)),
            pl.BlockSpec((seq, DIFF_HEAD_DIM), lambda b, h, q: (b, wv + h)),
            pl.BlockSpec((4, HEAD_DIM), lambda b, h, q: (0, 0)),
            pl.BlockSpec((1, DIFF_HEAD_DIM), lambda b, h, q: (0, 0)),
        ],
        out_specs=pl.BlockSpec((t, DIFF_HEAD_DIM), lambda b, h, q: (b * nq + q, h)),
        out_shape=jax.ShapeDtypeStruct((batch * seq, DIFF_WIDTH), BF16),
        scratch_shapes=[
            pltpu.VMEM((nq, vrows, t), BF16),
            pltpu.VMEM((2, t, t), F32),
            pltpu.VMEM((2, t, t), BF16),
            pltpu.VMEM((t, LANES), F32),
            pltpu.VMEM((2, 1, t), F32),
            pltpu.VMEM((2, 1, t), F32),
            pltpu.VMEM((2, vrows, t), F32),
        ],
        compiler_params=pltpu.CompilerParams(
            dimension_semantics=("parallel", "parallel", "arbitrary"),
            vmem_limit_bytes=VMEM_LIMIT_BYTES),
        name="diff",
    )(proj2d, proj2d, proj2d, lamv, g_subln)


def _out_kernel(ya_ref, yb_ref, zb0_ref, zb1_ref, zb2_ref, gla_ref, glb_ref, x_ref, mod_ref,
                gf_ref, wd_ref, wf_ref, wo_ref, out_ref, ybz_ref):
    zb_refs = (zb0_ref, zb1_ref, zb2_ref)
    wz = zb0_ref.shape[1]
    for i in range(3):
        sl = slice(i * wz, (i + 1) * wz)
        ybz_ref[:, sl] = (yb_ref[:, sl].astype(F32) * _silu(zb_refs[i][...].astype(F32))).astype(BF16)
    pa = jnp.dot(ya_ref[...], wd_ref[...], preferred_element_type=F32)
    pb = jnp.dot(ybz_ref[...], wf_ref[...], preferred_element_type=F32)
    merged = (_sigmoid(gla_ref[...].astype(F32)) * pa + _sigmoid(glb_ref[...].astype(F32)) * pb)
    delta = jnp.dot(merged.astype(BF16), wo_ref[...], preferred_element_type=F32)
    y = x_ref[...] + mod_ref[0, 2:3, :] * delta
    ms = jnp.mean(y * y, axis=-1, keepdims=True)
    out_ref[...] = y * lax.rsqrt(ms + EPS) * gf_ref[...]


def _const_spec(shape):
    return pl.BlockSpec(shape, lambda i: (0,) * len(shape), pipeline_mode=pl.Buffered(1))


def _out(ya, yb, proj2d, x2d, mod, g_final, wd, wf, wo, seq):
    t, d = x2d.shape
    tm = OUT_TM
    tiles_per_batch = seq // tm
    wz = 512
    zb_blk = OFF_ZB // wz
    row = lambda w: pl.BlockSpec((tm, w), lambda i: (i, 0))
    in_specs = (
        [row(DIL_WIDTH), row(DIFF_WIDTH)]
        + [pl.BlockSpec((tm, wz), functools.partial(lambda i, k: (i, zb_blk + k), k=k))
           for k in range(3)]
        + [pl.BlockSpec((tm, d), lambda i: (i, OFF_GLA // D_MODEL)),
           pl.BlockSpec((tm, d), lambda i: (i, OFF_GLB // D_MODEL)),
           row(d),
           pl.BlockSpec((1, 3, d), lambda i: (i // tiles_per_batch, 0, 0)),
           _const_spec((1, d)),
           _const_spec(wd.shape), _const_spec(wf.shape), _const_spec(wo.shape)])
    return pl.pallas_call(
        _out_kernel,
        grid=(t // tm,),
        in_specs=in_specs,
        out_specs=row(d),
        out_shape=jax.ShapeDtypeStruct((t, d), F32),
        scratch_shapes=[pltpu.VMEM((tm, DIFF_WIDTH), BF16)],
        compiler_params=pltpu.CompilerParams(
            dimension_semantics=("parallel",),
            vmem_limit_bytes=VMEM_LIMIT_BYTES),
        name="out",
    )(ya, yb, proj2d, proj2d, proj2d, proj2d, proj2d, x2d, mod, g_final, wd, wf, wo)


def _q_column_scale():
    cs = np.ones((1, N_IN), np.float32)
    cs[:, OFF_QA:OFF_QA + DIL_QKV] = Q_PRESCALE
    cs[:, OFF_QB:OFF_QB + DIFF_WIDTH] = Q_PRESCALE
    return jnp.asarray(cs)


def kernel(x, c, w_ada, b_ada, g_norm, w_in, w_o_dil, w_o_diff, w_out,
           lambda_q1, lambda_k1, lambda_q2, lambda_k2, g_subln, g_final):
    batch, seq, d = x.shape
    assert w_ada.shape[0] == 1 and d == D_MODEL and w_in.shape[2] == N_IN
    assert seq % DIL_TT == 0 and seq % PROJ_TM == 0 and seq % DIFF_T == 0
    rows = -(-batch // 8) * 8
    c_pad = jnp.pad(c, ((0, rows - batch), (0, 0)))
    x2d = x.reshape(batch * seq, d)
    ada = _ada(c_pad, w_ada[0], b_ada[0].reshape(1, 3 * d))[:batch]
    mod = ada.reshape(batch, 3, d)
    proj2d = _proj(x2d, mod, g_norm[0].reshape(1, d), w_in[0].astype(BF16), _q_column_scale(), seq)
    ya = _dil(proj2d, batch, seq)
    lamv = jnp.stack([lambda_q1[0], lambda_k1[0], lambda_q2[0], lambda_k2[0]])
    yb = _diff(proj2d, lamv, g_subln[0].reshape(1, DIFF_HEAD_DIM), batch, seq)
    out = _out(ya, yb, proj2d, x2d, mod, g_final.reshape(1, d), w_o_dil[0].astype(BF16),
               w_o_diff[0].astype(BF16), w_out[0].astype(BF16), seq)
    return out.reshape(batch, seq, d)
```

```python
import functools
import math

import numpy as np
import jax
import jax.numpy as jnp
from jax import lax
from jax.experimental import pallas as pl
from jax.experimental.pallas import tpu as pltpu

F32 = jnp.float32
BF16 = jnp.bfloat16

D_MODEL = 2048
HEAD_DIM = 128
LANES = 128
DIL_GROUPS = ((128, 1), (512, 4), (2048, 16))
DIL_HEADS_PER_GROUP = 8
DIL_N_HEADS = DIL_HEADS_PER_GROUP * len(DIL_GROUPS)
DIL_QKV = DIL_N_HEADS * HEAD_DIM
DIL_WIDTH = DIL_HEADS_PER_GROUP * HEAD_DIM
DIL_SPAN = 128
DIFF_N_HEADS = 6
DIFF_HEAD_DIM = 2 * HEAD_DIM
DIFF_WIDTH = DIFF_N_HEADS * DIFF_HEAD_DIM
EPS = 1e-6
NEG_INF = -1e30
IN_SIZES = (DIL_QKV, DIL_QKV, DIL_QKV, DIL_WIDTH,
            DIFF_WIDTH, DIFF_WIDTH, DIFF_WIDTH, DIFF_WIDTH, D_MODEL, D_MODEL)
N_IN = sum(IN_SIZES)
(OFF_QA, OFF_KA, OFF_VA, OFF_ZA, OFF_QB, OFF_KB, OFF_VB, OFF_ZB, OFF_GLA, OFF_GLB) = (
    int(v) for v in np.concatenate([[0], np.cumsum(IN_SIZES)[:-1]]))
LOG2E = 1.4426950408889634
Q_PRESCALE = HEAD_DIM ** -0.5 * LOG2E
LAMBDA_INIT = 0.8 - 0.6 * math.exp(-0.3 * 0)

VMEM_LIMIT_BYTES = 56 * 1024 * 1024

PROJ_TM = 1024
PROJ_TN = 2048
PROJ_NORM_CHUNK = 32
PROJ_NORM_ROWS = 128
DIL_TT = DIL_SPAN * DIL_GROUPS[-1][1]
DIL_UNITS = DIL_TT // DIL_SPAN
DIL_WIDE_R = DIL_GROUPS[-1][1]
DIL_WIDE_PITCH = 24
DIL_STAGE_ROWS = 256
DIL_MERGE_CHUNK = 64
DIFF_T = 1024
DIFF_DIAG_W = 256
DIFF_ONES_ROWS = 16
OUT_TM = 256


def _alibi_slopes(n):
    return [float(2.0 ** (-8.0 * (i + 1) / n)) for i in range(n)]


def _dot_nt(a, b):
    return lax.dot_general(a, b, (((1,), (1,)), ((), ())), preferred_element_type=F32)


def _silu(z):
    return z / (1.0 + jnp.exp(-z))


def _sigmoid(z):
    return 1.0 / (1.0 + jnp.exp(-z))


def _select_by_index(idx, values):
    out = jnp.float32(values[0])
    for i in range(1, len(values)):
        out = jnp.where(idx == i, jnp.float32(values[i]), out)
    return out


def _ada_kernel(c_ref, w_ref, b_ref, o_ref):
    c = c_ref[...]
    o_ref[...] = jnp.dot(_silu(c), w_ref[...], preferred_element_type=F32,
                         precision=lax.Precision.HIGHEST) + b_ref[...]


def _ada(c_pad, w_ada, b_ada):
    rows, d = c_pad.shape
    n = w_ada.shape[1]
    tn = 512
    return pl.pallas_call(
        _ada_kernel,
        grid=(n // tn,),
        in_specs=[pl.BlockSpec((rows, d), lambda j: (0, 0)),
                  pl.BlockSpec((d, tn), lambda j: (0, j)),
                  pl.BlockSpec((1, tn), lambda j: (0, j))],
        out_specs=pl.BlockSpec((rows, tn), lambda j: (0, j)),
        out_shape=jax.ShapeDtypeStruct((rows, n), F32),
        name="ada",
    )(c_pad, w_ada, b_ada)


def _proj_kernel(x_ref, mod_ref, g_ref, w_ref, cs_ref, o_ref, hn_a, hn_b):
    i = pl.program_id(0)
    j = pl.program_id(1)
    tm = x_ref.shape[0]
    shift = mod_ref[0, 0:1, :]
    mult = g_ref[...] * (1.0 + mod_ref[0, 1:2, :])

    def normalize(dst_ref, r0, rows):
        xx = x_ref[pl.ds(r0, rows), :]
        ms = jnp.mean(xx * xx, axis=-1, keepdims=True)
        dst_ref[pl.ds(r0, rows), :] = (xx * lax.rsqrt(ms + EPS) * mult + shift).astype(BF16)

    @pl.when(jnp.logical_and(i == 0, j == 0))
    def _():
        def body(c, carry):
            normalize(hn_a, pl.multiple_of(c * PROJ_NORM_CHUNK, PROJ_NORM_CHUNK), PROJ_NORM_CHUNK)
            return carry

        lax.fori_loop(0, tm // PROJ_NORM_CHUNK, body, 0)

    r0 = pl.multiple_of(jnp.clip((j - 1) * PROJ_NORM_ROWS, 0, tm - PROJ_NORM_ROWS), PROJ_NORM_ROWS)

    def step(cur_ref, nxt_ref):
        normalize(nxt_ref, r0, PROJ_NORM_ROWS)
        acc = jnp.dot(cur_ref[...], w_ref[...], preferred_element_type=F32)
        o_ref[...] = (acc * cs_ref[...]).astype(BF16)

    @pl.when(i % 2 == 0)
    def _():
        step(hn_a, hn_b)

    @pl.when(i % 2 == 1)
    def _():
        step(hn_b, hn_a)


def _proj(x2d, mod, g_norm, w_in_bf16, col_scale, seq):
    t, d = x2d.shape
    n = w_in_bf16.shape[1]
    tm, tn = PROJ_TM, PROJ_TN
    n_i, n_j = t // tm, n // tn
    tiles_per_batch = seq // tm
    assert (n_j - 1) * PROJ_NORM_ROWS >= tm and tm % PROJ_NORM_ROWS == 0

    def next_tile(i, j):
        return jnp.where(jnp.logical_and(i == 0, j == 0), 0, jnp.minimum(i + 1, n_i - 1))

    return pl.pallas_call(
        _proj_kernel,
        grid=(n_i, n_j),
        in_specs=[pl.BlockSpec((tm, d), lambda i, j: (next_tile(i, j), 0)),
                  pl.BlockSpec((1, 3, d), lambda i, j: (next_tile(i, j) // tiles_per_batch, 0, 0)),
                  pl.BlockSpec((1, d), lambda i, j: (0, 0)),
                  pl.BlockSpec((d, tn), lambda i, j: (0, j)),
                  pl.BlockSpec((1, tn), lambda i, j: (0, j))],
        out_specs=pl.BlockSpec((tm, tn), lambda i, j: (i, j)),
        out_shape=jax.ShapeDtypeStruct((t, n), BF16),
        scratch_shapes=[pltpu.VMEM((tm, d), BF16), pltpu.VMEM((tm, d), BF16)],
        compiler_params=pltpu.CompilerParams(
            dimension_semantics=("arbitrary", "arbitrary"),
            vmem_limit_bytes=VMEM_LIMIT_BYTES),
        name="proj",
    )(x2d, mod, g_norm, w_in_bf16, col_scale)


def _gather_residues(src_ref, src32, dst_ref, r, n_u, dst_pitch, dst_off):
    n32 = n_u * r // 2
    for c0 in range(0, n32, DIL_STAGE_ROWS):
        src32[c0:c0 + DIL_STAGE_ROWS, :] = pltpu.bitcast(
            src_ref[2 * c0:2 * (c0 + DIL_STAGE_ROWS), :], jnp.uint32)
    for pair in range(r // 2):
        for c0 in range(0, n_u, DIL_SPAN):
            w = src32[pl.ds(pair + (r // 2) * c0, DIL_SPAN, stride=r // 2), :]
            for p, half in ((2 * pair, w << 16), (2 * pair + 1, w & jnp.uint32(0xFFFF0000))):
                row = p * dst_pitch + dst_off + c0
                dst_ref[row:row + DIL_SPAN, :] = lax.bitcast_convert_type(half, F32).astype(BF16)


def _dil_kernel(*refs):
    ins, y_ref = refs[:10], refs[10]
    qd, w32, bm_ref = refs[11:14]
    kd, vd = refs[14:17], refs[17:20]
    o_scr, lse_scr = refs[20:23], refs[23:26]
    z_ref = ins[9]
    h = pl.program_id(1)
    first_tile = pl.program_id(2) == 0

    iq = lax.broadcasted_iota(jnp.int32, (DIL_SPAN, 2 * DIL_SPAN), 0)
    jk = lax.broadcasted_iota(jnp.int32, (DIL_SPAN, 2 * DIL_SPAN), 1)
    dist = (iq - jk + DIL_SPAN).astype(F32)
    valid = jnp.logical_and(jk >= iq, jk <= iq + DIL_SPAN)
    valid_first = jnp.logical_and(valid, jk >= DIL_SPAN)

    slopes = _alibi_slopes(DIL_N_HEADS)
    for g, (_, r) in enumerate(DIL_GROUPS):
        q_ref, kc_ref, vc_ref = ins[3 * g:3 * g + 3]
        kd_g, vd_g = kd[g], vd[g]
        nb = DIL_UNITS // r
        kpitch = (nb + 1) * DIL_SPAN
        coef = _select_by_index(
            h, [slopes[g * DIL_HEADS_PER_GROUP + i] * r * LOG2E for i in range(DIL_HEADS_PER_GROUP)])
        bias = -coef * dist
        bm_ref[0] = jnp.where(valid, bias, NEG_INF)
        bm_ref[1] = jnp.where(valid_first, bias, NEG_INF)

        @pl.when(first_tile)
        def _(r=r, kpitch=kpitch, kd_g=kd_g, vd_g=vd_g):
            for p in range(r):
                kd_g[p * kpitch:p * kpitch + DIL_SPAN, :] = jnp.zeros((DIL_SPAN, LANES), BF16)
                vd_g[p * kpitch:p * kpitch + DIL_SPAN, :] = jnp.zeros((DIL_SPAN, LANES), BF16)

        if r == 1:
            q_src = q_ref
            kd_g[DIL_SPAN:DIL_SPAN + DIL_TT, :] = kc_ref[...]
            vd_g[DIL_SPAN:DIL_SPAN + DIL_TT, :] = vc_ref[...]
        else:
            q_src = qd
            _gather_residues(q_ref, w32, qd, r, nb * DIL_SPAN, nb * DIL_SPAN, 0)
            _gather_residues(kc_ref, w32, kd_g, r, nb * DIL_SPAN, kpitch, DIL_SPAN)
            _gather_residues(vc_ref, w32, vd_g, r, nb * DIL_SPAN, kpitch, DIL_SPAN)

        for p in range(r):
            for ub in range(nb):
                rq = (p * nb + ub) * DIL_SPAN
                rk = p * kpitch + ub * DIL_SPAN
                bm = bm_ref[jnp.where(first_tile, 1, 0)] if ub == 0 else bm_ref[0]
                s = _dot_nt(q_src[rq:rq + DIL_SPAN, :], kd_g[rk:rk + 2 * DIL_SPAN, :]) + bm
                m = jnp.max(s, axis=-1, keepdims=True)
                e = jnp.exp2(s - m)
                l = jnp.sum(e, axis=-1, keepdims=True)
                o = jnp.dot(e.astype(BF16), vd_g[rk:rk + 2 * DIL_SPAN, :],
                            preferred_element_type=F32) * (1.0 / l)
                lse2 = jnp.broadcast_to(m + jnp.log(l) * LOG2E, (DIL_SPAN, LANES))
                if r == DIL_WIDE_R:
                    rows = pl.ds(p, DIL_SPAN, stride=DIL_WIDE_PITCH)
                elif r > 1:
                    rows = pl.ds(ub * (DIL_SPAN * r) + p, DIL_SPAN, stride=r)
                else:
                    rows = pl.ds(ub * DIL_SPAN, DIL_SPAN)
                o_scr[g][rows, :] = o
                lse_scr[g][rows, :] = lse2

        for p in range(r):
            last = p * kpitch + nb * DIL_SPAN
            kd_g[p * kpitch:p * kpitch + DIL_SPAN, :] = kd_g[last:last + DIL_SPAN, :]
            vd_g[p * kpitch:p * kpitch + DIL_SPAN, :] = vd_g[last:last + DIL_SPAN, :]

    def merge(i, carry):
        rows = pl.ds(pl.multiple_of(i * DIL_MERGE_CHUNK, DIL_MERGE_CHUNK), DIL_MERGE_CHUNK)

        def wide(ref):
            per = DIL_MERGE_CHUNK // DIL_WIDE_R
            return jnp.concatenate(
                [ref[pl.ds(pl.multiple_of((i * per + k) * DIL_WIDE_PITCH, 8), DIL_WIDE_R), :]
                 for k in range(per)], axis=0)

        l0, l1, l2 = lse_scr[0][rows, :], lse_scr[1][rows, :], wide(lse_scr[2])
        mx = jnp.maximum(jnp.maximum(l0, l1), l2)
        e0, e1, e2 = jnp.exp2(l0 - mx), jnp.exp2(l1 - mx), jnp.exp2(l2 - mx)
        om = (e0 * o_scr[0][rows, :] + e1 * o_scr[1][rows, :] + e2 * wide(o_scr[2])) / (e0 + e1 + e2)
        y_ref[rows, :] = (om * _silu(z_ref[rows, :].astype(F32))).astype(BF16)
        return carry

    lax.fori_loop(0, DIL_TT // DIL_MERGE_CHUNK, merge, 0, unroll=2)


def _dil(proj2d, batch, seq):
    tt = DIL_TT
    tiles = seq // tt

    def cur(col):
        return pl.BlockSpec((tt, LANES), lambda b, h, t: (b * tiles + t, col + h))

    in_specs = []
    for g in range(len(DIL_GROUPS)):
        in_specs += [cur((off + g * DIL_WIDTH) // LANES) for off in (OFF_QA, OFF_KA, OFF_VA)]
    in_specs.append(cur(OFF_ZA // LANES))
    kv_scratch = [pltpu.VMEM((tt + DIL_SPAN * r, LANES), BF16) for _, r in DIL_GROUPS]
    return pl.pallas_call(
        _dil_kernel,
        grid=(batch, DIL_HEADS_PER_GROUP, tiles),
        in_specs=in_specs,
        out_specs=pl.BlockSpec((tt, LANES), lambda b, h, t: (b * tiles + t, h)),
        out_shape=jax.ShapeDtypeStruct((batch * seq, DIL_WIDTH), BF16),
        scratch_shapes=(
            [pltpu.VMEM((tt, LANES), BF16), pltpu.VMEM((tt // 2, LANES), jnp.uint32),
             pltpu.VMEM((2, DIL_SPAN, 2 * DIL_SPAN), F32)]
            + kv_scratch + kv_scratch
            + [pltpu.VMEM((tt, LANES), F32)] * 2 + [pltpu.VMEM((DIL_SPAN * DIL_WIDE_PITCH, LANES), F32)]
            + [pltpu.VMEM((tt, LANES), F32)] * 2 + [pltpu.VMEM((DIL_SPAN * DIL_WIDE_PITCH, LANES), F32)]),
        compiler_params=pltpu.CompilerParams(
            dimension_semantics=("parallel", "parallel", "arbitrary"),
            vmem_limit_bytes=VMEM_LIMIT_BYTES),
        name="dil",
    )(*([proj2d] * 10))


def _diff_kernel(q_ref, k_ref, v_ref, lamv_ref, gsub_ref, o_ref,
                 vt_ref, st_ref, pt_ref, cb_ref, m_ref, a_ref, acc_ref, *, slopes2):
    h = pl.program_id(1)
    qi = pl.program_id(2)
    t = q_ref.shape[0]
    nk = vt_ref.shape[0]
    slope2 = _select_by_index(h, slopes2)

    @pl.when(qi == 0)
    def _():
        for j in range(nk):
            vt_ref[j, 0:DIFF_HEAD_DIM, :] = v_ref[j * t:(j + 1) * t, :].astype(F32).T.astype(BF16)
            vt_ref[j, DIFF_HEAD_DIM:, :] = jnp.ones((DIFF_ONES_ROWS, t), BF16)

    cb_ref[...] = slope2 * lax.broadcasted_iota(jnp.int32, (t, LANES), 0).astype(F32)
    m_ref[...] = jnp.full(m_ref.shape, NEG_INF, F32)
    acc_ref[...] = jnp.zeros(acc_ref.shape, F32)

    def block(kj):
        k0 = pl.multiple_of(kj * t, t)
        cblk = slope2 * ((kj - qi) * t).astype(F32)
        for c in range(2):
            sl = slice(c * HEAD_DIM, (c + 1) * HEAD_DIM)
            st_ref[c] = _dot_nt(k_ref[pl.ds(k0, t), sl], q_ref[:, sl])
        for c in range(2):
            for cg in range(t // LANES):
                cols = slice(cg * LANES, (cg + 1) * LANES)
                x = st_ref[c, :, cols] + cb_ref[...]
                m_old = m_ref[c, :, cols]
                m_new = jnp.maximum(m_old, jnp.max(x, axis=0, keepdims=True) + cblk)
                m_ref[c, :, cols] = m_new
                a_ref[c, :, cols] = jnp.exp2(m_old - m_new)
                pt_ref[c, :, cols] = jnp.exp2((x - (m_new - cblk)).astype(BF16))
        vt = vt_ref[kj]
        for c in range(2):
            acc_ref[c] = acc_ref[c] * a_ref[c] + jnp.dot(vt, pt_ref[c], preferred_element_type=F32)

    def diagonal_block():
        k0 = pl.multiple_of(qi * t, t)
        w = DIFF_DIAG_W
        for c in range(2):
            sl = slice(c * HEAD_DIM, (c + 1) * HEAD_DIM)
            for qh in range(t // w):
                kend = (qh + 1) * w
                qs = slice(qh * w, kend)
                st_ref[c, 0:kend, qs] = _dot_nt(k_ref[pl.ds(k0, kend), sl], q_ref[qs, sl])
        for c in range(2):
            for cg in range(t // LANES):
                cols = slice(cg * LANES, (cg + 1) * LANES)
                kend = (cg * LANES // w + 1) * w
                key = lax.broadcasted_iota(jnp.int32, (kend, LANES), 0)
                qry = lax.broadcasted_iota(jnp.int32, (kend, LANES), 1) + cg * LANES
                x = jnp.where(key <= qry, st_ref[c, 0:kend, cols] + cb_ref[0:kend, :], NEG_INF)
                m_old = m_ref[c, :, cols]
                m_new = jnp.maximum(m_old, jnp.max(x, axis=0, keepdims=True))
                m_ref[c, :, cols] = m_new
                a_ref[c, :, cols] = jnp.exp2(m_old - m_new)
                pt_ref[c, 0:kend, cols] = jnp.exp2((x - m_new).astype(BF16))
        vt = vt_ref[qi]
        for c in range(2):
            for qh in range(t // w):
                kend = (qh + 1) * w
                qs = slice(qh * w, kend)
                acc_ref[c, :, qs] = (acc_ref[c, :, qs] * a_ref[c, :, qs]
                                     + jnp.dot(vt[:, 0:kend], pt_ref[c, 0:kend, qs],
                                               preferred_element_type=F32))

    def body(kj, carry):
        block(kj)
        return carry

    lax.fori_loop(0, qi, body, 0)
    diagonal_block()

    lamv = lamv_ref[...]
    lam = (jnp.exp(jnp.sum(lamv[0:1] * lamv[1:2], axis=-1, keepdims=True))
           - jnp.exp(jnp.sum(lamv[2:3] * lamv[3:4], axis=-1, keepdims=True)) + LAMBDA_INIT)
    d = DIFF_HEAD_DIM
    ot = (acc_ref[0, 0:d, :] * (1.0 / acc_ref[0, d:d + 1, :])
          - lam * (acc_ref[1, 0:d, :] * (1.0 / acc_ref[1, d:d + 1, :])))
    ms = jnp.mean(ot * ot, axis=0, keepdims=True)
    ot = ot * lax.rsqrt(ms + EPS)
    o_ref[...] = (ot.T * gsub_ref[...] * (1.0 - LAMBDA_INIT)).astype(o_ref.dtype)


def _diff(proj2d, lamv, g_subln, batch, seq):
    t = DIFF_T
    nq = seq // t
    wq, wk, wv = OFF_QB // DIFF_HEAD_DIM, OFF_KB // DIFF_HEAD_DIM, OFF_VB // DIFF_HEAD_DIM
    slopes2 = tuple(s * LOG2E for s in _alibi_slopes(DIFF_N_HEADS))
    vrows = DIFF_HEAD_DIM + DIFF_ONES_ROWS
    return pl.pallas_call(
        functools.partial(_diff_kernel, slopes2=slopes2),
        grid=(batch, DIFF_N_HEADS, nq),
        in_specs=[
            pl.BlockSpec((t, DIFF_HEAD_DIM), lambda b, h, q: (b * nq + q, wq + h)),
            pl.BlockSpec((seq, DIFF_HEAD_DIM), lambda b, h, q: (b, wk + h)),
            pl.BlockSpec((seq, DIFF_HEAD_DIM), lambda b, h, q: (b, wv + h)),
            pl.BlockSpec((4, HEAD_DIM), lambda b, h, q: (0, 0)),
            pl.BlockSpec((1, DIFF_HEAD_DIM), lambda b, h, q: (0, 0)),
        ],
        out_specs=pl.BlockSpec((t, DIFF_HEAD_DIM), lambda b, h, q: (b * nq + q, h)),
        out_shape=jax.ShapeDtypeStruct((batch * seq, DIFF_WIDTH), BF16),
        scratch_shapes=[
            pltpu.VMEM((nq, vrows, t), BF16),
            pltpu.VMEM((2, t, t), F32),
            pltpu.VMEM((2, t, t), BF16),
            pltpu.VMEM((t, LANES), F32),
            pltpu.VMEM((2, 1, t), F32),
            pltpu.VMEM((2, 1, t), F32),
            pltpu.VMEM((2, vrows, t), F32),
        ],
        compiler_params=pltpu.CompilerParams(
            dimension_semantics=("parallel", "parallel", "arbitrary"),
            vmem_limit_bytes=VMEM_LIMIT_BYTES),
        name="diff",
    )(proj2d, proj2d, proj2d, lamv, g_subln)


def _out_kernel(ya_ref, yb_ref, zb0_ref, zb1_ref, zb2_ref, gla_ref, glb_ref, x_ref, mod_ref,
                gf_ref, wd_ref, wf_ref, wo_ref, out_ref, ybz_ref):
    zb_refs = (zb0_ref, zb1_ref, zb2_ref)
    wz = zb0_ref.shape[1]
    for i in range(3):
        sl = slice(i * wz, (i + 1) * wz)
        ybz_ref[:, sl] = (yb_ref[:, sl].astype(F32) * _silu(zb_refs[i][...].astype(F32))).astype(BF16)
    pa = jnp.dot(ya_ref[...], wd_ref[...], preferred_element_type=F32)
    pb = jnp.dot(ybz_ref[...], wf_ref[...], preferred_element_type=F32)
    merged = (_sigmoid(gla_ref[...].astype(F32)) * pa + _sigmoid(glb_ref[...].astype(F32)) * pb)
    delta = jnp.dot(merged.astype(BF16), wo_ref[...], preferred_element_type=F32)
    y = x_ref[...] + mod_ref[0, 2:3, :] * delta
    ms = jnp.mean(y * y, axis=-1, keepdims=True)
    out_ref[...] = y * lax.rsqrt(ms + EPS) * gf_ref[...]


def _const_spec(shape):
    return pl.BlockSpec(shape, lambda i: (0,) * len(shape), pipeline_mode=pl.Buffered(1))


def _out(ya, yb, proj2d, x2d, mod, g_final, wd, wf, wo, seq):
    t, d = x2d.shape
    tm = OUT_TM
    tiles_per_batch = seq // tm
    wz = 512
    zb_blk = OFF_ZB // wz
    row = lambda w: pl.BlockSpec((tm, w), lambda i: (i, 0))
    in_specs = (
        [row(DIL_WIDTH), row(DIFF_WIDTH)]
        + [pl.BlockSpec((tm, wz), functools.partial(lambda i, k: (i, zb_blk + k), k=k))
           for k in range(3)]
        + [pl.BlockSpec((tm, d), lambda i: (i, OFF_GLA // D_MODEL)),
           pl.BlockSpec((tm, d), lambda i: (i, OFF_GLB // D_MODEL)),
           row(d),
           pl.BlockSpec((1, 3, d), lambda i: (i // tiles_per_batch, 0, 0)),
           _const_spec((1, d)),
           _const_spec(wd.shape), _const_spec(wf.shape), _const_spec(wo.shape)])
    return pl.pallas_call(
        _out_kernel,
        grid=(t // tm,),
        in_specs=in_specs,
        out_specs=row(d),
        out_shape=jax.ShapeDtypeStruct((t, d), F32),
        scratch_shapes=[pltpu.VMEM((tm, DIFF_WIDTH), BF16)],
        compiler_params=pltpu.CompilerParams(
            dimension_semantics=("parallel",),
            vmem_limit_bytes=VMEM_LIMIT_BYTES),
        name="out",
    )(ya, yb, proj2d, proj2d, proj2d, proj2d, proj2d, x2d, mod, g_final, wd, wf, wo)


def _q_column_scale():
    cs = np.ones((1, N_IN), np.float32)
    cs[:, OFF_QA:OFF_QA + DIL_QKV] = Q_PRESCALE
    cs[:, OFF_QB:OFF_QB + DIFF_WIDTH] = Q_PRESCALE
    return jnp.asarray(cs)


def kernel(x, c, w_ada, b_ada, g_norm, w_in, w_o_dil, w_o_diff, w_out,
           lambda_q1, lambda_k1, lambda_q2, lambda_k2, g_subln, g_final):
    batch, seq, d = x.shape
    assert w_ada.shape[0] == 1 and d == D_MODEL and w_in.shape[2] == N_IN
    assert seq % DIL_TT == 0 and seq % PROJ_TM == 0 and seq % DIFF_T == 0
    rows = -(-batch // 8) * 8
    c_pad = jnp.pad(c, ((0, rows - batch), (0, 0)))
    x2d = x.reshape(batch * seq, d)
    ada = _ada(c_pad, w_ada[0], b_ada[0].reshape(1, 3 * d))[:batch]
    mod = ada.reshape(batch, 3, d)
    proj2d = _proj(x2d, mod, g_norm[0].reshape(1, d), w_in[0].astype(BF16), _q_column_scale(), seq)
    ya = _dil(proj2d, batch, seq)
    lamv = jnp.stack([lambda_q1[0], lambda_k1[0], lambda_q2[0], lambda_k2[0]])
    yb = _diff(proj2d, lamv, g_subln[0].reshape(1, DIFF_HEAD_DIM), batch, seq)
    out = _out(ya, yb, proj2d, x2d, mod, g_final.reshape(1, d), w_o_dil[0].astype(BF16),
               w_o_diff[0].astype(BF16), w_out[0].astype(BF16), seq)
    return out.reshape(batch, seq, d)
```

```python
import functools
import math

import numpy as np
import jax
import jax.numpy as jnp
from jax import lax
from jax.experimental import pallas as pl
from jax.experimental.pallas import tpu as pltpu

F32 = jnp.float32
BF16 = jnp.bfloat16

D_MODEL = 2048
HEAD_DIM = 128
LANES = 128
DIL_GROUPS = ((128, 1), (512, 4), (2048, 16))
DIL_HEADS_PER_GROUP = 8
DIL_N_HEADS = DIL_HEADS_PER_GROUP * len(DIL_GROUPS)
DIL_QKV = DIL_N_HEADS * HEAD_DIM
DIL_WIDTH = DIL_HEADS_PER_GROUP * HEAD_DIM
DIL_SPAN = 128
DIFF_N_HEADS = 6
DIFF_HEAD_DIM = 2 * HEAD_DIM
DIFF_WIDTH = DIFF_N_HEADS * DIFF_HEAD_DIM
EPS = 1e-6
NEG_INF = -1e30
IN_SIZES = (DIL_QKV, DIL_QKV, DIL_QKV, DIL_WIDTH,
            DIFF_WIDTH, DIFF_WIDTH, DIFF_WIDTH, DIFF_WIDTH, D_MODEL, D_MODEL)
N_IN = sum(IN_SIZES)
(OFF_QA, OFF_KA, OFF_VA, OFF_ZA, OFF_QB, OFF_KB, OFF_VB, OFF_ZB, OFF_GLA, OFF_GLB) = (
    int(v) for v in np.concatenate([[0], np.cumsum(IN_SIZES)[:-1]]))
LOG2E = 1.4426950408889634
Q_PRESCALE = HEAD_DIM ** -0.5 * LOG2E
LAMBDA_INIT = 0.8 - 0.6 * math.exp(-0.3 * 0)

VMEM_LIMIT_BYTES = 56 * 1024 * 1024

PROJ_TM = 1024
PROJ_TN = 2048
PROJ_NORM_CHUNK = 32
PROJ_NORM_ROWS = 128
DIL_TT = DIL_SPAN * DIL_GROUPS[-1][1]
DIL_UNITS = DIL_TT // DIL_SPAN
DIL_WIDE_R = DIL_GROUPS[-1][1]
DIL_WIDE_PITCH = 24
DIL_STAGE_ROWS = 256
DIL_MERGE_CHUNK = 64
DIFF_T = 1024
DIFF_DIAG_W = 256
DIFF_ONES_ROWS = 16
OUT_TM = 256


def _alibi_slopes(n):
    return [float(2.0 ** (-8.0 * (i + 1) / n)) for i in range(n)]


def _dot_nt(a, b):
    return lax.dot_general(a, b, (((1,), (1,)), ((), ())), preferred_element_type=F32)


def _silu(z):
    return z / (1.0 + jnp.exp(-z))


def _sigmoid(z):
    return 1.0 / (1.0 + jnp.exp(-z))


def _select_by_index(idx, values):
    out = jnp.float32(values[0])
    for i in range(1, len(values)):
        out = jnp.where(idx == i, jnp.float32(values[i]), out)
    return out


def _ada_kernel(c_ref, w_ref, b_ref, o_ref):
    c = c_ref[...]
    o_ref[...] = jnp.dot(_silu(c), w_ref[...], preferred_element_type=F32,
                         precision=lax.Precision.HIGHEST) + b_ref[...]


def _ada(c_pad, w_ada, b_ada):
    rows, d = c_pad.shape
    n = w_ada.shape[1]
    tn = 512
    return pl.pallas_call(
        _ada_kernel,
        grid=(n // tn,),
        in_specs=[pl.BlockSpec((rows, d), lambda j: (0, 0)),
                  pl.BlockSpec((d, tn), lambda j: (0, j)),
                  pl.BlockSpec((1, tn), lambda j: (0, j))],
        out_specs=pl.BlockSpec((rows, tn), lambda j: (0, j)),
        out_shape=jax.ShapeDtypeStruct((rows, n), F32),
        name="ada",
    )(c_pad, w_ada, b_ada)


def _proj_kernel(x_ref, mod_ref, g_ref, w_ref, cs_ref, o_ref, hn_a, hn_b):
    i = pl.program_id(0)
    j = pl.program_id(1)
    tm = x_ref.shape[0]
    shift = mod_ref[0, 0:1, :]
    mult = g_ref[...] * (1.0 + mod_ref[0, 1:2, :])

    def normalize(dst_ref, r0, rows):
        xx = x_ref[pl.ds(r0, rows), :]
        ms = jnp.mean(xx * xx, axis=-1, keepdims=True)
        dst_ref[pl.ds(r0, rows), :] = (xx * lax.rsqrt(ms + EPS) * mult + shift).astype(BF16)

    @pl.when(jnp.logical_and(i == 0, j == 0))
    def _():
        def body(c, carry):
            normalize(hn_a, pl.multiple_of(c * PROJ_NORM_CHUNK, PROJ_NORM_CHUNK), PROJ_NORM_CHUNK)
            return carry

        lax.fori_loop(0, tm // PROJ_NORM_CHUNK, body, 0)

    r0 = pl.multiple_of(jnp.clip((j - 1) * PROJ_NORM_ROWS, 0, tm - PROJ_NORM_ROWS), PROJ_NORM_ROWS)

    def step(cur_ref, nxt_ref):
        normalize(nxt_ref, r0, PROJ_NORM_ROWS)
        acc = jnp.dot(cur_ref[...], w_ref[...], preferred_element_type=F32)
        o_ref[...] = (acc * cs_ref[...]).astype(BF16)

    @pl.when(i % 2 == 0)
    def _():
        step(hn_a, hn_b)

    @pl.when(i % 2 == 1)
    def _():
        step(hn_b, hn_a)


def _proj(x2d, mod, g_norm, w_in_bf16, col_scale, seq):
    t, d = x2d.shape
    n = w_in_bf16.shape[1]
    tm, tn = PROJ_TM, PROJ_TN
    n_i, n_j = t // tm, n // tn
    tiles_per_batch = seq // tm
    assert (n_j - 1) * PROJ_NORM_ROWS >= tm and tm % PROJ_NORM_ROWS == 0

    def next_tile(i, j):
        return jnp.where(jnp.logical_and(i == 0, j == 0), 0, jnp.minimum(i + 1, n_i - 1))

    return pl.pallas_call(
        _proj_kernel,
        grid=(n_i, n_j),
        in_specs=[pl.BlockSpec((tm, d), lambda i, j: (next_tile(i, j), 0)),
                  pl.BlockSpec((1, 3, d), lambda i, j: (next_tile(i, j) // tiles_per_batch, 0, 0)),
                  pl.BlockSpec((1, d), lambda i, j: (0, 0)),
                  pl.BlockSpec((d, tn), lambda i, j: (0, j)),
                  pl.BlockSpec((1, tn), lambda i, j: (0, j))],
        out_specs=pl.BlockSpec((tm, tn), lambda i, j: (i, j)),
        out_shape=jax.ShapeDtypeStruct((t, n), BF16),
        scratch_shapes=[pltpu.VMEM((tm, d), BF16), pltpu.VMEM((tm, d), BF16)],
        compiler_params=pltpu.CompilerParams(
            dimension_semantics=("arbitrary", "arbitrary"),
            vmem_limit_bytes=VMEM_LIMIT_BYTES),
        name="proj",
    )(x2d, mod, g_norm, w_in_bf16, col_scale)


def _gather_residues(src_ref, src32, dst_ref, r, n_u, dst_pitch, dst_off):
    n32 = n_u * r // 2
    for c0 in range(0, n32, DIL_STAGE_ROWS):
        src32[c0:c0 + DIL_STAGE_ROWS, :] = pltpu.bitcast(
            src_ref[2 * c0:2 * (c0 + DIL_STAGE_ROWS), :], jnp.uint32)
    for pair in range(r // 2):
        for c0 in range(0, n_u, DIL_SPAN):
            w = src32[pl.ds(pair + (r // 2) * c0, DIL_SPAN, stride=r // 2), :]
            for p, half in ((2 * pair, w << 16), (2 * pair + 1, w & jnp.uint32(0xFFFF0000))):
                row = p * dst_pitch + dst_off + c0
                dst_ref[row:row + DIL_SPAN, :] = lax.bitcast_convert_type(half, F32).astype(BF16)


def _dil_kernel(*refs):
    ins, y_ref = refs[:10], refs[10]
    qd, w32, bm_ref = refs[11:14]
    kd, vd = refs[14:17], refs[17:20]
    o_scr, lse_scr = refs[20:23], refs[23:26]
    z_ref = ins[9]
    h = pl.program_id(1)
    first_tile = pl.program_id(2) == 0

    iq = lax.broadcasted_iota(jnp.int32, (DIL_SPAN, 2 * DIL_SPAN), 0)
    jk = lax.broadcasted_iota(jnp.int32, (DIL_SPAN, 2 * DIL_SPAN), 1)
    dist = (iq - jk + DIL_SPAN).astype(F32)
    valid = jnp.logical_and(jk >= iq, jk <= iq + DIL_SPAN)
    valid_first = jnp.logical_and(valid, jk >= DIL_SPAN)

    slopes = _alibi_slopes(DIL_N_HEADS)
    for g, (_, r) in enumerate(DIL_GROUPS):
        q_ref, kc_ref, vc_ref = ins[3 * g:3 * g + 3]
        kd_g, vd_g = kd[g], vd[g]
        nb = DIL_UNITS // r
        kpitch = (nb + 1) * DIL_SPAN
        coef = _select_by_index(
            h, [slopes[g * DIL_HEADS_PER_GROUP + i] * r * LOG2E for i in range(DIL_HEADS_PER_GROUP)])
        bias = -coef * dist
        bm_ref[0] = jnp.where(valid, bias, NEG_INF)
        bm_ref[1] = jnp.where(valid_first, bias, NEG_INF)

        @pl.when(first_tile)
        def _(r=r, kpitch=kpitch, kd_g=kd_g, vd_g=vd_g):
            for p in range(r):
                kd_g[p * kpitch:p * kpitch + DIL_SPAN, :] = jnp.zeros((DIL_SPAN, LANES), BF16)
                vd_g[p * kpitch:p * kpitch + DIL_SPAN, :] = jnp.zeros((DIL_SPAN, LANES), BF16)

        if r == 1:
            q_src = q_ref
            kd_g[DIL_SPAN:DIL_SPAN + DIL_TT, :] = kc_ref[...]
            vd_g[DIL_SPAN:DIL_SPAN + DIL_TT, :] = vc_ref[...]
        else:
            q_src = qd
            _gather_residues(q_ref, w32, qd, r, nb * DIL_SPAN, nb * DIL_SPAN, 0)
            _gather_residues(kc_ref, w32, kd_g, r, nb * DIL_SPAN, kpitch, DIL_SPAN)
            _gather_residues(vc_ref, w32, vd_g, r, nb * DIL_SPAN, kpitch, DIL_SPAN)

        for p in range(r):
            for ub in range(nb):
                rq = (p * nb + ub) * DIL_SPAN
                rk = p * kpitch + ub * DIL_SPAN
                bm = bm_ref[jnp.where(first_tile, 1, 0)] if ub == 0 else bm_ref[0]
                s = _dot_nt(q_src[rq:rq + DIL_SPAN, :], kd_g[rk:rk + 2 * DIL_SPAN, :]) + bm
                m = jnp.max(s, axis=-1, keepdims=True)
                e = jnp.exp2((s - m).astype(BF16))
                l = jnp.sum((e[:, 0:DIL_SPAN] + e[:, DIL_SPAN:]).astype(F32), axis=-1, keepdims=True)
                o = jnp.dot(e, vd_g[rk:rk + 2 * DIL_SPAN, :], preferred_element_type=F32) * (1.0 / l)
                lse2 = jnp.broadcast_to(m + jnp.log(l) * LOG2E, (DIL_SPAN, LANES))
                if r == DIL_WIDE_R:
                    rows = pl.ds(p, DIL_SPAN, stride=DIL_WIDE_PITCH)
                elif r > 1:
                    rows = pl.ds(ub * (DIL_SPAN * r) + p, DIL_SPAN, stride=r)
                else:
                    rows = pl.ds(ub * DIL_SPAN, DIL_SPAN)
                o_scr[g][rows, :] = o
                lse_scr[g][rows, :] = lse2

        for p in range(r):
            last = p * kpitch + nb * DIL_SPAN
            kd_g[p * kpitch:p * kpitch + DIL_SPAN, :] = kd_g[last:last + DIL_SPAN, :]
            vd_g[p * kpitch:p * kpitch + DIL_SPAN, :] = vd_g[last:last + DIL_SPAN, :]

    def merge(i, carry):
        rows = pl.ds(pl.multiple_of(i * DIL_MERGE_CHUNK, DIL_MERGE_CHUNK), DIL_MERGE_CHUNK)

        def wide(ref):
            per = DIL_MERGE_CHUNK // DIL_WIDE_R
            return jnp.concatenate(
                [ref[pl.ds(pl.multiple_of((i * per + k) * DIL_WIDE_PITCH, 8), DIL_WIDE_R), :]
                 for k in range(per)], axis=0)

        l0, l1, l2 = lse_scr[0][rows, :], lse_scr[1][rows, :], wide(lse_scr[2])
        mx = jnp.maximum(jnp.maximum(l0, l1), l2)
        e0, e1, e2 = jnp.exp2(l0 - mx), jnp.exp2(l1 - mx), jnp.exp2(l2 - mx)
        om = (e0 * o_scr[0][rows, :] + e1 * o_scr[1][rows, :] + e2 * wide(o_scr[2])) / (e0 + e1 + e2)
        y_ref[rows, :] = (om * _silu(z_ref[rows, :].astype(F32))).astype(BF16)
        return carry

    lax.fori_loop(0, DIL_TT // DIL_MERGE_CHUNK, merge, 0, unroll=2)


def _dil(proj2d, batch, seq):
    tt = DIL_TT
    tiles = seq // tt

    def cur(col):
        return pl.BlockSpec((tt, LANES), lambda b, h, t: (b * tiles + t, col + h))

    in_specs = []
    for g in range(len(DIL_GROUPS)):
        in_specs += [cur((off + g * DIL_WIDTH) // LANES) for off in (OFF_QA, OFF_KA, OFF_VA)]
    in_specs.append(cur(OFF_ZA // LANES))
    kv_scratch = [pltpu.VMEM((tt + DIL_SPAN * r, LANES), BF16) for _, r in DIL_GROUPS]
    return pl.pallas_call(
        _dil_kernel,
        grid=(batch, DIL_HEADS_PER_GROUP, tiles),
        in_specs=in_specs,
        out_specs=pl.BlockSpec((tt, LANES), lambda b, h, t: (b * tiles + t, h)),
        out_shape=jax.ShapeDtypeStruct((batch * seq, DIL_WIDTH), BF16),
        scratch_shapes=(
            [pltpu.VMEM((tt, LANES), BF16), pltpu.VMEM((tt // 2, LANES), jnp.uint32),
             pltpu.VMEM((2, DIL_SPAN, 2 * DIL_SPAN), F32)]
            + kv_scratch + kv_scratch
            + [pltpu.VMEM((tt, LANES), F32)] * 2 + [pltpu.VMEM((DIL_SPAN * DIL_WIDE_PITCH, LANES), F32)]
            + [pltpu.VMEM((tt, LANES), F32)] * 2 + [pltpu.VMEM((DIL_SPAN * DIL_WIDE_PITCH, LANES), F32)]),
        compiler_params=pltpu.CompilerParams(
            dimension_semantics=("parallel", "parallel", "arbitrary"),
            vmem_limit_bytes=VMEM_LIMIT_BYTES),
        name="dil",
    )(*([proj2d] * 10))


def _diff_kernel(q_ref, k_ref, v_ref, lamv_ref, gsub_ref, o_ref,
                 vt_ref, st_ref, pt_ref, cb_ref, m_ref, a_ref, acc_ref, *, slopes2):
    h = pl.program_id(1)
    qi = pl.program_id(2)
    t = q_ref.shape[0]
    nk = vt_ref.shape[0]
    slope2 = _select_by_index(h, slopes2)

    @pl.when(qi == 0)
    def _():
        for j in range(nk):
            vt_ref[j, 0:DIFF_HEAD_DIM, :] = v_ref[j * t:(j + 1) * t, :].astype(F32).T.astype(BF16)
            vt_ref[j, DIFF_HEAD_DIM:, :] = jnp.ones((DIFF_ONES_ROWS, t), BF16)

    cb_ref[...] = slope2 * lax.broadcasted_iota(jnp.int32, (t, LANES), 0).astype(F32)
    m_ref[...] = jnp.full(m_ref.shape, NEG_INF, F32)
    acc_ref[...] = jnp.zeros(acc_ref.shape, F32)

    def block(kj):
        k0 = pl.multiple_of(kj * t, t)
        cblk = slope2 * ((kj - qi) * t).astype(F32)
        for c in range(2):
            sl = slice(c * HEAD_DIM, (c + 1) * HEAD_DIM)
            st_ref[c] = _dot_nt(k_ref[pl.ds(k0, t), sl], q_ref[:, sl])
        for c in range(2):
            for cg in range(t // LANES):
                cols = slice(cg * LANES, (cg + 1) * LANES)
                x = st_ref[c, :, cols] + cb_ref[...]
                m_old = m_ref[c, :, cols]
                m_new = jnp.maximum(m_old, jnp.max(x, axis=0, keepdims=True) + cblk)
                m_ref[c, :, cols] = m_new
                a_ref[c, :, cols] = jnp.exp2(m_old - m_new)
                pt_ref[c, :, cols] = jnp.exp2((x - (m_new - cblk)).astype(BF16))
        vt = vt_ref[kj]
        for c in range(2):
            acc_ref[c] = acc_ref[c] * a_ref[c] + jnp.dot(vt, pt_ref[c], preferred_element_type=F32)

    def diagonal_block():
        k0 = pl.multiple_of(qi * t, t)
        w = DIFF_DIAG_W
        for c in range(2):
            sl = slice(c * HEAD_DIM, (c + 1) * HEAD_DIM)
            for qh in range(t // w):
                kend = (qh + 1) * w
                qs = slice(qh * w, kend)
                st_ref[c, 0:kend, qs] = _dot_nt(k_ref[pl.ds(k0, kend), sl], q_ref[qs, sl])
        for c in range(2):
            for cg in range(t // LANES):
                cols = slice(cg * LANES, (cg + 1) * LANES)
                kend = (cg * LANES // w + 1) * w
                key = lax.broadcasted_iota(jnp.int32, (kend, LANES), 0)
                qry = lax.broadcasted_iota(jnp.int32, (kend, LANES), 1) + cg * LANES
                x = jnp.where(key <= qry, st_ref[c, 0:kend, cols] + cb_ref[0:kend, :], NEG_INF)
                m_old = m_ref[c, :, cols]
                m_new = jnp.maximum(m_old, jnp.max(x, axis=0, keepdims=True))
                m_ref[c, :, cols] = m_new
                a_ref[c, :, cols] = jnp.exp2(m_old - m_new)
                pt_ref[c, 0:kend, cols] = jnp.exp2((x - m_new).astype(BF16))
        vt = vt_ref[qi]
        for c in range(2):
            for qh in range(t // w):
                kend = (qh + 1) * w
                qs = slice(qh * w, kend)
                acc_ref[c, :, qs] = (acc_ref[c, :, qs] * a_ref[c, :, qs]
                                     + jnp.dot(vt[:, 0:kend], pt_ref[c, 0:kend, qs],
                                               preferred_element_type=F32))

    def body(kj, carry):
        block(kj)
        return carry

    lax.fori_loop(0, qi, body, 0)
    diagonal_block()

    lamv = lamv_ref[...]
    lam = (jnp.exp(jnp.sum(lamv[0:1] * lamv[1:2], axis=-1, keepdims=True))
           - jnp.exp(jnp.sum(lamv[2:3] * lamv[3:4], axis=-1, keepdims=True)) + LAMBDA_INIT)
    d = DIFF_HEAD_DIM
    ot = (acc_ref[0, 0:d, :] * (1.0 / acc_ref[0, d:d + 1, :])
          - lam * (acc_ref[1, 0:d, :] * (1.0 / acc_ref[1, d:d + 1, :])))
    ms = jnp.mean(ot * ot, axis=0, keepdims=True)
    ot = ot * lax.rsqrt(ms + EPS)
    o_ref[...] = (ot.T * gsub_ref[...] * (1.0 - LAMBDA_INIT)).astype(o_ref.dtype)


def _diff(proj2d, lamv, g_subln, batch, seq):
    t = DIFF_T
    nq = seq // t
    wq, wk, wv = OFF_QB // DIFF_HEAD_DIM, OFF_KB // DIFF_HEAD_DIM, OFF_VB // DIFF_HEAD_DIM
    slopes2 = tuple(s * LOG2E for s in _alibi_slopes(DIFF_N_HEADS))
    vrows = DIFF_HEAD_DIM + DIFF_ONES_ROWS
    return pl.pallas_call(
        functools.partial(_diff_kernel, slopes2=slopes2),
        grid=(batch, DIFF_N_HEADS, nq),
        in_specs=[
            pl.BlockSpec((t, DIFF_HEAD_DIM), lambda b, h, q: (b * nq + q, wq + h)),
            pl.BlockSpec((seq, DIFF_HEAD_DIM), lambda b, h, q: (b, wk + h)),
            pl.BlockSpec((seq, DIFF_HEAD_DIM), lambda b, h, q: (b, wv + h)),
            pl.BlockSpec((4, HEAD_DIM), lambda b, h, q: (0, 0)),
            pl.BlockSpec((1, DIFF_HEAD_DIM), lambda b, h, q: (0, 0)),
        ],
        out_specs=pl.BlockSpec((t, DIFF_HEAD_DIM), lambda b, h, q: (b * nq + q, h)),
        out_shape=jax.ShapeDtypeStruct((batch * seq, DIFF_WIDTH), BF16),
        scratch_shapes=[
            pltpu.VMEM((nq, vrows, t), BF16),
            pltpu.VMEM((2, t, t), F32),
            pltpu.VMEM((2, t, t), BF16),
            pltpu.VMEM((t, LANES), F32),
            pltpu.VMEM((2, 1, t), F32),
            pltpu.VMEM((2, 1, t), F32),
            pltpu.VMEM((2, vrows, t), F32),
        ],
        compiler_params=pltpu.CompilerParams(
            dimension_semantics=("parallel", "parallel", "arbitrary"),
            vmem_limit_bytes=VMEM_LIMIT_BYTES),
        name="diff",
    )(proj2d, proj2d, proj2d, lamv, g_subln)


def _out_kernel(ya_ref, yb_ref, zb0_ref, zb1_ref, zb2_ref, gla_ref, glb_ref, x_ref, mod_ref,
                gf_ref, wd_ref, wf_ref, wo_ref, out_ref, ybz_ref):
    zb_refs = (zb0_ref, zb1_ref, zb2_ref)
    wz = zb0_ref.shape[1]
    for i in range(3):
        sl = slice(i * wz, (i + 1) * wz)
        ybz_ref[:, sl] = (yb_ref[:, sl].astype(F32) * _silu(zb_refs[i][...].astype(F32))).astype(BF16)
    pa = jnp.dot(ya_ref[...], wd_ref[...], preferred_element_type=F32)
    pb = jnp.dot(ybz_ref[...], wf_ref[...], preferred_element_type=F32)
    merged = (_sigmoid(gla_ref[...].astype(F32)) * pa + _sigmoid(glb_ref[...].astype(F32)) * pb)
    delta = jnp.dot(merged.astype(BF16), wo_ref[...], preferred_element_type=F32)
    y = x_ref[...] + mod_ref[0, 2:3, :] * delta
    ms = jnp.mean(y * y, axis=-1, keepdims=True)
    out_ref[...] = y * lax.rsqrt(ms + EPS) * gf_ref[...]


def _const_spec(shape):
    return pl.BlockSpec(shape, lambda i: (0,) * len(shape), pipeline_mode=pl.Buffered(1))


def _out(ya, yb, proj2d, x2d, mod, g_final, wd, wf, wo, seq):
    t, d = x2d.shape
    tm = OUT_TM
    tiles_per_batch = seq // tm
    wz = 512
    zb_blk = OFF_ZB // wz
    row = lambda w: pl.BlockSpec((tm, w), lambda i: (i, 0))
    in_specs = (
        [row(DIL_WIDTH), row(DIFF_WIDTH)]
        + [pl.BlockSpec((tm, wz), functools.partial(lambda i, k: (i, zb_blk + k), k=k))
           for k in range(3)]
        + [pl.BlockSpec((tm, d), lambda i: (i, OFF_GLA // D_MODEL)),
           pl.BlockSpec((tm, d), lambda i: (i, OFF_GLB // D_MODEL)),
           row(d),
           pl.BlockSpec((1, 3, d), lambda i: (i // tiles_per_batch, 0, 0)),
           _const_spec((1, d)),
           _const_spec(wd.shape), _const_spec(wf.shape), _const_spec(wo.shape)])
    return pl.pallas_call(
        _out_kernel,
        grid=(t // tm,),
        in_specs=in_specs,
        out_specs=row(d),
        out_shape=jax.ShapeDtypeStruct((t, d), F32),
        scratch_shapes=[pltpu.VMEM((tm, DIFF_WIDTH), BF16)],
        compiler_params=pltpu.CompilerParams(
            dimension_semantics=("parallel",),
            vmem_limit_bytes=VMEM_LIMIT_BYTES),
        name="out",
    )(ya, yb, proj2d, proj2d, proj2d, proj2d, proj2d, x2d, mod, g_final, wd, wf, wo)


def _q_column_scale():
    cs = np.ones((1, N_IN), np.float32)
    cs[:, OFF_QA:OFF_QA + DIL_QKV] = Q_PRESCALE
    cs[:, OFF_QB:OFF_QB + DIFF_WIDTH] = Q_PRESCALE
    return jnp.asarray(cs)


def kernel(x, c, w_ada, b_ada, g_norm, w_in, w_o_dil, w_o_diff, w_out,
           lambda_q1, lambda_k1, lambda_q2, lambda_k2, g_subln, g_final):
    batch, seq, d = x.shape
    assert w_ada.shape[0] == 1 and d == D_MODEL and w_in.shape[2] == N_IN
    assert seq % DIL_TT == 0 and seq % PROJ_TM == 0 and seq % DIFF_T == 0
    rows = -(-batch // 8) * 8
    c_pad = jnp.pad(c, ((0, rows - batch), (0, 0)))
    x2d = x.reshape(batch * seq, d)
    ada = _ada(c_pad, w_ada[0], b_ada[0].reshape(1, 3 * d))[:batch]
    mod = ada.reshape(batch, 3, d)
    proj2d = _proj(x2d, mod, g_norm[0].reshape(1, d), w_in[0].astype(BF16), _q_column_scale(), seq)
    ya = _dil(proj2d, batch, seq)
    lamv = jnp.stack([lambda_q1[0], lambda_k1[0], lambda_q2[0], lambda_k2[0]])
    yb = _diff(proj2d, lamv, g_subln[0].reshape(1, DIFF_HEAD_DIM), batch, seq)
    out = _out(ya, yb, proj2d, x2d, mod, g_final.reshape(1, d), w_o_dil[0].astype(BF16),
               w_o_diff[0].astype(BF16), w_out[0].astype(BF16), seq)
    return out.reshape(batch, seq, d)
```

```python
import functools
import math

import numpy as np
import jax
import jax.numpy as jnp
from jax import lax
from jax.experimental import pallas as pl
from jax.experimental.pallas import tpu as pltpu

F32 = jnp.float32
BF16 = jnp.bfloat16

D_MODEL = 2048
HEAD_DIM = 128
LANES = 128
DIL_GROUPS = ((128, 1), (512, 4), (2048, 16))
DIL_HEADS_PER_GROUP = 8
DIL_N_HEADS = DIL_HEADS_PER_GROUP * len(DIL_GROUPS)
DIL_QKV = DIL_N_HEADS * HEAD_DIM
DIL_WIDTH = DIL_HEADS_PER_GROUP * HEAD_DIM
DIL_SPAN = 128
DIFF_N_HEADS = 6
DIFF_HEAD_DIM = 2 * HEAD_DIM
DIFF_WIDTH = DIFF_N_HEADS * DIFF_HEAD_DIM
EPS = 1e-6
NEG_INF = -1e30
IN_SIZES = (DIL_QKV, DIL_QKV, DIL_QKV, DIL_WIDTH,
            DIFF_WIDTH, DIFF_WIDTH, DIFF_WIDTH, DIFF_WIDTH, D_MODEL, D_MODEL)
N_IN = sum(IN_SIZES)
(OFF_QA, OFF_KA, OFF_VA, OFF_ZA, OFF_QB, OFF_KB, OFF_VB, OFF_ZB, OFF_GLA, OFF_GLB) = (
    int(v) for v in np.concatenate([[0], np.cumsum(IN_SIZES)[:-1]]))
LOG2E = 1.4426950408889634
Q_PRESCALE = HEAD_DIM ** -0.5 * LOG2E
LAMBDA_INIT = 0.8 - 0.6 * math.exp(-0.3 * 0)

VMEM_LIMIT_BYTES = 56 * 1024 * 1024

PROJ_TM = 1024
PROJ_TN = 2048
PROJ_NORM_CHUNK = 32
PROJ_NORM_ROWS = 128
DIL_TT = DIL_SPAN * DIL_GROUPS[-1][1]
DIL_UNITS = DIL_TT // DIL_SPAN
DIL_WIDE_R = DIL_GROUPS[-1][1]
DIL_WIDE_PITCH = 24
DIL_STAGE_ROWS = 256
DIL_MERGE_CHUNK = 64
DIFF_T = 1024
DIFF_DIAG_W = 256
DIFF_ONES_ROWS = 16
OUT_TM = 256


def _alibi_slopes(n):
    return [float(2.0 ** (-8.0 * (i + 1) / n)) for i in range(n)]


def _dot_nt(a, b):
    return lax.dot_general(a, b, (((1,), (1,)), ((), ())), preferred_element_type=F32)


def _silu(z):
    return z / (1.0 + jnp.exp(-z))


def _sigmoid(z):
    return 1.0 / (1.0 + jnp.exp(-z))


def _select_by_index(idx, values):
    out = jnp.float32(values[0])
    for i in range(1, len(values)):
        out = jnp.where(idx == i, jnp.float32(values[i]), out)
    return out


def _ada_kernel(c_ref, w_ref, b_ref, o_ref):
    c = c_ref[...]
    o_ref[...] = jnp.dot(_silu(c), w_ref[...], preferred_element_type=F32,
                         precision=lax.Precision.HIGHEST) + b_ref[...]


def _ada(c_pad, w_ada, b_ada):
    rows, d = c_pad.shape
    n = w_ada.shape[1]
    tn = 512
    return pl.pallas_call(
        _ada_kernel,
        grid=(n // tn,),
        in_specs=[pl.BlockSpec((rows, d), lambda j: (0, 0)),
                  pl.BlockSpec((d, tn), lambda j: (0, j)),
                  pl.BlockSpec((1, tn), lambda j: (0, j))],
        out_specs=pl.BlockSpec((rows, tn), lambda j: (0, j)),
        out_shape=jax.ShapeDtypeStruct((rows, n), F32),
        name="ada",
    )(c_pad, w_ada, b_ada)


def _proj_kernel(x_ref, mod_ref, g_ref, w_ref, cs_ref, o_ref, hn_a, hn_b):
    i = pl.program_id(0)
    j = pl.program_id(1)
    tm = x_ref.shape[0]
    shift = mod_ref[0, 0:1, :]
    mult = g_ref[...] * (1.0 + mod_ref[0, 1:2, :])

    def normalize(dst_ref, r0, rows):
        xx = x_ref[pl.ds(r0, rows), :]
        ms = jnp.mean(xx * xx, axis=-1, keepdims=True)
        dst_ref[pl.ds(r0, rows), :] = (xx * lax.rsqrt(ms + EPS) * mult + shift).astype(BF16)

    @pl.when(jnp.logical_and(i == 0, j == 0))
    def _():
        def body(c, carry):
            normalize(hn_a, pl.multiple_of(c * PROJ_NORM_CHUNK, PROJ_NORM_CHUNK), PROJ_NORM_CHUNK)
            return carry

        lax.fori_loop(0, tm // PROJ_NORM_CHUNK, body, 0)

    r0 = pl.multiple_of(jnp.clip((j - 1) * PROJ_NORM_ROWS, 0, tm - PROJ_NORM_ROWS), PROJ_NORM_ROWS)

    def step(cur_ref, nxt_ref):
        normalize(nxt_ref, r0, PROJ_NORM_ROWS)
        acc = jnp.dot(cur_ref[...], w_ref[...], preferred_element_type=F32)
        o_ref[...] = (acc * cs_ref[...]).astype(BF16)

    @pl.when(i % 2 == 0)
    def _():
        step(hn_a, hn_b)

    @pl.when(i % 2 == 1)
    def _():
        step(hn_b, hn_a)


def _proj(x2d, mod, g_norm, w_in_bf16, col_scale, seq):
    t, d = x2d.shape
    n = w_in_bf16.shape[1]
    tm, tn = PROJ_TM, PROJ_TN
    n_i, n_j = t // tm, n // tn
    tiles_per_batch = seq // tm
    assert (n_j - 1) * PROJ_NORM_ROWS >= tm and tm % PROJ_NORM_ROWS == 0

    def next_tile(i, j):
        return jnp.where(jnp.logical_and(i == 0, j == 0), 0, jnp.minimum(i + 1, n_i - 1))

    return pl.pallas_call(
        _proj_kernel,
        grid=(n_i, n_j),
        in_specs=[pl.BlockSpec((tm, d), lambda i, j: (next_tile(i, j), 0)),
                  pl.BlockSpec((1, 3, d), lambda i, j: (next_tile(i, j) // tiles_per_batch, 0, 0)),
                  pl.BlockSpec((1, d), lambda i, j: (0, 0)),
                  pl.BlockSpec((d, tn), lambda i, j: (0, j)),
                  pl.BlockSpec((1, tn), lambda i, j: (0, j))],
        out_specs=pl.BlockSpec((tm, tn), lambda i, j: (i, j)),
        out_shape=jax.ShapeDtypeStruct((t, n), BF16),
        scratch_shapes=[pltpu.VMEM((tm, d), BF16), pltpu.VMEM((tm, d), BF16)],
        compiler_params=pltpu.CompilerParams(
            dimension_semantics=("arbitrary", "arbitrary"),
            vmem_limit_bytes=VMEM_LIMIT_BYTES),
        name="proj",
    )(x2d, mod, g_norm, w_in_bf16, col_scale)


def _gather_residues(src_ref, src32, dst_ref, r, n_u, dst_pitch, dst_off):
    n32 = n_u * r // 2
    for c0 in range(0, n32, DIL_STAGE_ROWS):
        src32[c0:c0 + DIL_STAGE_ROWS, :] = pltpu.bitcast(
            src_ref[2 * c0:2 * (c0 + DIL_STAGE_ROWS), :], jnp.uint32)
    for pair in range(r // 2):
        for c0 in range(0, n_u, DIL_SPAN):
            w = src32[pl.ds(pair + (r // 2) * c0, DIL_SPAN, stride=r // 2), :]
            for p, half in ((2 * pair, w << 16), (2 * pair + 1, w & jnp.uint32(0xFFFF0000))):
                row = p * dst_pitch + dst_off + c0
                dst_ref[row:row + DIL_SPAN, :] = lax.bitcast_convert_type(half, F32).astype(BF16)


def _dil_kernel(*refs):
    ins, y_ref = refs[:10], refs[10]
    qd, w32, bm_ref = refs[11:14]
    kd, vd = refs[14:17], refs[17:20]
    o_scr, lse_scr = refs[20:23], refs[23:26]
    z_ref = ins[9]
    h = pl.program_id(1)
    first_tile = pl.program_id(2) == 0

    iq = lax.broadcasted_iota(jnp.int32, (DIL_SPAN, 2 * DIL_SPAN), 0)
    jk = lax.broadcasted_iota(jnp.int32, (DIL_SPAN, 2 * DIL_SPAN), 1)
    dist = (iq - jk + DIL_SPAN).astype(F32)
    valid = jnp.logical_and(jk >= iq, jk <= iq + DIL_SPAN)
    valid_first = jnp.logical_and(valid, jk >= DIL_SPAN)

    slopes = _alibi_slopes(DIL_N_HEADS)
    for g, (_, r) in enumerate(DIL_GROUPS):
        q_ref, kc_ref, vc_ref = ins[3 * g:3 * g + 3]
        kd_g, vd_g = kd[g], vd[g]
        nb = DIL_UNITS // r
        kpitch = (nb + 1) * DIL_SPAN
        coef = _select_by_index(
            h, [slopes[g * DIL_HEADS_PER_GROUP + i] * r * LOG2E for i in range(DIL_HEADS_PER_GROUP)])
        bias = -coef * dist
        bm_ref[0] = jnp.where(valid, bias, NEG_INF)
        bm_ref[1] = jnp.where(valid_first, bias, NEG_INF)

        @pl.when(first_tile)
        def _(r=r, kpitch=kpitch, kd_g=kd_g, vd_g=vd_g):
            for p in range(r):
                kd_g[p * kpitch:p * kpitch + DIL_SPAN, :] = jnp.zeros((DIL_SPAN, LANES), BF16)
                vd_g[p * kpitch:p * kpitch + DIL_SPAN, :] = jnp.zeros((DIL_SPAN, LANES), BF16)

        if r == 1:
            q_src = q_ref
            kd_g[DIL_SPAN:DIL_SPAN + DIL_TT, :] = kc_ref[...]
            vd_g[DIL_SPAN:DIL_SPAN + DIL_TT, :] = vc_ref[...]
        else:
            q_src = qd
            _gather_residues(q_ref, w32, qd, r, nb * DIL_SPAN, nb * DIL_SPAN, 0)
            _gather_residues(kc_ref, w32, kd_g, r, nb * DIL_SPAN, kpitch, DIL_SPAN)
            _gather_residues(vc_ref, w32, vd_g, r, nb * DIL_SPAN, kpitch, DIL_SPAN)

        for p in range(r):
            for ub in range(nb):
                rq = (p * nb + ub) * DIL_SPAN
                rk = p * kpitch + ub * DIL_SPAN
                bm = bm_ref[jnp.where(first_tile, 1, 0)] if ub == 0 else bm_ref[0]
                s = _dot_nt(q_src[rq:rq + DIL_SPAN, :], kd_g[rk:rk + 2 * DIL_SPAN, :]) + bm
                m = jnp.max(s, axis=-1, keepdims=True)
                e = jnp.exp2(s - m)
                l = jnp.sum(e, axis=-1, keepdims=True)
                o = jnp.dot(e.astype(BF16), vd_g[rk:rk + 2 * DIL_SPAN, :],
                            preferred_element_type=F32) * (1.0 / l)
                lse2 = jnp.broadcast_to(m + jnp.log(l) * LOG2E, (DIL_SPAN, LANES))
                if r == DIL_WIDE_R:
                    rows = pl.ds(p, DIL_SPAN, stride=DIL_WIDE_PITCH)
                elif r > 1:
                    rows = pl.ds(ub * (DIL_SPAN * r) + p, DIL_SPAN, stride=r)
                else:
                    rows = pl.ds(ub * DIL_SPAN, DIL_SPAN)
                o_scr[g][rows, :] = o
                lse_scr[g][rows, :] = lse2

        for p in range(r):
            last = p * kpitch + nb * DIL_SPAN
            kd_g[p * kpitch:p * kpitch + DIL_SPAN, :] = kd_g[last:last + DIL_SPAN, :]
            vd_g[p * kpitch:p * kpitch + DIL_SPAN, :] = vd_g[last:last + DIL_SPAN, :]

    def merge(i, carry):
        rows = pl.ds(pl.multiple_of(i * DIL_MERGE_CHUNK, DIL_MERGE_CHUNK), DIL_MERGE_CHUNK)

        def wide(ref):
            per = DIL_MERGE_CHUNK // DIL_WIDE_R
            return jnp.concatenate(
                [ref[pl.ds(pl.multiple_of((i * per + k) * DIL_WIDE_PITCH, 8), DIL_WIDE_R), :]
                 for k in range(per)], axis=0)

        l0, l1, l2 = lse_scr[0][rows, :], lse_scr[1][rows, :], wide(lse_scr[2])
        mx = jnp.maximum(jnp.maximum(l0, l1), l2)
        e0, e1, e2 = jnp.exp2(l0 - mx), jnp.exp2(l1 - mx), jnp.exp2(l2 - mx)
        om = (e0 * o_scr[0][rows, :] + e1 * o_scr[1][rows, :] + e2 * wide(o_scr[2])) / (e0 + e1 + e2)
        y_ref[rows, :] = (om * _silu(z_ref[rows, :].astype(F32))).astype(BF16)
        return carry

    lax.fori_loop(0, DIL_TT // DIL_MERGE_CHUNK, merge, 0, unroll=2)


def _dil(proj2d, batch, seq):
    tt = DIL_TT
    tiles = seq // tt

    def cur(col):
        return pl.BlockSpec((tt, LANES), lambda b, h, t: (b * tiles + t, col + h))

    in_specs = []
    for g in range(len(DIL_GROUPS)):
        in_specs += [cur((off + g * DIL_WIDTH) // LANES) for off in (OFF_QA, OFF_KA, OFF_VA)]
    in_specs.append(cur(OFF_ZA // LANES))
    kv_scratch = [pltpu.VMEM((tt + DIL_SPAN * r, LANES), BF16) for _, r in DIL_GROUPS]
    return pl.pallas_call(
        _dil_kernel,
        grid=(batch, DIL_HEADS_PER_GROUP, tiles),
        in_specs=in_specs,
        out_specs=pl.BlockSpec((tt, LANES), lambda b, h, t: (b * tiles + t, h)),
        out_shape=jax.ShapeDtypeStruct((batch * seq, DIL_WIDTH), BF16),
        scratch_shapes=(
            [pltpu.VMEM((tt, LANES), BF16), pltpu.VMEM((tt // 2, LANES), jnp.uint32),
             pltpu.VMEM((2, DIL_SPAN, 2 * DIL_SPAN), F32)]
            + kv_scratch + kv_scratch
            + [pltpu.VMEM((tt, LANES), F32)] * 2 + [pltpu.VMEM((DIL_SPAN * DIL_WIDE_PITCH, LANES), F32)]
            + [pltpu.VMEM((tt, LANES), F32)] * 2 + [pltpu.VMEM((DIL_SPAN * DIL_WIDE_PITCH, LANES), F32)]),
        compiler_params=pltpu.CompilerParams(
            dimension_semantics=("parallel", "parallel", "arbitrary"),
            vmem_limit_bytes=VMEM_LIMIT_BYTES),
        name="dil",
    )(*([proj2d] * 10))


def _diff_kernel(q_ref, k_ref, v_ref, lamv_ref, gsub_ref, o_ref,
                 vt_ref, st_ref, pt_ref, cb_ref, m_ref, a_ref, acc_ref, qt_ref, *, slopes2):
    h = pl.program_id(1)
    qi = pl.program_id(2)
    t = q_ref.shape[0]
    nk = vt_ref.shape[0]
    slope2 = _select_by_index(h, slopes2)

    @pl.when(qi == 0)
    def _():
        for j in range(nk):
            vt_ref[j, 0:DIFF_HEAD_DIM, :] = v_ref[j * t:(j + 1) * t, :].astype(F32).T.astype(BF16)
            vt_ref[j, DIFF_HEAD_DIM:, :] = jnp.ones((DIFF_ONES_ROWS, t), BF16)

    cb_ref[...] = slope2 * lax.broadcasted_iota(jnp.int32, (t, LANES), 0).astype(F32)
    m_ref[...] = jnp.full(m_ref.shape, NEG_INF, F32)
    acc_ref[...] = jnp.zeros(acc_ref.shape, F32)
    for c in range(2):
        qt_ref[c] = q_ref[:, c * HEAD_DIM:(c + 1) * HEAD_DIM].astype(F32).T.astype(BF16)

    def block(kj):
        k0 = pl.multiple_of(kj * t, t)
        cblk = slope2 * ((kj - qi) * t).astype(F32)
        for c in range(2):
            sl = slice(c * HEAD_DIM, (c + 1) * HEAD_DIM)
            st_ref[c] = jnp.dot(k_ref[pl.ds(k0, t), sl], qt_ref[c], preferred_element_type=F32)
        for c in range(2):
            for cg in range(t // LANES):
                cols = slice(cg * LANES, (cg + 1) * LANES)
                x = st_ref[c, :, cols] + cb_ref[...]
                m_old = m_ref[c, :, cols]
                m_new = jnp.maximum(m_old, jnp.max(x, axis=0, keepdims=True) + cblk)
                m_ref[c, :, cols] = m_new
                a_ref[c, :, cols] = jnp.exp2(m_old - m_new)
                pt_ref[c, :, cols] = jnp.exp2((x - (m_new - cblk)).astype(BF16))
        vt = vt_ref[kj]
        for c in range(2):
            acc_ref[c] = acc_ref[c] * a_ref[c] + jnp.dot(vt, pt_ref[c], preferred_element_type=F32)

    def diagonal_block():
        k0 = pl.multiple_of(qi * t, t)
        w = DIFF_DIAG_W
        for c in range(2):
            sl = slice(c * HEAD_DIM, (c + 1) * HEAD_DIM)
            for qh in range(t // w):
                kend = (qh + 1) * w
                qs = slice(qh * w, kend)
                st_ref[c, 0:kend, qs] = jnp.dot(k_ref[pl.ds(k0, kend), sl], qt_ref[c, :, qs],
                                                preferred_element_type=F32)
        for c in range(2):
            for cg in range(t // LANES):
                cols = slice(cg * LANES, (cg + 1) * LANES)
                kend = (cg * LANES // w + 1) * w
                key = lax.broadcasted_iota(jnp.int32, (kend, LANES), 0)
                qry = lax.broadcasted_iota(jnp.int32, (kend, LANES), 1) + cg * LANES
                x = jnp.where(key <= qry, st_ref[c, 0:kend, cols] + cb_ref[0:kend, :], NEG_INF)
                m_old = m_ref[c, :, cols]
                m_new = jnp.maximum(m_old, jnp.max(x, axis=0, keepdims=True))
                m_ref[c, :, cols] = m_new
                a_ref[c, :, cols] = jnp.exp2(m_old - m_new)
                pt_ref[c, 0:kend, cols] = jnp.exp2((x - m_new).astype(BF16))
        vt = vt_ref[qi]
        for c in range(2):
            for qh in range(t // w):
                kend = (qh + 1) * w
                qs = slice(qh * w, kend)
                acc_ref[c, :, qs] = (acc_ref[c, :, qs] * a_ref[c, :, qs]
                                     + jnp.dot(vt[:, 0:kend], pt_ref[c, 0:kend, qs],
                                               preferred_element_type=F32))

    def body(kj, carry):
        block(kj)
        return carry

    lax.fori_loop(0, qi, body, 0)
    diagonal_block()

    lamv = lamv_ref[...]
    lam = (jnp.exp(jnp.sum(lamv[0:1] * lamv[1:2], axis=-1, keepdims=True))
           - jnp.exp(jnp.sum(lamv[2:3] * lamv[3:4], axis=-1, keepdims=True)) + LAMBDA_INIT)
    d = DIFF_HEAD_DIM
    ot = (acc_ref[0, 0:d, :] * (1.0 / acc_ref[0, d:d + 1, :])
          - lam * (acc_ref[1, 0:d, :] * (1.0 / acc_ref[1, d:d + 1, :])))
    ms = jnp.mean(ot * ot, axis=0, keepdims=True)
    ot = ot * lax.rsqrt(ms + EPS)
    o_ref[...] = (ot.T * gsub_ref[...] * (1.0 - LAMBDA_INIT)).astype(o_ref.dtype)


def _diff(proj2d, lamv, g_subln, batch, seq):
    t = DIFF_T
    nq = seq // t
    wq, wk, wv = OFF_QB // DIFF_HEAD_DIM, OFF_KB // DIFF_HEAD_DIM, OFF_VB // DIFF_HEAD_DIM
    slopes2 = tuple(s * LOG2E for s in _alibi_slopes(DIFF_N_HEADS))
    vrows = DIFF_HEAD_DIM + DIFF_ONES_ROWS
    return pl.pallas_call(
        functools.partial(_diff_kernel, slopes2=slopes2),
        grid=(batch, DIFF_N_HEADS, nq),
        in_specs=[
            pl.BlockSpec((t, DIFF_HEAD_DIM), lambda b, h, q: (b * nq + q, wq + h)),
            pl.BlockSpec((seq, DIFF_HEAD_DIM), lambda b, h, q: (b, wk + h)),
            pl.BlockSpec((seq, DIFF_HEAD_DIM), lambda b, h, q: (b, wv + h)),
            pl.BlockSpec((4, HEAD_DIM), lambda b, h, q: (0, 0)),
            pl.BlockSpec((1, DIFF_HEAD_DIM), lambda b, h, q: (0, 0)),
        ],
        out_specs=pl.BlockSpec((t, DIFF_HEAD_DIM), lambda b, h, q: (b * nq + q, h)),
        out_shape=jax.ShapeDtypeStruct((batch * seq, DIFF_WIDTH), BF16),
        scratch_shapes=[
            pltpu.VMEM((nq, vrows, t), BF16),
            pltpu.VMEM((2, t, t), F32),
            pltpu.VMEM((2, t, t), BF16),
            pltpu.VMEM((t, LANES), F32),
            pltpu.VMEM((2, 1, t), F32),
            pltpu.VMEM((2, 1, t), F32),
            pltpu.VMEM((2, vrows, t), F32),
            pltpu.VMEM((2, HEAD_DIM, t), BF16),
        ],
        compiler_params=pltpu.CompilerParams(
            dimension_semantics=("parallel", "parallel", "arbitrary"),
            vmem_limit_bytes=VMEM_LIMIT_BYTES),
        name="diff",
    )(proj2d, proj2d, proj2d, lamv, g_subln)


def _out_kernel(ya_ref, yb_ref, zb0_ref, zb1_ref, zb2_ref, gla_ref, glb_ref, x_ref, mod_ref,
                gf_ref, wd_ref, wf_ref, wo_ref, out_ref, ybz_ref):
    zb_refs = (zb0_ref, zb1_ref, zb2_ref)
    wz = zb0_ref.shape[1]
    for i in range(3):
        sl = slice(i * wz, (i + 1) * wz)
        ybz_ref[:, sl] = (yb_ref[:, sl].astype(F32) * _silu(zb_refs[i][...].astype(F32))).astype(BF16)
    pa = jnp.dot(ya_ref[...], wd_ref[...], preferred_element_type=F32)
    pb = jnp.dot(ybz_ref[...], wf_ref[...], preferred_element_type=F32)
    merged = (_sigmoid(gla_ref[...].astype(F32)) * pa + _sigmoid(glb_ref[...].astype(F32)) * pb)
    delta = jnp.dot(merged.astype(BF16), wo_ref[...], preferred_element_type=F32)
    y = x_ref[...] + mod_ref[0, 2:3, :] * delta
    ms = jnp.mean(y * y, axis=-1, keepdims=True)
    out_ref[...] = y * lax.rsqrt(ms + EPS) * gf_ref[...]


def _const_spec(shape):
    return pl.BlockSpec(shape, lambda i: (0,) * len(shape), pipeline_mode=pl.Buffered(1))


def _out(ya, yb, proj2d, x2d, mod, g_final, wd, wf, wo, seq):
    t, d = x2d.shape
    tm = OUT_TM
    tiles_per_batch = seq // tm
    wz = 512
    zb_blk = OFF_ZB // wz
    row = lambda w: pl.BlockSpec((tm, w), lambda i: (i, 0))
    in_specs = (
        [row(DIL_WIDTH), row(DIFF_WIDTH)]
        + [pl.BlockSpec((tm, wz), functools.partial(lambda i, k: (i, zb_blk + k), k=k))
           for k in range(3)]
        + [pl.BlockSpec((tm, d), lambda i: (i, OFF_GLA // D_MODEL)),
           pl.BlockSpec((tm, d), lambda i: (i, OFF_GLB // D_MODEL)),
           row(d),
           pl.BlockSpec((1, 3, d), lambda i: (i // tiles_per_batch, 0, 0)),
           _const_spec((1, d)),
           _const_spec(wd.shape), _const_spec(wf.shape), _const_spec(wo.shape)])
    return pl.pallas_call(
        _out_kernel,
        grid=(t // tm,),
        in_specs=in_specs,
        out_specs=row(d),
        out_shape=jax.ShapeDtypeStruct((t, d), F32),
        scratch_shapes=[pltpu.VMEM((tm, DIFF_WIDTH), BF16)],
        compiler_params=pltpu.CompilerParams(
            dimension_semantics=("parallel",),
            vmem_limit_bytes=VMEM_LIMIT_BYTES),
        name="out",
    )(ya, yb, proj2d, proj2d, proj2d, proj2d, proj2d, x2d, mod, g_final, wd, wf, wo)


def _q_column_scale():
    cs = np.ones((1, N_IN), np.float32)
    cs[:, OFF_QA:OFF_QA + DIL_QKV] = Q_PRESCALE
    cs[:, OFF_QB:OFF_QB + DIFF_WIDTH] = Q_PRESCALE
    return jnp.asarray(cs)


def kernel(x, c, w_ada, b_ada, g_norm, w_in, w_o_dil, w_o_diff, w_out,
           lambda_q1, lambda_k1, lambda_q2, lambda_k2, g_subln, g_final):
    batch, seq, d = x.shape
    assert w_ada.shape[0] == 1 and d == D_MODEL and w_in.shape[2] == N_IN
    assert seq % DIL_TT == 0 and seq % PROJ_TM == 0 and seq % DIFF_T == 0
    rows = -(-batch // 8) * 8
    c_pad = jnp.pad(c, ((0, rows - batch), (0, 0)))
    x2d = x.reshape(batch * seq, d)
    ada = _ada(c_pad, w_ada[0], b_ada[0].reshape(1, 3 * d))[:batch]
    mod = ada.reshape(batch, 3, d)
    proj2d = _proj(x2d, mod, g_norm[0].reshape(1, d), w_in[0].astype(BF16), _q_column_scale(), seq)
    ya = _dil(proj2d, batch, seq)
    lamv = jnp.stack([lambda_q1[0], lambda_k1[0], lambda_q2[0], lambda_k2[0]])
    yb = _diff(proj2d, lamv, g_subln[0].reshape(1, DIFF_HEAD_DIM), batch, seq)
    out = _out(ya, yb, proj2d, x2d, mod, g_final.reshape(1, d), w_o_dil[0].astype(BF16),
               w_o_diff[0].astype(BF16), w_out[0].astype(BF16), seq)
    return out.reshape(batch, seq, d)
```

```python
import functools
import math

import numpy as np
import jax
import jax.numpy as jnp
from jax import lax
from jax.experimental import pallas as pl
from jax.experimental.pallas import tpu as pltpu

F32 = jnp.float32
BF16 = jnp.bfloat16

D_MODEL = 2048
HEAD_DIM = 128
LANES = 128
DIL_GROUPS = ((128, 1), (512, 4), (2048, 16))
DIL_HEADS_PER_GROUP = 8
DIL_N_HEADS = DIL_HEADS_PER_GROUP * len(DIL_GROUPS)
DIL_QKV = DIL_N_HEADS * HEAD_DIM
DIL_WIDTH = DIL_HEADS_PER_GROUP * HEAD_DIM
DIL_SPAN = 128
DIFF_N_HEADS = 6
DIFF_HEAD_DIM = 2 * HEAD_DIM
DIFF_WIDTH = DIFF_N_HEADS * DIFF_HEAD_DIM
EPS = 1e-6
NEG_INF = -1e30
IN_SIZES = (DIL_QKV, DIL_QKV, DIL_QKV, DIL_WIDTH,
            DIFF_WIDTH, DIFF_WIDTH, DIFF_WIDTH, DIFF_WIDTH, D_MODEL, D_MODEL)
N_IN = sum(IN_SIZES)
(OFF_QA, OFF_KA, OFF_VA, OFF_ZA, OFF_QB, OFF_KB, OFF_VB, OFF_ZB, OFF_GLA, OFF_GLB) = (
    int(v) for v in np.concatenate([[0], np.cumsum(IN_SIZES)[:-1]]))
LOG2E = 1.4426950408889634
Q_PRESCALE = HEAD_DIM ** -0.5 * LOG2E
LAMBDA_INIT = 0.8 - 0.6 * math.exp(-0.3 * 0)

VMEM_LIMIT_BYTES = 56 * 1024 * 1024

PROJ_TM = 1024
PROJ_TN = 2048
PROJ_NORM_CHUNK = 32
PROJ_NORM_ROWS = 128
DIL_TT = DIL_SPAN * DIL_GROUPS[-1][1]
DIL_UNITS = DIL_TT // DIL_SPAN
DIL_WIDE_R = DIL_GROUPS[-1][1]
DIL_WIDE_PITCH = 24
DIL_MERGE_CHUNK = 64
DIFF_T = 1024
DIFF_DIAG_W = 256
DIFF_ONES_ROWS = 16
OUT_TM = 256


def _alibi_slopes(n):
    return [float(2.0 ** (-8.0 * (i + 1) / n)) for i in range(n)]


def _dot_nt(a, b):
    return lax.dot_general(a, b, (((1,), (1,)), ((), ())), preferred_element_type=F32)


def _silu(z):
    return z / (1.0 + jnp.exp(-z))


def _sigmoid(z):
    return 1.0 / (1.0 + jnp.exp(-z))


def _select_by_index(idx, values):
    out = jnp.float32(values[0])
    for i in range(1, len(values)):
        out = jnp.where(idx == i, jnp.float32(values[i]), out)
    return out


def _ada_kernel(c_ref, w_ref, b_ref, o_ref):
    c = c_ref[...]
    o_ref[...] = jnp.dot(_silu(c), w_ref[...], preferred_element_type=F32,
                         precision=lax.Precision.HIGHEST) + b_ref[...]


def _ada(c_pad, w_ada, b_ada):
    rows, d = c_pad.shape
    n = w_ada.shape[1]
    tn = 512
    return pl.pallas_call(
        _ada_kernel,
        grid=(n // tn,),
        in_specs=[pl.BlockSpec((rows, d), lambda j: (0, 0)),
                  pl.BlockSpec((d, tn), lambda j: (0, j)),
                  pl.BlockSpec((1, tn), lambda j: (0, j))],
        out_specs=pl.BlockSpec((rows, tn), lambda j: (0, j)),
        out_shape=jax.ShapeDtypeStruct((rows, n), F32),
        name="ada",
    )(c_pad, w_ada, b_ada)


def _proj_kernel(x_ref, mod_ref, g_ref, w_ref, cs_ref, o_ref, hn_a, hn_b):
    i = pl.program_id(0)
    j = pl.program_id(1)
    tm = x_ref.shape[0]
    shift = mod_ref[0, 0:1, :]
    mult = g_ref[...] * (1.0 + mod_ref[0, 1:2, :])

    def normalize(dst_ref, r0, rows):
        xx = x_ref[pl.ds(r0, rows), :]
        ms = jnp.mean(xx * xx, axis=-1, keepdims=True)
        dst_ref[pl.ds(r0, rows), :] = (xx * lax.rsqrt(ms + EPS) * mult + shift).astype(BF16)

    @pl.when(jnp.logical_and(i == 0, j == 0))
    def _():
        def body(c, carry):
            normalize(hn_a, pl.multiple_of(c * PROJ_NORM_CHUNK, PROJ_NORM_CHUNK), PROJ_NORM_CHUNK)
            return carry

        lax.fori_loop(0, tm // PROJ_NORM_CHUNK, body, 0)

    r0 = pl.multiple_of(jnp.clip((j - 1) * PROJ_NORM_ROWS, 0, tm - PROJ_NORM_ROWS), PROJ_NORM_ROWS)

    def step(cur_ref, nxt_ref):
        normalize(nxt_ref, r0, PROJ_NORM_ROWS)
        acc = jnp.dot(cur_ref[...], w_ref[...], preferred_element_type=F32)
        o_ref[...] = (acc * cs_ref[...]).astype(BF16)

    @pl.when(i % 2 == 0)
    def _():
        step(hn_a, hn_b)

    @pl.when(i % 2 == 1)
    def _():
        step(hn_b, hn_a)


def _proj(x2d, mod, g_norm, w_in_bf16, col_scale, seq):
    t, d = x2d.shape
    n = w_in_bf16.shape[1]
    tm, tn = PROJ_TM, PROJ_TN
    n_i, n_j = t // tm, n // tn
    tiles_per_batch = seq // tm
    assert (n_j - 1) * PROJ_NORM_ROWS >= tm and tm % PROJ_NORM_ROWS == 0

    def next_tile(i, j):
        return jnp.where(jnp.logical_and(i == 0, j == 0), 0, jnp.minimum(i + 1, n_i - 1))

    return pl.pallas_call(
        _proj_kernel,
        grid=(n_i, n_j),
        in_specs=[pl.BlockSpec((tm, d), lambda i, j: (next_tile(i, j), 0)),
                  pl.BlockSpec((1, 3, d), lambda i, j: (next_tile(i, j) // tiles_per_batch, 0, 0)),
                  pl.BlockSpec((1, d), lambda i, j: (0, 0)),
                  pl.BlockSpec((d, tn), lambda i, j: (0, j)),
                  pl.BlockSpec((1, tn), lambda i, j: (0, j))],
        out_specs=pl.BlockSpec((tm, tn), lambda i, j: (i, j)),
        out_shape=jax.ShapeDtypeStruct((t, n), BF16),
        scratch_shapes=[pltpu.VMEM((tm, d), BF16), pltpu.VMEM((tm, d), BF16)],
        compiler_params=pltpu.CompilerParams(
            dimension_semantics=("arbitrary", "arbitrary"),
            vmem_limit_bytes=VMEM_LIMIT_BYTES),
        name="proj",
    )(x2d, mod, g_norm, w_in_bf16, col_scale)


def _gather_residues(src_ref, stage, dst_ref, r, n_u, dst_pitch, dst_off):
    pitch = DIL_WIDE_PITCH if r == DIL_WIDE_R else r
    for c0 in range(0, n_u * r, DIL_SPAN):
        x = src_ref[c0:c0 + DIL_SPAN, :].astype(F32)
        if pitch == r:
            stage[c0:c0 + DIL_SPAN, :] = x
        else:
            for k in range(DIL_SPAN // r):
                row = (c0 // r + k) * pitch
                stage[row:row + r, :] = x[k * r:(k + 1) * r, :]
    for p in range(r):
        for c0 in range(0, n_u, DIL_SPAN):
            row = p * dst_pitch + dst_off + c0
            dst_ref[row:row + DIL_SPAN, :] = stage[
                pl.ds(p + pitch * c0, DIL_SPAN, stride=pitch), :].astype(BF16)


def _dil_kernel(*refs):
    ins, y_ref = refs[:10], refs[10]
    qd, stage, bm_ref = refs[11:14]
    kd, vd = refs[14:17], refs[17:20]
    o_scr, lse_scr = refs[20:23], refs[23:26]
    z_ref = ins[9]
    h = pl.program_id(1)
    first_tile = pl.program_id(2) == 0

    iq = lax.broadcasted_iota(jnp.int32, (DIL_SPAN, 2 * DIL_SPAN), 0)
    jk = lax.broadcasted_iota(jnp.int32, (DIL_SPAN, 2 * DIL_SPAN), 1)
    dist = (iq - jk + DIL_SPAN).astype(F32)
    valid = jnp.logical_and(jk >= iq, jk <= iq + DIL_SPAN)
    valid_first = jnp.logical_and(valid, jk >= DIL_SPAN)

    slopes = _alibi_slopes(DIL_N_HEADS)
    for g, (_, r) in enumerate(DIL_GROUPS):
        q_ref, kc_ref, vc_ref = ins[3 * g:3 * g + 3]
        kd_g, vd_g = kd[g], vd[g]
        nb = DIL_UNITS // r
        kpitch = (nb + 1) * DIL_SPAN
        coef = _select_by_index(
            h, [slopes[g * DIL_HEADS_PER_GROUP + i] * r * LOG2E for i in range(DIL_HEADS_PER_GROUP)])
        bias = -coef * dist
        bm_ref[0] = jnp.where(valid, bias, NEG_INF)
        bm_ref[1] = jnp.where(valid_first, bias, NEG_INF)

        @pl.when(first_tile)
        def _(r=r, kpitch=kpitch, kd_g=kd_g, vd_g=vd_g):
            for p in range(r):
                kd_g[p * kpitch:p * kpitch + DIL_SPAN, :] = jnp.zeros((DIL_SPAN, LANES), BF16)
                vd_g[p * kpitch:p * kpitch + DIL_SPAN, :] = jnp.zeros((DIL_SPAN, LANES), BF16)

        if r == 1:
            q_src = q_ref
            kd_g[DIL_SPAN:DIL_SPAN + DIL_TT, :] = kc_ref[...]
            vd_g[DIL_SPAN:DIL_SPAN + DIL_TT, :] = vc_ref[...]
        else:
            q_src = qd
            _gather_residues(q_ref, stage, qd, r, nb * DIL_SPAN, nb * DIL_SPAN, 0)
            _gather_residues(kc_ref, stage, kd_g, r, nb * DIL_SPAN, kpitch, DIL_SPAN)
            _gather_residues(vc_ref, stage, vd_g, r, nb * DIL_SPAN, kpitch, DIL_SPAN)

        for p in range(r):
            for ub in range(nb):
                rq = (p * nb + ub) * DIL_SPAN
                rk = p * kpitch + ub * DIL_SPAN
                bm = bm_ref[jnp.where(first_tile, 1, 0)] if ub == 0 else bm_ref[0]
                s = _dot_nt(q_src[rq:rq + DIL_SPAN, :], kd_g[rk:rk + 2 * DIL_SPAN, :]) + bm
                m = jnp.max(s, axis=-1, keepdims=True)
                e = jnp.exp2(s - m)
                l = jnp.sum(e, axis=-1, keepdims=True)
                o = jnp.dot(e.astype(BF16), vd_g[rk:rk + 2 * DIL_SPAN, :],
                            preferred_element_type=F32) * (1.0 / l)
                lse2 = jnp.broadcast_to(m + jnp.log(l) * LOG2E, (DIL_SPAN, LANES))
                if r == DIL_WIDE_R:
                    rows = pl.ds(p, DIL_SPAN, stride=DIL_WIDE_PITCH)
                elif r > 1:
                    rows = pl.ds(ub * (DIL_SPAN * r) + p, DIL_SPAN, stride=r)
                else:
                    rows = pl.ds(ub * DIL_SPAN, DIL_SPAN)
                o_scr[g][rows, :] = o
                lse_scr[g][rows, :] = lse2

        for p in range(r):
            last = p * kpitch + nb * DIL_SPAN
            kd_g[p * kpitch:p * kpitch + DIL_SPAN, :] = kd_g[last:last + DIL_SPAN, :]
            vd_g[p * kpitch:p * kpitch + DIL_SPAN, :] = vd_g[last:last + DIL_SPAN, :]

    def merge(i, carry):
        rows = pl.ds(pl.multiple_of(i * DIL_MERGE_CHUNK, DIL_MERGE_CHUNK), DIL_MERGE_CHUNK)

        def wide(ref):
            per = DIL_MERGE_CHUNK // DIL_WIDE_R
            return jnp.concatenate(
                [ref[pl.ds(pl.multiple_of((i * per + k) * DIL_WIDE_PITCH, 8), DIL_WIDE_R), :]
                 for k in range(per)], axis=0)

        l0, l1, l2 = lse_scr[0][rows, :], lse_scr[1][rows, :], wide(lse_scr[2])
        mx = jnp.maximum(jnp.maximum(l0, l1), l2)
        e0, e1, e2 = jnp.exp2(l0 - mx), jnp.exp2(l1 - mx), jnp.exp2(l2 - mx)
        om = (e0 * o_scr[0][rows, :] + e1 * o_scr[1][rows, :] + e2 * wide(o_scr[2])) / (e0 + e1 + e2)
        y_ref[rows, :] = (om * _silu(z_ref[rows, :].astype(F32))).astype(BF16)
        return carry

    lax.fori_loop(0, DIL_TT // DIL_MERGE_CHUNK, merge, 0, unroll=2)


def _dil(proj2d, batch, seq):
    tt = DIL_TT
    tiles = seq // tt

    def cur(col):
        return pl.BlockSpec((tt, LANES), lambda b, h, t: (b * tiles + t, col + h))

    in_specs = []
    for g in range(len(DIL_GROUPS)):
        in_specs += [cur((off + g * DIL_WIDTH) // LANES) for off in (OFF_QA, OFF_KA, OFF_VA)]
    in_specs.append(cur(OFF_ZA // LANES))
    kv_scratch = [pltpu.VMEM((tt + DIL_SPAN * r, LANES), BF16) for _, r in DIL_GROUPS]
    return pl.pallas_call(
        _dil_kernel,
        grid=(batch, DIL_HEADS_PER_GROUP, tiles),
        in_specs=in_specs,
        out_specs=pl.BlockSpec((tt, LANES), lambda b, h, t: (b * tiles + t, h)),
        out_shape=jax.ShapeDtypeStruct((batch * seq, DIL_WIDTH), BF16),
        scratch_shapes=(
            [pltpu.VMEM((tt, LANES), BF16), pltpu.VMEM((DIL_SPAN * DIL_WIDE_PITCH, LANES), F32),
             pltpu.VMEM((2, DIL_SPAN, 2 * DIL_SPAN), F32)]
            + kv_scratch + kv_scratch
            + [pltpu.VMEM((tt, LANES), F32)] * 2 + [pltpu.VMEM((DIL_SPAN * DIL_WIDE_PITCH, LANES), F32)]
            + [pltpu.VMEM((tt, LANES), F32)] * 2 + [pltpu.VMEM((DIL_SPAN * DIL_WIDE_PITCH, LANES), F32)]),
        compiler_params=pltpu.CompilerParams(
            dimension_semantics=("parallel", "parallel", "arbitrary"),
            vmem_limit_bytes=VMEM_LIMIT_BYTES),
        name="dil",
    )(*([proj2d] * 10))


def _diff_kernel(q_ref, k_ref, v_ref, lamv_ref, gsub_ref, o_ref,
                 vt_ref, st_ref, pt_ref, cb_ref, m_ref, a_ref, acc_ref, *, slopes2):
    h = pl.program_id(1)
    qi = pl.program_id(2)
    t = q_ref.shape[0]
    nk = vt_ref.shape[0]
    slope2 = _select_by_index(h, slopes2)

    @pl.when(qi == 0)
    def _():
        for j in range(nk):
            vt_ref[j, 0:DIFF_HEAD_DIM, :] = v_ref[j * t:(j + 1) * t, :].astype(F32).T.astype(BF16)
            vt_ref[j, DIFF_HEAD_DIM:, :] = jnp.ones((DIFF_ONES_ROWS, t), BF16)

    cb_ref[...] = slope2 * lax.broadcasted_iota(jnp.int32, (t, LANES), 0).astype(F32)
    m_ref[...] = jnp.full(m_ref.shape, NEG_INF, F32)
    acc_ref[...] = jnp.zeros(acc_ref.shape, F32)

    def block(kj):
        k0 = pl.multiple_of(kj * t, t)
        cblk = slope2 * ((kj - qi) * t).astype(F32)
        for c in range(2):
            sl = slice(c * HEAD_DIM, (c + 1) * HEAD_DIM)
            st_ref[c] = _dot_nt(k_ref[pl.ds(k0, t), sl], q_ref[:, sl])
        for c in range(2):
            for cg in range(t // LANES):
                cols = slice(cg * LANES, (cg + 1) * LANES)
                x = st_ref[c, :, cols] + cb_ref[...]
                m_old = m_ref[c, :, cols]
                m_new = jnp.maximum(m_old, jnp.max(x, axis=0, keepdims=True) + cblk)
                m_ref[c, :, cols] = m_new
                a_ref[c, :, cols] = jnp.exp2(m_old - m_new)
                pt_ref[c, :, cols] = jnp.exp2((x - (m_new - cblk)).astype(BF16))
        vt = vt_ref[kj]
        for c in range(2):
            acc_ref[c] = acc_ref[c] * a_ref[c] + jnp.dot(vt, pt_ref[c], preferred_element_type=F32)

    def diagonal_block():
        k0 = pl.multiple_of(qi * t, t)
        w = DIFF_DIAG_W
        for c in range(2):
            sl = slice(c * HEAD_DIM, (c + 1) * HEAD_DIM)
            for qh in range(t // w):
                kend = (qh + 1) * w
                qs = slice(qh * w, kend)
                st_ref[c, 0:kend, qs] = _dot_nt(k_ref[pl.ds(k0, kend), sl], q_ref[qs, sl])
        for c in range(2):
            for cg in range(t // LANES):
                cols = slice(cg * LANES, (cg + 1) * LANES)
                kend = (cg * LANES // w + 1) * w
                key = lax.broadcasted_iota(jnp.int32, (kend, LANES), 0)
                qry = lax.broadcasted_iota(jnp.int32, (kend, LANES), 1) + cg * LANES
                x = jnp.where(key <= qry, st_ref[c, 0:kend, cols] + cb_ref[0:kend, :], NEG_INF)
                m_old = m_ref[c, :, cols]
                m_new = jnp.maximum(m_old, jnp.max(x, axis=0, keepdims=True))
                m_ref[c, :, cols] = m_new
                a_ref[c, :, cols] = jnp.exp2(m_old - m_new)
                pt_ref[c, 0:kend, cols] = jnp.exp2((x - m_new).astype(BF16))
        vt = vt_ref[qi]
        for c in range(2):
            for qh in range(t // w):
                kend = (qh + 1) * w
                qs = slice(qh * w, kend)
                acc_ref[c, :, qs] = (acc_ref[c, :, qs] * a_ref[c, :, qs]
                                     + jnp.dot(vt[:, 0:kend], pt_ref[c, 0:kend, qs],
                                               preferred_element_type=F32))

    def body(kj, carry):
        block(kj)
        return carry

    lax.fori_loop(0, qi, body, 0)
    diagonal_block()

    lamv = lamv_ref[...]
    lam = (jnp.exp(jnp.sum(lamv[0:1] * lamv[1:2], axis=-1, keepdims=True))
           - jnp.exp(jnp.sum(lamv[2:3] * lamv[3:4], axis=-1, keepdims=True)) + LAMBDA_INIT)
    d = DIFF_HEAD_DIM
    ot = (acc_ref[0, 0:d, :] * (1.0 / acc_ref[0, d:d + 1, :])
          - lam * (acc_ref[1, 0:d, :] * (1.0 / acc_ref[1, d:d + 1, :])))
    ms = jnp.mean(ot * ot, axis=0, keepdims=True)
    ot = ot * lax.rsqrt(ms + EPS)
    o_ref[...] = (ot.T * gsub_ref[...] * (1.0 - LAMBDA_INIT)).astype(o_ref.dtype)


def _diff(proj2d, lamv, g_subln, batch, seq):
    t = DIFF_T
    nq = seq // t
    wq, wk, wv = OFF_QB // DIFF_HEAD_DIM, OFF_KB // DIFF_HEAD_DIM, OFF_VB // DIFF_HEAD_DIM
    slopes2 = tuple(s * LOG2E for s in _alibi_slopes(DIFF_N_HEADS))
    vrows = DIFF_HEAD_DIM + DIFF_ONES_ROWS
    return pl.pallas_call(
        functools.partial(_diff_kernel, slopes2=slopes2),
        grid=(batch, DIFF_N_HEADS, nq),
        in_specs=[
            pl.BlockSpec((t, DIFF_HEAD_DIM), lambda b, h, q: (b * nq + q, wq + h)),
            pl.BlockSpec((seq, DIFF_HEAD_DIM), lambda b, h, q: (b, wk + h)),
            pl.BlockSpec((seq, DIFF_HEAD_DIM), lambda b, h, q: (b, wv + h)),
            pl.BlockSpec((4, HEAD_DIM), lambda b, h, q: (0, 0)),
            pl.BlockSpec((1, DIFF_HEAD_DIM), lambda b, h, q: (0, 0)),
        ],
        out_specs=pl.BlockSpec((t, DIFF_HEAD_DIM), lambda b, h, q: (b * nq + q, h)),
        out_shape=jax.ShapeDtypeStruct((batch * seq, DIFF_WIDTH), BF16),
        scratch_shapes=[
            pltpu.VMEM((nq, vrows, t), BF16),
            pltpu.VMEM((2, t, t), F32),
            pltpu.VMEM((2, t, t), BF16),
            pltpu.VMEM((t, LANES), F32),
            pltpu.VMEM((2, 1, t), F32),
            pltpu.VMEM((2, 1, t), F32),
            pltpu.VMEM((2, vrows, t), F32),
        ],
        compiler_params=pltpu.CompilerParams(
            dimension_semantics=("parallel", "parallel", "arbitrary"),
            vmem_limit_bytes=VMEM_LIMIT_BYTES),
        name="diff",
    )(proj2d, proj2d, proj2d, lamv, g_subln)


def _out_kernel(ya_ref, yb_ref, zb0_ref, zb1_ref, zb2_ref, gla_ref, glb_ref, x_ref, mod_ref,
                gf_ref, wd_ref, wf_ref, wo_ref, out_ref, ybz_ref):
    zb_refs = (zb0_ref, zb1_ref, zb2_ref)
    wz = zb0_ref.shape[1]
    for i in range(3):
        sl = slice(i * wz, (i + 1) * wz)
        ybz_ref[:, sl] = (yb_ref[:, sl].astype(F32) * _silu(zb_refs[i][...].astype(F32))).astype(BF16)
    pa = jnp.dot(ya_ref[...], wd_ref[...], preferred_element_type=F32)
    pb = jnp.dot(ybz_ref[...], wf_ref[...], preferred_element_type=F32)
    merged = (_sigmoid(gla_ref[...].astype(F32)) * pa + _sigmoid(glb_ref[...].astype(F32)) * pb)
    delta = jnp.dot(merged.astype(BF16), wo_ref[...], preferred_element_type=F32)
    y = x_ref[...] + mod_ref[0, 2:3, :] * delta
    ms = jnp.mean(y * y, axis=-1, keepdims=True)
    out_ref[...] = y * lax.rsqrt(ms + EPS) * gf_ref[...]


def _const_spec(shape):
    return pl.BlockSpec(shape, lambda i: (0,) * len(shape), pipeline_mode=pl.Buffered(1))


def _out(ya, yb, proj2d, x2d, mod, g_final, wd, wf, wo, seq):
    t, d = x2d.shape
    tm = OUT_TM
    tiles_per_batch = seq // tm
    wz = 512
    zb_blk = OFF_ZB // wz
    row = lambda w: pl.BlockSpec((tm, w), lambda i: (i, 0))
    in_specs = (
        [row(DIL_WIDTH), row(DIFF_WIDTH)]
        + [pl.BlockSpec((tm, wz), functools.partial(lambda i, k: (i, zb_blk + k), k=k))
           for k in range(3)]
        + [pl.BlockSpec((tm, d), lambda i: (i, OFF_GLA // D_MODEL)),
           pl.BlockSpec((tm, d), lambda i: (i, OFF_GLB // D_MODEL)),
           row(d),
           pl.BlockSpec((1, 3, d), lambda i: (i // tiles_per_batch, 0, 0)),
           _const_spec((1, d)),
           _const_spec(wd.shape), _const_spec(wf.shape), _const_spec(wo.shape)])
    return pl.pallas_call(
        _out_kernel,
        grid=(t // tm,),
        in_specs=in_specs,
        out_specs=row(d),
        out_shape=jax.ShapeDtypeStruct((t, d), F32),
        scratch_shapes=[pltpu.VMEM((tm, DIFF_WIDTH), BF16)],
        compiler_params=pltpu.CompilerParams(
            dimension_semantics=("parallel",),
            vmem_limit_bytes=VMEM_LIMIT_BYTES),
        name="out",
    )(ya, yb, proj2d, proj2d, proj2d, proj2d, proj2d, x2d, mod, g_final, wd, wf, wo)


def _q_column_scale():
    cs = np.ones((1, N_IN), np.float32)
    cs[:, OFF_QA:OFF_QA + DIL_QKV] = Q_PRESCALE
    cs[:, OFF_QB:OFF_QB + DIFF_WIDTH] = Q_PRESCALE
    return jnp.asarray(cs)


def kernel(x, c, w_ada, b_ada, g_norm, w_in, w_o_dil, w_o_diff, w_out,
           lambda_q1, lambda_k1, lambda_q2, lambda_k2, g_subln, g_final):
    batch, seq, d = x.shape
    assert w_ada.shape[0] == 1 and d == D_MODEL and w_in.shape[2] == N_IN
    assert seq % DIL_TT == 0 and seq % PROJ_TM == 0 and seq % DIFF_T == 0
    rows = -(-batch // 8) * 8
    c_pad = jnp.pad(c, ((0, rows - batch), (0, 0)))
    x2d = x.reshape(batch * seq, d)
    ada = _ada(c_pad, w_ada[0], b_ada[0].reshape(1, 3 * d))[:batch]
    mod = ada.reshape(batch, 3, d)
    proj2d = _proj(x2d, mod, g_norm[0].reshape(1, d), w_in[0].astype(BF16), _q_column_scale(), seq)
    ya = _dil(proj2d, batch, seq)
    lamv = jnp.stack([lambda_q1[0], lambda_k1[0], lambda_q2[0], lambda_k2[0]])
    yb = _diff(proj2d, lamv, g_subln[0].reshape(1, DIFF_HEAD_DIM), batch, seq)
    out = _out(ya, yb, proj2d, x2d, mod, g_final.reshape(1, d), w_o_dil[0].astype(BF16),
               w_o_diff[0].astype(BF16), w_out[0].astype(BF16), seq)
    return out.reshape(batch, seq, d)
```

```python
import functools
import math

import numpy as np
import jax
import jax.numpy as jnp
from jax import lax
from jax.experimental import pallas as pl
from jax.experimental.pallas import tpu as pltpu

F32 = jnp.float32
BF16 = jnp.bfloat16

D_MODEL = 2048
HEAD_DIM = 128
LANES = 128
SUBLANES = 8
DIL_GROUPS = ((128, 1), (512, 4), (2048, 16))
DIL_HEADS_PER_GROUP = 8
DIL_N_HEADS = DIL_HEADS_PER_GROUP * len(DIL_GROUPS)
DIL_QKV = DIL_N_HEADS * HEAD_DIM
DIL_WIDTH = DIL_HEADS_PER_GROUP * HEAD_DIM
DIL_SPAN = 128
DIFF_N_HEADS = 6
DIFF_HEAD_DIM = 2 * HEAD_DIM
DIFF_WIDTH = DIFF_N_HEADS * DIFF_HEAD_DIM
EPS = 1e-6
NEG_INF = -1e30
IN_SIZES = (DIL_QKV, DIL_QKV, DIL_QKV, DIL_WIDTH,
            DIFF_WIDTH, DIFF_WIDTH, DIFF_WIDTH, DIFF_WIDTH, D_MODEL, D_MODEL)
N_IN = sum(IN_SIZES)
(OFF_QA, OFF_KA, OFF_VA, OFF_ZA, OFF_QB, OFF_KB, OFF_VB, OFF_ZB, OFF_GLA, OFF_GLB) = (
    int(v) for v in np.concatenate([[0], np.cumsum(IN_SIZES)[:-1]]))
LOG2E = 1.4426950408889634
Q_PRESCALE = HEAD_DIM ** -0.5 * LOG2E
LAMBDA_INIT = 0.8 - 0.6 * math.exp(-0.3 * 0)

VMEM_LIMIT_BYTES = 56 * 1024 * 1024

ADA_TN = 512
PROJ_TM = 1024
PROJ_TN = 2048
PROJ_NORM_CHUNK = 32
PROJ_NORM_ROWS = 128
DIL_TT = DIL_SPAN * DIL_GROUPS[-1][1]
DIL_UNITS = DIL_TT // DIL_SPAN
DIL_WIDE_R = DIL_GROUPS[-1][1]
DIL_WIDE_PITCH = 24
DIL_MERGE_CHUNK = 64
DIFF_T = 1024
DIFF_DIAG_W = 256
DIFF_ONES_ROWS = 16
OUT_TM = 256
OUT_ZB_W = 512


def _alibi_slopes(n):
    return [float(2.0 ** (-8.0 * (i + 1) / n)) for i in range(n)]


def _dot_nt(a, b):
    return lax.dot_general(a, b, (((1,), (1,)), ((), ())), preferred_element_type=F32)


def _silu(z):
    return z / (1.0 + jnp.exp(-z))


def _sigmoid(z):
    return 1.0 / (1.0 + jnp.exp(-z))


def _select_by_index(idx, values):
    out = jnp.float32(values[0])
    for i in range(1, len(values)):
        out = jnp.where(idx == i, jnp.float32(values[i]), out)
    return out


def _ada_kernel(c_ref, w_ref, b_ref, o_ref):
    c = c_ref[...]
    o_ref[...] = jnp.dot(_silu(c), w_ref[...], preferred_element_type=F32,
                         precision=lax.Precision.HIGHEST) + b_ref[...]


def _ada(c_pad, w_ada, b_ada):
    rows, d = c_pad.shape
    n = w_ada.shape[1]
    tn = ADA_TN
    return pl.pallas_call(
        _ada_kernel,
        grid=(n // tn,),
        in_specs=[pl.BlockSpec((rows, d), lambda j: (0, 0)),
                  pl.BlockSpec((d, tn), lambda j: (0, j)),
                  pl.BlockSpec((1, tn), lambda j: (0, j))],
        out_specs=pl.BlockSpec((rows, tn), lambda j: (0, j)),
        out_shape=jax.ShapeDtypeStruct((rows, n), F32),
        name="ada",
    )(c_pad, w_ada, b_ada)


def _proj_kernel(x_ref, mod_ref, g_ref, w_ref, cs_ref, o_ref, hn_a, hn_b):
    i = pl.program_id(0)
    j = pl.program_id(1)
    tm = x_ref.shape[0]
    shift = mod_ref[0, 0:1, :]
    mult = g_ref[...] * (1.0 + mod_ref[0, 1:2, :])

    def normalize(dst_ref, r0, rows):
        xx = x_ref[pl.ds(r0, rows), :]
        ms = jnp.mean(xx * xx, axis=-1, keepdims=True)
        dst_ref[pl.ds(r0, rows), :] = (xx * lax.rsqrt(ms + EPS) * mult + shift).astype(BF16)

    @pl.when(jnp.logical_and(i == 0, j == 0))
    def _():
        def body(c, carry):
            normalize(hn_a, pl.multiple_of(c * PROJ_NORM_CHUNK, PROJ_NORM_CHUNK), PROJ_NORM_CHUNK)
            return carry

        lax.fori_loop(0, tm // PROJ_NORM_CHUNK, body, 0)

    r0 = pl.multiple_of(jnp.clip((j - 1) * PROJ_NORM_ROWS, 0, tm - PROJ_NORM_ROWS), PROJ_NORM_ROWS)

    def step(cur_ref, nxt_ref):
        normalize(nxt_ref, r0, PROJ_NORM_ROWS)
        acc = jnp.dot(cur_ref[...], w_ref[...], preferred_element_type=F32)
        o_ref[...] = (acc * cs_ref[...]).astype(BF16)

    @pl.when(i % 2 == 0)
    def _():
        step(hn_a, hn_b)

    @pl.when(i % 2 == 1)
    def _():
        step(hn_b, hn_a)


def _proj(x2d, mod, g_norm, w_in_bf16, col_scale, seq):
    t, d = x2d.shape
    n = w_in_bf16.shape[1]
    tm, tn = PROJ_TM, PROJ_TN
    n_i, n_j = t // tm, n // tn
    tiles_per_batch = seq // tm
    assert (n_j - 1) * PROJ_NORM_ROWS >= tm and tm % PROJ_NORM_ROWS == 0

    def next_tile(i, j):
        return jnp.where(jnp.logical_and(i == 0, j == 0), 0, jnp.minimum(i + 1, n_i - 1))

    return pl.pallas_call(
        _proj_kernel,
        grid=(n_i, n_j),
        in_specs=[pl.BlockSpec((tm, d), lambda i, j: (next_tile(i, j), 0)),
                  pl.BlockSpec((1, 3, d), lambda i, j: (next_tile(i, j) // tiles_per_batch, 0, 0)),
                  pl.BlockSpec((1, d), lambda i, j: (0, 0)),
                  pl.BlockSpec((d, tn), lambda i, j: (0, j)),
                  pl.BlockSpec((1, tn), lambda i, j: (0, j))],
        out_specs=pl.BlockSpec((tm, tn), lambda i, j: (i, j)),
        out_shape=jax.ShapeDtypeStruct((t, n), BF16),
        scratch_shapes=[pltpu.VMEM((tm, d), BF16), pltpu.VMEM((tm, d), BF16)],
        compiler_params=pltpu.CompilerParams(
            dimension_semantics=("arbitrary", "arbitrary"),
            vmem_limit_bytes=VMEM_LIMIT_BYTES),
        name="proj",
    )(x2d, mod, g_norm, w_in_bf16, col_scale)


def _gather_residues(src_ref, stage, dst_ref, r, n_u, dst_pitch, dst_off):
    pitch = DIL_WIDE_PITCH if r == DIL_WIDE_R else r
    for c0 in range(0, n_u * r, DIL_SPAN):
        x = src_ref[c0:c0 + DIL_SPAN, :].astype(F32)
        if pitch == r:
            stage[c0:c0 + DIL_SPAN, :] = x
        else:
            for k in range(DIL_SPAN // r):
                row = (c0 // r + k) * pitch
                stage[row:row + r, :] = x[k * r:(k + 1) * r, :]
    for p in range(r):
        for c0 in range(0, n_u, DIL_SPAN):
            row = p * dst_pitch + dst_off + c0
            dst_ref[row:row + DIL_SPAN, :] = stage[
                pl.ds(p + pitch * c0, DIL_SPAN, stride=pitch), :].astype(BF16)


def _dil_kernel(*refs):
    ins, y_ref = refs[:10], refs[10]
    qd, stage, bm_ref = refs[11:14]
    kd, vd = refs[14:17], refs[17:20]
    o_scr, lse_scr = refs[20:23], refs[23:26]
    z_ref = ins[9]
    h = pl.program_id(1)
    first_tile = pl.program_id(2) == 0

    iq = lax.broadcasted_iota(jnp.int32, (DIL_SPAN, 2 * DIL_SPAN), 0)
    jk = lax.broadcasted_iota(jnp.int32, (DIL_SPAN, 2 * DIL_SPAN), 1)
    dist = (iq - jk + DIL_SPAN).astype(F32)
    valid = jnp.logical_and(jk >= iq, jk <= iq + DIL_SPAN)
    valid_first = jnp.logical_and(valid, jk >= DIL_SPAN)

    slopes = _alibi_slopes(DIL_N_HEADS)
    for g, (_, r) in enumerate(DIL_GROUPS):
        q_ref, kc_ref, vc_ref = ins[3 * g:3 * g + 3]
        kd_g, vd_g = kd[g], vd[g]
        nb = DIL_UNITS // r
        kpitch = (nb + 1) * DIL_SPAN
        coef = _select_by_index(
            h, [slopes[g * DIL_HEADS_PER_GROUP + i] * r * LOG2E for i in range(DIL_HEADS_PER_GROUP)])
        bias = -coef * dist
        bm_ref[0] = jnp.where(valid, bias, NEG_INF)
        bm_ref[1] = jnp.where(valid_first, bias, NEG_INF)

        @pl.when(first_tile)
        def _(r=r, kpitch=kpitch, kd_g=kd_g, vd_g=vd_g):
            for p in range(r):
                kd_g[p * kpitch:p * kpitch + DIL_SPAN, :] = jnp.zeros((DIL_SPAN, LANES), BF16)
                vd_g[p * kpitch:p * kpitch + DIL_SPAN, :] = jnp.zeros((DIL_SPAN, LANES), BF16)

        if r == 1:
            q_src = q_ref
            kd_g[DIL_SPAN:DIL_SPAN + DIL_TT, :] = kc_ref[...]
            vd_g[DIL_SPAN:DIL_SPAN + DIL_TT, :] = vc_ref[...]
        else:
            q_src = qd
            _gather_residues(q_ref, stage, qd, r, nb * DIL_SPAN, nb * DIL_SPAN, 0)
            _gather_residues(kc_ref, stage, kd_g, r, nb * DIL_SPAN, kpitch, DIL_SPAN)
            _gather_residues(vc_ref, stage, vd_g, r, nb * DIL_SPAN, kpitch, DIL_SPAN)

        for p in range(r):
            for ub in range(nb):
                rq = (p * nb + ub) * DIL_SPAN
                rk = p * kpitch + ub * DIL_SPAN
                bm = bm_ref[jnp.where(first_tile, 1, 0)] if ub == 0 else bm_ref[0]
                s = _dot_nt(q_src[rq:rq + DIL_SPAN, :], kd_g[rk:rk + 2 * DIL_SPAN, :]) + bm
                m = jnp.max(s, axis=-1, keepdims=True)
                e = jnp.exp2(s - m)
                l = jnp.sum(e, axis=-1, keepdims=True)
                o = jnp.dot(e.astype(BF16), vd_g[rk:rk + 2 * DIL_SPAN, :],
                            preferred_element_type=F32) * (1.0 / l)
                lse2 = jnp.broadcast_to(m + jnp.log(l) * LOG2E, (DIL_SPAN, LANES))
                if r == DIL_WIDE_R:
                    rows = pl.ds(p, DIL_SPAN, stride=DIL_WIDE_PITCH)
                elif r > 1:
                    rows = pl.ds(ub * (DIL_SPAN * r) + p, DIL_SPAN, stride=r)
                else:
                    rows = pl.ds(ub * DIL_SPAN, DIL_SPAN)
                o_scr[g][rows, :] = o
                lse_scr[g][rows, :] = lse2

        for p in range(r):
            last = p * kpitch + nb * DIL_SPAN
            kd_g[p * kpitch:p * kpitch + DIL_SPAN, :] = kd_g[last:last + DIL_SPAN, :]
            vd_g[p * kpitch:p * kpitch + DIL_SPAN, :] = vd_g[last:last + DIL_SPAN, :]

    def merge(i, carry):
        rows = pl.ds(pl.multiple_of(i * DIL_MERGE_CHUNK, DIL_MERGE_CHUNK), DIL_MERGE_CHUNK)

        def wide(ref):
            per = DIL_MERGE_CHUNK // DIL_WIDE_R
            return jnp.concatenate(
                [ref[pl.ds(pl.multiple_of((i * per + k) * DIL_WIDE_PITCH, SUBLANES), DIL_WIDE_R), :]
                 for k in range(per)], axis=0)

        l0, l1, l2 = lse_scr[0][rows, :], lse_scr[1][rows, :], wide(lse_scr[2])
        mx = jnp.maximum(jnp.maximum(l0, l1), l2)
        e0, e1, e2 = jnp.exp2(l0 - mx), jnp.exp2(l1 - mx), jnp.exp2(l2 - mx)
        om = (e0 * o_scr[0][rows, :] + e1 * o_scr[1][rows, :] + e2 * wide(o_scr[2])) / (e0 + e1 + e2)
        y_ref[rows, :] = (om * _silu(z_ref[rows, :].astype(F32))).astype(BF16)
        return carry

    lax.fori_loop(0, DIL_TT // DIL_MERGE_CHUNK, merge, 0, unroll=2)


def _dil(proj2d, batch, seq):
    tt = DIL_TT
    tiles = seq // tt

    def cur(col):
        return pl.BlockSpec((tt, LANES), lambda b, h, t: (b * tiles + t, col + h))

    in_specs = []
    for g in range(len(DIL_GROUPS)):
        in_specs += [cur((off + g * DIL_WIDTH) // LANES) for off in (OFF_QA, OFF_KA, OFF_VA)]
    in_specs.append(cur(OFF_ZA // LANES))
    kv_scratch = [pltpu.VMEM((tt + DIL_SPAN * r, LANES), BF16) for _, r in DIL_GROUPS]
    return pl.pallas_call(
        _dil_kernel,
        grid=(batch, DIL_HEADS_PER_GROUP, tiles),
        in_specs=in_specs,
        out_specs=pl.BlockSpec((tt, LANES), lambda b, h, t: (b * tiles + t, h)),
        out_shape=jax.ShapeDtypeStruct((batch * seq, DIL_WIDTH), BF16),
        scratch_shapes=(
            [pltpu.VMEM((tt, LANES), BF16), pltpu.VMEM((DIL_SPAN * DIL_WIDE_PITCH, LANES), F32),
             pltpu.VMEM((2, DIL_SPAN, 2 * DIL_SPAN), F32)]
            + kv_scratch + kv_scratch
            + [pltpu.VMEM((tt, LANES), F32)] * 2 + [pltpu.VMEM((DIL_SPAN * DIL_WIDE_PITCH, LANES), F32)]
            + [pltpu.VMEM((tt, LANES), F32)] * 2 + [pltpu.VMEM((DIL_SPAN * DIL_WIDE_PITCH, LANES), F32)]),
        compiler_params=pltpu.CompilerParams(
            dimension_semantics=("parallel", "parallel", "arbitrary"),
            vmem_limit_bytes=VMEM_LIMIT_BYTES),
        name="dil",
    )(*([proj2d] * 10))


def _diff_kernel(q_ref, k_ref, v_ref, lamv_ref, gsub_ref, o_ref,
                 vt_ref, st_ref, pt_ref, cb_ref, m_ref, a_ref, acc_ref, *, slopes2):
    h = pl.program_id(1)
    qi = pl.program_id(2)
    t = q_ref.shape[0]
    nk = vt_ref.shape[0]
    slope2 = _select_by_index(h, slopes2)

    @pl.when(qi == 0)
    def _():
        for j in range(nk):
            vt_ref[j, 0:DIFF_HEAD_DIM, :] = v_ref[j * t:(j + 1) * t, :].astype(F32).T.astype(BF16)
            vt_ref[j, DIFF_HEAD_DIM:, :] = jnp.ones((DIFF_ONES_ROWS, t), BF16)

    cb_ref[...] = slope2 * lax.broadcasted_iota(jnp.int32, (t, LANES), 0).astype(F32)
    m_ref[...] = jnp.full(m_ref.shape, NEG_INF, F32)
    acc_ref[...] = jnp.zeros(acc_ref.shape, F32)

    def block(kj):
        k0 = pl.multiple_of(kj * t, t)
        cblk = slope2 * ((kj - qi) * t).astype(F32)
        for c in range(2):
            sl = slice(c * HEAD_DIM, (c + 1) * HEAD_DIM)
            st_ref[c] = _dot_nt(k_ref[pl.ds(k0, t), sl], q_ref[:, sl])
        for c in range(2):
            for cg in range(t // LANES):
                cols = slice(cg * LANES, (cg + 1) * LANES)
                x = st_ref[c, :, cols] + cb_ref[...]
                m_old = m_ref[c, :, cols]
                m_new = jnp.maximum(m_old, jnp.max(x, axis=0, keepdims=True) + cblk)
                m_ref[c, :, cols] = m_new
                a_ref[c, :, cols] = jnp.exp2(m_old - m_new)
                pt_ref[c, :, cols] = jnp.exp2((x - (m_new - cblk)).astype(BF16))
        vt = vt_ref[kj]
        for c in range(2):
            acc_ref[c] = acc_ref[c] * a_ref[c] + jnp.dot(vt, pt_ref[c], preferred_element_type=F32)

    def diagonal_block():
        k0 = pl.multiple_of(qi * t, t)
        w = DIFF_DIAG_W
        for c in range(2):
            sl = slice(c * HEAD_DIM, (c + 1) * HEAD_DIM)
            for qh in range(t // w):
                kend = (qh + 1) * w
                qs = slice(qh * w, kend)
                st_ref[c, 0:kend, qs] = _dot_nt(k_ref[pl.ds(k0, kend), sl], q_ref[qs, sl])
        for c in range(2):
            for cg in range(t // LANES):
                cols = slice(cg * LANES, (cg + 1) * LANES)
                kend = (cg * LANES // w + 1) * w
                key = lax.broadcasted_iota(jnp.int32, (kend, LANES), 0)
                qry = lax.broadcasted_iota(jnp.int32, (kend, LANES), 1) + cg * LANES
                x = jnp.where(key <= qry, st_ref[c, 0:kend, cols] + cb_ref[0:kend, :], NEG_INF)
                m_old = m_ref[c, :, cols]
                m_new = jnp.maximum(m_old, jnp.max(x, axis=0, keepdims=True))
                m_ref[c, :, cols] = m_new
                a_ref[c, :, cols] = jnp.exp2(m_old - m_new)
                pt_ref[c, 0:kend, cols] = jnp.exp2((x - m_new).astype(BF16))
        vt = vt_ref[qi]
        for c in range(2):
            for qh in range(t // w):
                kend = (qh + 1) * w
                qs = slice(qh * w, kend)
                acc_ref[c, :, qs] = (acc_ref[c, :, qs] * a_ref[c, :, qs]
                                     + jnp.dot(vt[:, 0:kend], pt_ref[c, 0:kend, qs],
                                               preferred_element_type=F32))

    def body(kj, carry):
        block(kj)
        return carry

    lax.fori_loop(0, qi, body, 0)
    diagonal_block()

    lamv = lamv_ref[...]
    lam = (jnp.exp(jnp.sum(lamv[0:1] * lamv[1:2], axis=-1, keepdims=True))
           - jnp.exp(jnp.sum(lamv[2:3] * lamv[3:4], axis=-1, keepdims=True)) + LAMBDA_INIT)
    d = DIFF_HEAD_DIM
    ot = (acc_ref[0, 0:d, :] * (1.0 / acc_ref[0, d:d + 1, :])
          - lam * (acc_ref[1, 0:d, :] * (1.0 / acc_ref[1, d:d + 1, :])))
    ms = jnp.mean(ot * ot, axis=0, keepdims=True)
    ot = ot * lax.rsqrt(ms + EPS)
    o_ref[...] = (ot.T * gsub_ref[...] * (1.0 - LAMBDA_INIT)).astype(o_ref.dtype)


def _diff(proj2d, lamv, g_subln, batch, seq):
    t = DIFF_T
    nq = seq // t
    wq, wk, wv = OFF_QB // DIFF_HEAD_DIM, OFF_KB // DIFF_HEAD_DIM, OFF_VB // DIFF_HEAD_DIM
    slopes2 = tuple(s * LOG2E for s in _alibi_slopes(DIFF_N_HEADS))
    vrows = DIFF_HEAD_DIM + DIFF_ONES_ROWS
    return pl.pallas_call(
        functools.partial(_diff_kernel, slopes2=slopes2),
        grid=(batch, DIFF_N_HEADS, nq),
        in_specs=[
            pl.BlockSpec((t, DIFF_HEAD_DIM), lambda b, h, q: (b * nq + q, wq + h)),
            pl.BlockSpec((seq, DIFF_HEAD_DIM), lambda b, h, q: (b, wk + h)),
            pl.BlockSpec((seq, DIFF_HEAD_DIM), lambda b, h, q: (b, wv + h)),
            pl.BlockSpec((4, HEAD_DIM), lambda b, h, q: (0, 0)),
            pl.BlockSpec((1, DIFF_HEAD_DIM), lambda b, h, q: (0, 0)),
        ],
        out_specs=pl.BlockSpec((t, DIFF_HEAD_DIM), lambda b, h, q: (b * nq + q, h)),
        out_shape=jax.ShapeDtypeStruct((batch * seq, DIFF_WIDTH), BF16),
        scratch_shapes=[
            pltpu.VMEM((nq, vrows, t), BF16),
            pltpu.VMEM((2, t, t), F32),
            pltpu.VMEM((2, t, t), BF16),
            pltpu.VMEM((t, LANES), F32),
            pltpu.VMEM((2, 1, t), F32),
            pltpu.VMEM((2, 1, t), F32),
            pltpu.VMEM((2, vrows, t), F32),
        ],
        compiler_params=pltpu.CompilerParams(
            dimension_semantics=("parallel", "parallel", "arbitrary"),
            vmem_limit_bytes=VMEM_LIMIT_BYTES),
        name="diff",
    )(proj2d, proj2d, proj2d, lamv, g_subln)


def _out_kernel(ya_ref, yb_ref, zb0_ref, zb1_ref, zb2_ref, gla_ref, glb_ref, x_ref, mod_ref,
                gf_ref, wd_ref, wf_ref, wo_ref, out_ref, ybz_ref):
    zb_refs = (zb0_ref, zb1_ref, zb2_ref)
    wz = zb0_ref.shape[1]
    for i in range(3):
        sl = slice(i * wz, (i + 1) * wz)
        ybz_ref[:, sl] = (yb_ref[:, sl].astype(F32) * _silu(zb_refs[i][...].astype(F32))).astype(BF16)
    pa = jnp.dot(ya_ref[...], wd_ref[...], preferred_element_type=F32)
    pb = jnp.dot(ybz_ref[...], wf_ref[...], preferred_element_type=F32)
    merged = (_sigmoid(gla_ref[...].astype(F32)) * pa + _sigmoid(glb_ref[...].astype(F32)) * pb)
    delta = jnp.dot(merged.astype(BF16), wo_ref[...], preferred_element_type=F32)
    y = x_ref[...] + mod_ref[0, 2:3, :] * delta
    ms = jnp.mean(y * y, axis=-1, keepdims=True)
    out_ref[...] = y * lax.rsqrt(ms + EPS) * gf_ref[...]


def _const_spec(shape):
    return pl.BlockSpec(shape, lambda i: (0,) * len(shape), pipeline_mode=pl.Buffered(1))


def _out(ya, yb, proj2d, x2d, mod, g_final, wd, wf, wo, seq):
    t, d = x2d.shape
    tm = OUT_TM
    tiles_per_batch = seq // tm
    wz = OUT_ZB_W
    assert OFF_ZB % wz == 0 and DIFF_WIDTH == 3 * wz and seq % tm == 0
    zb_blk = OFF_ZB // wz
    row = lambda w: pl.BlockSpec((tm, w), lambda i: (i, 0))
    in_specs = (
        [row(DIL_WIDTH), row(DIFF_WIDTH)]
        + [pl.BlockSpec((tm, wz), functools.partial(lambda i, k: (i, zb_blk + k), k=k))
           for k in range(3)]
        + [pl.BlockSpec((tm, d), lambda i: (i, OFF_GLA // D_MODEL)),
           pl.BlockSpec((tm, d), lambda i: (i, OFF_GLB // D_MODEL)),
           row(d),
           pl.BlockSpec((1, 3, d), lambda i: (i // tiles_per_batch, 0, 0)),
           _const_spec((1, d)),
           _const_spec(wd.shape), _const_spec(wf.shape), _const_spec(wo.shape)])
    return pl.pallas_call(
        _out_kernel,
        grid=(t // tm,),
        in_specs=in_specs,
        out_specs=row(d),
        out_shape=jax.ShapeDtypeStruct((t, d), F32),
        scratch_shapes=[pltpu.VMEM((tm, DIFF_WIDTH), BF16)],
        compiler_params=pltpu.CompilerParams(
            dimension_semantics=("parallel",),
            vmem_limit_bytes=VMEM_LIMIT_BYTES),
        name="out",
    )(ya, yb, proj2d, proj2d, proj2d, proj2d, proj2d, x2d, mod, g_final, wd, wf, wo)


def _q_column_scale():
    cs = np.ones((1, N_IN), np.float32)
    cs[:, OFF_QA:OFF_QA + DIL_QKV] = Q_PRESCALE
    cs[:, OFF_QB:OFF_QB + DIFF_WIDTH] = Q_PRESCALE
    return jnp.asarray(cs)


def kernel(x, c, w_ada, b_ada, g_norm, w_in, w_o_dil, w_o_diff, w_out,
           lambda_q1, lambda_k1, lambda_q2, lambda_k2, g_subln, g_final):
    batch, seq, d = x.shape
    assert w_ada.shape[0] == 1 and d == D_MODEL and w_in.shape[2] == N_IN
    assert seq % DIL_TT == 0 and seq % PROJ_TM == 0 and seq % DIFF_T == 0
    rows = -(-batch // SUBLANES) * SUBLANES
    c_pad = jnp.pad(c, ((0, rows - batch), (0, 0)))
    x2d = x.reshape(batch * seq, d)
    ada = _ada(c_pad, w_ada[0], b_ada[0].reshape(1, 3 * d))[:batch]
    mod = ada.reshape(batch, 3, d)
    proj2d = _proj(x2d, mod, g_norm[0].reshape(1, d), w_in[0].astype(BF16), _q_column_scale(), seq)
    ya = _dil(proj2d, batch, seq)
    lamv = jnp.stack([lambda_q1[0], lambda_k1[0], lambda_q2[0], lambda_k2[0]])
    yb = _diff(proj2d, lamv, g_subln[0].reshape(1, DIFF_HEAD_DIM), batch, seq)
    out = _out(ya, yb, proj2d, x2d, mod, g_final.reshape(1, d), w_o_dil[0].astype(BF16),
               w_o_diff[0].astype(BF16), w_out[0].astype(BF16), seq)
    return out.reshape(batch, seq, d)
```

```python
import functools
import math

import numpy as np
import jax
import jax.numpy as jnp
from jax import lax
from jax.experimental import pallas as pl
from jax.experimental.pallas import tpu as pltpu

F32 = jnp.float32
BF16 = jnp.bfloat16

D_MODEL = 2048
HEAD_DIM = 128
LANES = 128
SUBLANES = 8
DIL_GROUPS = ((128, 1), (512, 4), (2048, 16))
DIL_HEADS_PER_GROUP = 8
DIL_N_HEADS = DIL_HEADS_PER_GROUP * len(DIL_GROUPS)
DIL_QKV = DIL_N_HEADS * HEAD_DIM
DIL_WIDTH = DIL_HEADS_PER_GROUP * HEAD_DIM
DIL_SPAN = 128
DIFF_N_HEADS = 6
DIFF_HEAD_DIM = 2 * HEAD_DIM
DIFF_WIDTH = DIFF_N_HEADS * DIFF_HEAD_DIM
EPS = 1e-6
NEG_INF = -1e30
IN_SIZES = (DIL_QKV, DIL_QKV, DIL_QKV, DIL_WIDTH,
            DIFF_WIDTH, DIFF_WIDTH, DIFF_WIDTH, DIFF_WIDTH, D_MODEL, D_MODEL)
N_IN = sum(IN_SIZES)
(OFF_QA, OFF_KA, OFF_VA, OFF_ZA, OFF_QB, OFF_KB, OFF_VB, OFF_ZB, OFF_GLA, OFF_GLB) = (
    int(v) for v in np.concatenate([[0], np.cumsum(IN_SIZES)[:-1]]))
LOG2E = 1.4426950408889634
Q_PRESCALE = HEAD_DIM ** -0.5 * LOG2E
LAMBDA_INIT = 0.8 - 0.6 * math.exp(-0.3 * 0)

VMEM_LIMIT_BYTES = 56 * 1024 * 1024

ADA_TN = 512
PROJ_TM = 1024
PROJ_TN = 2048
PROJ_NORM_CHUNK = 32
PROJ_NORM_ROWS = 128
DIL_TT = DIL_SPAN * DIL_GROUPS[-1][1]
DIL_UNITS = DIL_TT // DIL_SPAN
DIL_WIDE_R = DIL_GROUPS[-1][1]
DIL_WIDE_PITCH = 24
DIL_MERGE_CHUNK = 64
DIFF_T = 1024
DIFF_SUB = 128
DIFF_DIAG_W = 256
DIFF_ONES_ROWS = 16
OUT_TM = 256
OUT_ZB_W = 512


def _alibi_slopes(n):
    return [float(2.0 ** (-8.0 * (i + 1) / n)) for i in range(n)]


def _dot_nt(a, b):
    return lax.dot_general(a, b, (((1,), (1,)), ((), ())), preferred_element_type=F32)


def _silu(z):
    return z / (1.0 + jnp.exp(-z))


def _sigmoid(z):
    return 1.0 / (1.0 + jnp.exp(-z))


def _select_by_index(idx, values):
    out = jnp.float32(values[0])
    for i in range(1, len(values)):
        out = jnp.where(idx == i, jnp.float32(values[i]), out)
    return out


def _ada_kernel(c_ref, w_ref, b_ref, o_ref):
    c = c_ref[...]
    o_ref[...] = jnp.dot(_silu(c), w_ref[...], preferred_element_type=F32,
                         precision=lax.Precision.HIGHEST) + b_ref[...]


def _ada(c_pad, w_ada, b_ada):
    rows, d = c_pad.shape
    n = w_ada.shape[1]
    tn = ADA_TN
    return pl.pallas_call(
        _ada_kernel,
        grid=(n // tn,),
        in_specs=[pl.BlockSpec((rows, d), lambda j: (0, 0)),
                  pl.BlockSpec((d, tn), lambda j: (0, j)),
                  pl.BlockSpec((1, tn), lambda j: (0, j))],
        out_specs=pl.BlockSpec((rows, tn), lambda j: (0, j)),
        out_shape=jax.ShapeDtypeStruct((rows, n), F32),
        name="ada",
    )(c_pad, w_ada, b_ada)


def _proj_kernel(x_ref, mod_ref, g_ref, w_ref, cs_ref, o_ref, hn_a, hn_b):
    i = pl.program_id(0)
    j = pl.program_id(1)
    tm = x_ref.shape[0]
    shift = mod_ref[0, 0:1, :]
    mult = g_ref[...] * (1.0 + mod_ref[0, 1:2, :])

    def normalize(dst_ref, r0, rows):
        xx = x_ref[pl.ds(r0, rows), :]
        ms = jnp.mean(xx * xx, axis=-1, keepdims=True)
        dst_ref[pl.ds(r0, rows), :] = (xx * lax.rsqrt(ms + EPS) * mult + shift).astype(BF16)

    @pl.when(jnp.logical_and(i == 0, j == 0))
    def _():
        def body(c, carry):
            normalize(hn_a, pl.multiple_of(c * PROJ_NORM_CHUNK, PROJ_NORM_CHUNK), PROJ_NORM_CHUNK)
            return carry

        lax.fori_loop(0, tm // PROJ_NORM_CHUNK, body, 0)

    r0 = pl.multiple_of(jnp.clip((j - 1) * PROJ_NORM_ROWS, 0, tm - PROJ_NORM_ROWS), PROJ_NORM_ROWS)

    def step(cur_ref, nxt_ref):
        normalize(nxt_ref, r0, PROJ_NORM_ROWS)
        acc = jnp.dot(cur_ref[...], w_ref[...], preferred_element_type=F32)
        o_ref[...] = (acc * cs_ref[...]).astype(BF16)

    @pl.when(i % 2 == 0)
    def _():
        step(hn_a, hn_b)

    @pl.when(i % 2 == 1)
    def _():
        step(hn_b, hn_a)


def _proj(x2d, mod, g_norm, w_in_bf16, col_scale, seq):
    t, d = x2d.shape
    n = w_in_bf16.shape[1]
    tm, tn = PROJ_TM, PROJ_TN
    n_i, n_j = t // tm, n // tn
    tiles_per_batch = seq // tm
    assert (n_j - 1) * PROJ_NORM_ROWS >= tm and tm % PROJ_NORM_ROWS == 0

    def next_tile(i, j):
        return jnp.where(jnp.logical_and(i == 0, j == 0), 0, jnp.minimum(i + 1, n_i - 1))

    return pl.pallas_call(
        _proj_kernel,
        grid=(n_i, n_j),
        in_specs=[pl.BlockSpec((tm, d), lambda i, j: (next_tile(i, j), 0)),
                  pl.BlockSpec((1, 3, d), lambda i, j: (next_tile(i, j) // tiles_per_batch, 0, 0)),
                  pl.BlockSpec((1, d), lambda i, j: (0, 0)),
                  pl.BlockSpec((d, tn), lambda i, j: (0, j)),
                  pl.BlockSpec((1, tn), lambda i, j: (0, j))],
        out_specs=pl.BlockSpec((tm, tn), lambda i, j: (i, j)),
        out_shape=jax.ShapeDtypeStruct((t, n), BF16),
        scratch_shapes=[pltpu.VMEM((tm, d), BF16), pltpu.VMEM((tm, d), BF16)],
        compiler_params=pltpu.CompilerParams(
            dimension_semantics=("arbitrary", "arbitrary"),
            vmem_limit_bytes=VMEM_LIMIT_BYTES),
        name="proj",
    )(x2d, mod, g_norm, w_in_bf16, col_scale)


def _gather_residues(src_ref, stage, dst_ref, r, n_u, dst_pitch, dst_off):
    pitch = DIL_WIDE_PITCH if r == DIL_WIDE_R else r
    for c0 in range(0, n_u * r, DIL_SPAN):
        x = src_ref[c0:c0 + DIL_SPAN, :].astype(F32)
        if pitch == r:
            stage[c0:c0 + DIL_SPAN, :] = x
        else:
            for k in range(DIL_SPAN // r):
                row = (c0 // r + k) * pitch
                stage[row:row + r, :] = x[k * r:(k + 1) * r, :]
    for p in range(r):
        for c0 in range(0, n_u, DIL_SPAN):
            row = p * dst_pitch + dst_off + c0
            dst_ref[row:row + DIL_SPAN, :] = stage[
                pl.ds(p + pitch * c0, DIL_SPAN, stride=pitch), :].astype(BF16)


def _dil_kernel(*refs):
    ins, y_ref = refs[:10], refs[10]
    qd, stage, bm_ref = refs[11:14]
    kd, vd = refs[14:17], refs[17:20]
    o_scr, lse_scr = refs[20:23], refs[23:26]
    z_ref = ins[9]
    h = pl.program_id(1)
    first_tile = pl.program_id(2) == 0

    iq = lax.broadcasted_iota(jnp.int32, (DIL_SPAN, 2 * DIL_SPAN), 0)
    jk = lax.broadcasted_iota(jnp.int32, (DIL_SPAN, 2 * DIL_SPAN), 1)
    dist = (iq - jk + DIL_SPAN).astype(F32)
    valid = jnp.logical_and(jk >= iq, jk <= iq + DIL_SPAN)
    valid_first = jnp.logical_and(valid, jk >= DIL_SPAN)

    slopes = _alibi_slopes(DIL_N_HEADS)
    for g, (_, r) in enumerate(DIL_GROUPS):
        q_ref, kc_ref, vc_ref = ins[3 * g:3 * g + 3]
        kd_g, vd_g = kd[g], vd[g]
        nb = DIL_UNITS // r
        kpitch = (nb + 1) * DIL_SPAN
        coef = _select_by_index(
            h, [slopes[g * DIL_HEADS_PER_GROUP + i] * r * LOG2E for i in range(DIL_HEADS_PER_GROUP)])
        bias = -coef * dist
        bm_ref[0] = jnp.where(valid, bias, NEG_INF)
        bm_ref[1] = jnp.where(valid_first, bias, NEG_INF)

        @pl.when(first_tile)
        def _(r=r, kpitch=kpitch, kd_g=kd_g, vd_g=vd_g):
            for p in range(r):
                kd_g[p * kpitch:p * kpitch + DIL_SPAN, :] = jnp.zeros((DIL_SPAN, LANES), BF16)
                vd_g[p * kpitch:p * kpitch + DIL_SPAN, :] = jnp.zeros((DIL_SPAN, LANES), BF16)

        if r == 1:
            q_src = q_ref
            kd_g[DIL_SPAN:DIL_SPAN + DIL_TT, :] = kc_ref[...]
            vd_g[DIL_SPAN:DIL_SPAN + DIL_TT, :] = vc_ref[...]
        else:
            q_src = qd
            _gather_residues(q_ref, stage, qd, r, nb * DIL_SPAN, nb * DIL_SPAN, 0)
            _gather_residues(kc_ref, stage, kd_g, r, nb * DIL_SPAN, kpitch, DIL_SPAN)
            _gather_residues(vc_ref, stage, vd_g, r, nb * DIL_SPAN, kpitch, DIL_SPAN)

        for p in range(r):
            for ub in range(nb):
                rq = (p * nb + ub) * DIL_SPAN
                rk = p * kpitch + ub * DIL_SPAN
                bm = bm_ref[jnp.where(first_tile, 1, 0)] if ub == 0 else bm_ref[0]
                s = _dot_nt(q_src[rq:rq + DIL_SPAN, :], kd_g[rk:rk + 2 * DIL_SPAN, :]) + bm
                m = jnp.max(s, axis=-1, keepdims=True)
                e = jnp.exp2(s - m)
                l = jnp.sum(e, axis=-1, keepdims=True)
                o = jnp.dot(e.astype(BF16), vd_g[rk:rk + 2 * DIL_SPAN, :],
                            preferred_element_type=F32) * (1.0 / l)
                lse2 = jnp.broadcast_to(m + jnp.log(l) * LOG2E, (DIL_SPAN, LANES))
                if r == DIL_WIDE_R:
                    rows = pl.ds(p, DIL_SPAN, stride=DIL_WIDE_PITCH)
                elif r > 1:
                    rows = pl.ds(ub * (DIL_SPAN * r) + p, DIL_SPAN, stride=r)
                else:
                    rows = pl.ds(ub * DIL_SPAN, DIL_SPAN)
                o_scr[g][rows, :] = o
                lse_scr[g][rows, :] = lse2

        for p in range(r):
            last = p * kpitch + nb * DIL_SPAN
            kd_g[p * kpitch:p * kpitch + DIL_SPAN, :] = kd_g[last:last + DIL_SPAN, :]
            vd_g[p * kpitch:p * kpitch + DIL_SPAN, :] = vd_g[last:last + DIL_SPAN, :]

    def merge(i, carry):
        rows = pl.ds(pl.multiple_of(i * DIL_MERGE_CHUNK, DIL_MERGE_CHUNK), DIL_MERGE_CHUNK)

        def wide(ref):
            per = DIL_MERGE_CHUNK // DIL_WIDE_R
            return jnp.concatenate(
                [ref[pl.ds(pl.multiple_of((i * per + k) * DIL_WIDE_PITCH, SUBLANES), DIL_WIDE_R), :]
                 for k in range(per)], axis=0)

        l0, l1, l2 = lse_scr[0][rows, :], lse_scr[1][rows, :], wide(lse_scr[2])
        mx = jnp.maximum(jnp.maximum(l0, l1), l2)
        e0, e1, e2 = jnp.exp2(l0 - mx), jnp.exp2(l1 - mx), jnp.exp2(l2 - mx)
        om = (e0 * o_scr[0][rows, :] + e1 * o_scr[1][rows, :] + e2 * wide(o_scr[2])) / (e0 + e1 + e2)
        y_ref[rows, :] = (om * _silu(z_ref[rows, :].astype(F32))).astype(BF16)
        return carry

    lax.fori_loop(0, DIL_TT // DIL_MERGE_CHUNK, merge, 0, unroll=2)


def _dil(proj2d, batch, seq):
    tt = DIL_TT
    tiles = seq // tt

    def cur(col):
        return pl.BlockSpec((tt, LANES), lambda b, h, t: (b * tiles + t, col + h))

    in_specs = []
    for g in range(len(DIL_GROUPS)):
        in_specs += [cur((off + g * DIL_WIDTH) // LANES) for off in (OFF_QA, OFF_KA, OFF_VA)]
    in_specs.append(cur(OFF_ZA // LANES))
    kv_scratch = [pltpu.VMEM((tt + DIL_SPAN * r, LANES), BF16) for _, r in DIL_GROUPS]
    return pl.pallas_call(
        _dil_kernel,
        grid=(batch, DIL_HEADS_PER_GROUP, tiles),
        in_specs=in_specs,
        out_specs=pl.BlockSpec((tt, LANES), lambda b, h, t: (b * tiles + t, h)),
        out_shape=jax.ShapeDtypeStruct((batch * seq, DIL_WIDTH), BF16),
        scratch_shapes=(
            [pltpu.VMEM((tt, LANES), BF16), pltpu.VMEM((DIL_SPAN * DIL_WIDE_PITCH, LANES), F32),
             pltpu.VMEM((2, DIL_SPAN, 2 * DIL_SPAN), F32)]
            + kv_scratch + kv_scratch
            + [pltpu.VMEM((tt, LANES), F32)] * 2 + [pltpu.VMEM((DIL_SPAN * DIL_WIDE_PITCH, LANES), F32)]
            + [pltpu.VMEM((tt, LANES), F32)] * 2 + [pltpu.VMEM((DIL_SPAN * DIL_WIDE_PITCH, LANES), F32)]),
        compiler_params=pltpu.CompilerParams(
            dimension_semantics=("parallel", "parallel", "arbitrary"),
            vmem_limit_bytes=VMEM_LIMIT_BYTES),
        name="dil",
    )(*([proj2d] * 10))


def _diff_kernel(q_ref, k_ref, v_ref, lamv_ref, gsub_ref, o_ref,
                 vt_ref, st_ref, pt_ref, cb_ref, m_ref, a_ref, acc_ref, *, slopes2):
    h = pl.program_id(1)
    qi = pl.program_id(2)
    t = q_ref.shape[0]
    nk = vt_ref.shape[0]
    slope2 = _select_by_index(h, slopes2)

    @pl.when(qi == 0)
    def _():
        for j in range(nk):
            vt_ref[j, 0:DIFF_HEAD_DIM, :] = v_ref[j * t:(j + 1) * t, :].astype(F32).T.astype(BF16)
            vt_ref[j, DIFF_HEAD_DIM:, :] = jnp.ones((DIFF_ONES_ROWS, t), BF16)

    cb_ref[...] = slope2 * lax.broadcasted_iota(jnp.int32, (t, LANES), 0).astype(F32)
    m_ref[...] = jnp.full(m_ref.shape, NEG_INF, F32)
    acc_ref[...] = jnp.zeros(acc_ref.shape, F32)

    def block(kj):
        k0 = pl.multiple_of(kj * t, t)
        cblk = slope2 * ((kj - qi) * t).astype(F32)
        for c in range(2):
            sl = slice(c * HEAD_DIM, (c + 1) * HEAD_DIM)
            st_ref[c] = _dot_nt(k_ref[pl.ds(k0, t), sl], q_ref[:, sl])
        for c in range(2):
            for cg in range(t // LANES):
                cols = slice(cg * LANES, (cg + 1) * LANES)
                m_old = m_ref[c, :, cols]
                m_run = m_old
                m_sub = []
                for r0 in range(0, t, DIFF_SUB):
                    rows = slice(r0, r0 + DIFF_SUB)
                    x = st_ref[c, rows, cols] + cb_ref[rows, :]
                    m_run = jnp.maximum(m_run, jnp.max(x, axis=0, keepdims=True) + cblk)
                    pt_ref[c, rows, cols] = jnp.exp2((x - (m_run - cblk)).astype(BF16))
                    m_sub.append(m_run)
                for i, r0 in enumerate(range(0, t - DIFF_SUB, DIFF_SUB)):
                    rows = slice(r0, r0 + DIFF_SUB)
                    pt_ref[c, rows, cols] = (pt_ref[c, rows, cols]
                                             * jnp.exp2(m_sub[i] - m_run).astype(BF16))
                m_ref[c, :, cols] = m_run
                a_ref[c, :, cols] = jnp.exp2(m_old - m_run)
        vt = vt_ref[kj]
        for c in range(2):
            acc_ref[c] = acc_ref[c] * a_ref[c] + jnp.dot(vt, pt_ref[c], preferred_element_type=F32)

    def diagonal_block():
        k0 = pl.multiple_of(qi * t, t)
        w = DIFF_DIAG_W
        for c in range(2):
            sl = slice(c * HEAD_DIM, (c + 1) * HEAD_DIM)
            for qh in range(t // w):
                kend = (qh + 1) * w
                qs = slice(qh * w, kend)
                st_ref[c, 0:kend, qs] = _dot_nt(k_ref[pl.ds(k0, kend), sl], q_ref[qs, sl])
        for c in range(2):
            for cg in range(t // LANES):
                cols = slice(cg * LANES, (cg + 1) * LANES)
                kend = (cg * LANES // w + 1) * w
                key = lax.broadcasted_iota(jnp.int32, (kend, LANES), 0)
                qry = lax.broadcasted_iota(jnp.int32, (kend, LANES), 1) + cg * LANES
                x = jnp.where(key <= qry, st_ref[c, 0:kend, cols] + cb_ref[0:kend, :], NEG_INF)
                m_old = m_ref[c, :, cols]
                m_new = jnp.maximum(m_old, jnp.max(x, axis=0, keepdims=True))
                m_ref[c, :, cols] = m_new
                a_ref[c, :, cols] = jnp.exp2(m_old - m_new)
                pt_ref[c, 0:kend, cols] = jnp.exp2((x - m_new).astype(BF16))
        vt = vt_ref[qi]
        for c in range(2):
            for qh in range(t // w):
                kend = (qh + 1) * w
                qs = slice(qh * w, kend)
                acc_ref[c, :, qs] = (acc_ref[c, :, qs] * a_ref[c, :, qs]
                                     + jnp.dot(vt[:, 0:kend], pt_ref[c, 0:kend, qs],
                                               preferred_element_type=F32))

    def body(kj, carry):
        block(kj)
        return carry

    lax.fori_loop(0, qi, body, 0)
    diagonal_block()

    lamv = lamv_ref[...]
    lam = (jnp.exp(jnp.sum(lamv[0:1] * lamv[1:2], axis=-1, keepdims=True))
           - jnp.exp(jnp.sum(lamv[2:3] * lamv[3:4], axis=-1, keepdims=True)) + LAMBDA_INIT)
    d = DIFF_HEAD_DIM
    ot = (acc_ref[0, 0:d, :] * (1.0 / acc_ref[0, d:d + 1, :])
          - lam * (acc_ref[1, 0:d, :] * (1.0 / acc_ref[1, d:d + 1, :])))
    ms = jnp.mean(ot * ot, axis=0, keepdims=True)
    ot = ot * lax.rsqrt(ms + EPS)
    o_ref[...] = (ot.T * gsub_ref[...] * (1.0 - LAMBDA_INIT)).astype(o_ref.dtype)


def _diff(proj2d, lamv, g_subln, batch, seq):
    t = DIFF_T
    nq = seq // t
    wq, wk, wv = OFF_QB // DIFF_HEAD_DIM, OFF_KB // DIFF_HEAD_DIM, OFF_VB // DIFF_HEAD_DIM
    slopes2 = tuple(s * LOG2E for s in _alibi_slopes(DIFF_N_HEADS))
    vrows = DIFF_HEAD_DIM + DIFF_ONES_ROWS
    return pl.pallas_call(
        functools.partial(_diff_kernel, slopes2=slopes2),
        grid=(batch, DIFF_N_HEADS, nq),
        in_specs=[
            pl.BlockSpec((t, DIFF_HEAD_DIM), lambda b, h, q: (b * nq + q, wq + h)),
            pl.BlockSpec((seq, DIFF_HEAD_DIM), lambda b, h, q: (b, wk + h)),
            pl.BlockSpec((seq, DIFF_HEAD_DIM), lambda b, h, q: (b, wv + h)),
            pl.BlockSpec((4, HEAD_DIM), lambda b, h, q: (0, 0)),
            pl.BlockSpec((1, DIFF_HEAD_DIM), lambda b, h, q: (0, 0)),
        ],
        out_specs=pl.BlockSpec((t, DIFF_HEAD_DIM), lambda b, h, q: (b * nq + q, h)),
        out_shape=jax.ShapeDtypeStruct((batch * seq, DIFF_WIDTH), BF16),
        scratch_shapes=[
            pltpu.VMEM((nq, vrows, t), BF16),
            pltpu.VMEM((2, t, t), F32),
            pltpu.VMEM((2, t, t), BF16),
            pltpu.VMEM((t, LANES), F32),
            pltpu.VMEM((2, 1, t), F32),
            pltpu.VMEM((2, 1, t), F32),
            pltpu.VMEM((2, vrows, t), F32),
        ],
        compiler_params=pltpu.CompilerParams(
            dimension_semantics=("parallel", "parallel", "arbitrary"),
            vmem_limit_bytes=VMEM_LIMIT_BYTES),
        name="diff",
    )(proj2d, proj2d, proj2d, lamv, g_subln)


def _out_kernel(ya_ref, yb_ref, zb0_ref, zb1_ref, zb2_ref, gla_ref, glb_ref, x_ref, mod_ref,
                gf_ref, wd_ref, wf_ref, wo_ref, out_ref, ybz_ref):
    zb_refs = (zb0_ref, zb1_ref, zb2_ref)
    wz = zb0_ref.shape[1]
    for i in range(3):
        sl = slice(i * wz, (i + 1) * wz)
        ybz_ref[:, sl] = (yb_ref[:, sl].astype(F32) * _silu(zb_refs[i][...].astype(F32))).astype(BF16)
    pa = jnp.dot(ya_ref[...], wd_ref[...], preferred_element_type=F32)
    pb = jnp.dot(ybz_ref[...], wf_ref[...], preferred_element_type=F32)
    merged = (_sigmoid(gla_ref[...].astype(F32)) * pa + _sigmoid(glb_ref[...].astype(F32)) * pb)
    delta = jnp.dot(merged.astype(BF16), wo_ref[...], preferred_element_type=F32)
    y = x_ref[...] + mod_ref[0, 2:3, :] * delta
    ms = jnp.mean(y * y, axis=-1, keepdims=True)
    out_ref[...] = y * lax.rsqrt(ms + EPS) * gf_ref[...]


def _const_spec(shape):
    return pl.BlockSpec(shape, lambda i: (0,) * len(shape), pipeline_mode=pl.Buffered(1))


def _out(ya, yb, proj2d, x2d, mod, g_final, wd, wf, wo, seq):
    t, d = x2d.shape
    tm = OUT_TM
    tiles_per_batch = seq // tm
    wz = OUT_ZB_W
    assert OFF_ZB % wz == 0 and DIFF_WIDTH == 3 * wz and seq % tm == 0
    zb_blk = OFF_ZB // wz
    row = lambda w: pl.BlockSpec((tm, w), lambda i: (i, 0))
    in_specs = (
        [row(DIL_WIDTH), row(DIFF_WIDTH)]
        + [pl.BlockSpec((tm, wz), functools.partial(lambda i, k: (i, zb_blk + k), k=k))
           for k in range(3)]
        + [pl.BlockSpec((tm, d), lambda i: (i, OFF_GLA // D_MODEL)),
           pl.BlockSpec((tm, d), lambda i: (i, OFF_GLB // D_MODEL)),
           row(d),
           pl.BlockSpec((1, 3, d), lambda i: (i // tiles_per_batch, 0, 0)),
           _const_spec((1, d)),
           _const_spec(wd.shape), _const_spec(wf.shape), _const_spec(wo.shape)])
    return pl.pallas_call(
        _out_kernel,
        grid=(t // tm,),
        in_specs=in_specs,
        out_specs=row(d),
        out_shape=jax.ShapeDtypeStruct((t, d), F32),
        scratch_shapes=[pltpu.VMEM((tm, DIFF_WIDTH), BF16)],
        compiler_params=pltpu.CompilerParams(
            dimension_semantics=("parallel",),
            vmem_limit_bytes=VMEM_LIMIT_BYTES),
        name="out",
    )(ya, yb, proj2d, proj2d, proj2d, proj2d, proj2d, x2d, mod, g_final, wd, wf, wo)


def _q_column_scale():
    cs = np.ones((1, N_IN), np.float32)
    cs[:, OFF_QA:OFF_QA + DIL_QKV] = Q_PRESCALE
    cs[:, OFF_QB:OFF_QB + DIFF_WIDTH] = Q_PRESCALE
    return jnp.asarray(cs)


def kernel(x, c, w_ada, b_ada, g_norm, w_in, w_o_dil, w_o_diff, w_out,
           lambda_q1, lambda_k1, lambda_q2, lambda_k2, g_subln, g_final):
    batch, seq, d = x.shape
    assert w_ada.shape[0] == 1 and d == D_MODEL and w_in.shape[2] == N_IN
    assert seq % DIL_TT == 0 and seq % PROJ_TM == 0 and seq % DIFF_T == 0
    rows = -(-batch // SUBLANES) * SUBLANES
    c_pad = jnp.pad(c, ((0, rows - batch), (0, 0)))
    x2d = x.reshape(batch * seq, d)
    ada = _ada(c_pad, w_ada[0], b_ada[0].reshape(1, 3 * d))[:batch]
    mod = ada.reshape(batch, 3, d)
    proj2d = _proj(x2d, mod, g_norm[0].reshape(1, d), w_in[0].astype(BF16), _q_column_scale(), seq)
    ya = _dil(proj2d, batch, seq)
    lamv = jnp.stack([lambda_q1[0], lambda_k1[0], lambda_q2[0], lambda_k2[0]])
    yb = _diff(proj2d, lamv, g_subln[0].reshape(1, DIFF_HEAD_DIM), batch, seq)
    out = _out(ya, yb, proj2d, x2d, mod, g_final.reshape(1, d), w_o_dil[0].astype(BF16),
               w_o_diff[0].astype(BF16), w_out[0].astype(BF16), seq)
    return out.reshape(batch, seq, d)
```

```python
import functools
import math

import numpy as np
import jax
import jax.numpy as jnp
from jax import lax
from jax.experimental import pallas as pl
from jax.experimental.pallas import tpu as pltpu

F32 = jnp.float32
BF16 = jnp.bfloat16

D_MODEL = 2048
HEAD_DIM = 128
LANES = 128
SUBLANES = 8
DIL_GROUPS = ((128, 1), (512, 4), (2048, 16))
DIL_HEADS_PER_GROUP = 8
DIL_N_HEADS = DIL_HEADS_PER_GROUP * len(DIL_GROUPS)
DIL_QKV = DIL_N_HEADS * HEAD_DIM
DIL_WIDTH = DIL_HEADS_PER_GROUP * HEAD_DIM
DIL_SPAN = 128
DIFF_N_HEADS = 6
DIFF_HEAD_DIM = 2 * HEAD_DIM
DIFF_WIDTH = DIFF_N_HEADS * DIFF_HEAD_DIM
EPS = 1e-6
NEG_INF = -1e30
IN_SIZES = (DIL_QKV, DIL_QKV, DIL_QKV, DIL_WIDTH,
            DIFF_WIDTH, DIFF_WIDTH, DIFF_WIDTH, DIFF_WIDTH, D_MODEL, D_MODEL)
N_IN = sum(IN_SIZES)
(OFF_QA, OFF_KA, OFF_VA, OFF_ZA, OFF_QB, OFF_KB, OFF_VB, OFF_ZB, OFF_GLA, OFF_GLB) = (
    int(v) for v in np.concatenate([[0], np.cumsum(IN_SIZES)[:-1]]))
LOG2E = 1.4426950408889634
Q_PRESCALE = HEAD_DIM ** -0.5 * LOG2E
LAMBDA_INIT = 0.8 - 0.6 * math.exp(-0.3 * 0)

VMEM_LIMIT_BYTES = 56 * 1024 * 1024

ADA_TN = 512
PROJ_TM = 1024
PROJ_TN = 2048
PROJ_NORM_CHUNK = 32
PROJ_NORM_ROWS = 128
DIL_TT = DIL_SPAN * DIL_GROUPS[-1][1]
DIL_UNITS = DIL_TT // DIL_SPAN
DIL_WIDE_R = DIL_GROUPS[-1][1]
DIL_WIDE_PITCH = 24
DIL_MERGE_CHUNK = 64
DIFF_T = 1024
DIFF_SUB = 64
DIFF_DIAG_W = 256
DIFF_ONES_ROWS = 16
OUT_TM = 256
OUT_ZB_W = 512


def _alibi_slopes(n):
    return [float(2.0 ** (-8.0 * (i + 1) / n)) for i in range(n)]


def _dot_nt(a, b):
    return lax.dot_general(a, b, (((1,), (1,)), ((), ())), preferred_element_type=F32)


def _silu(z):
    return z / (1.0 + jnp.exp(-z))


def _sigmoid(z):
    return 1.0 / (1.0 + jnp.exp(-z))


def _select_by_index(idx, values):
    out = jnp.float32(values[0])
    for i in range(1, len(values)):
        out = jnp.where(idx == i, jnp.float32(values[i]), out)
    return out


def _ada_kernel(c_ref, w_ref, b_ref, o_ref):
    c = c_ref[...]
    o_ref[...] = jnp.dot(_silu(c), w_ref[...], preferred_element_type=F32,
                         precision=lax.Precision.HIGHEST) + b_ref[...]


def _ada(c_pad, w_ada, b_ada):
    rows, d = c_pad.shape
    n = w_ada.shape[1]
    tn = ADA_TN
    return pl.pallas_call(
        _ada_kernel,
        grid=(n // tn,),
        in_specs=[pl.BlockSpec((rows, d), lambda j: (0, 0)),
                  pl.BlockSpec((d, tn), lambda j: (0, j)),
                  pl.BlockSpec((1, tn), lambda j: (0, j))],
        out_specs=pl.BlockSpec((rows, tn), lambda j: (0, j)),
        out_shape=jax.ShapeDtypeStruct((rows, n), F32),
        name="ada",
    )(c_pad, w_ada, b_ada)


def _proj_kernel(x_ref, mod_ref, g_ref, w_ref, cs_ref, o_ref, hn_a, hn_b):
    i = pl.program_id(0)
    j = pl.program_id(1)
    tm = x_ref.shape[0]
    shift = mod_ref[0, 0:1, :]
    mult = g_ref[...] * (1.0 + mod_ref[0, 1:2, :])

    def normalize(dst_ref, r0, rows):
        xx = x_ref[pl.ds(r0, rows), :]
        ms = jnp.mean(xx * xx, axis=-1, keepdims=True)
        dst_ref[pl.ds(r0, rows), :] = (xx * lax.rsqrt(ms + EPS) * mult + shift).astype(BF16)

    @pl.when(jnp.logical_and(i == 0, j == 0))
    def _():
        def body(c, carry):
            normalize(hn_a, pl.multiple_of(c * PROJ_NORM_CHUNK, PROJ_NORM_CHUNK), PROJ_NORM_CHUNK)
            return carry

        lax.fori_loop(0, tm // PROJ_NORM_CHUNK, body, 0)

    r0 = pl.multiple_of(jnp.clip((j - 1) * PROJ_NORM_ROWS, 0, tm - PROJ_NORM_ROWS), PROJ_NORM_ROWS)

    def step(cur_ref, nxt_ref):
        normalize(nxt_ref, r0, PROJ_NORM_ROWS)
        acc = jnp.dot(cur_ref[...], w_ref[...], preferred_element_type=F32)
        o_ref[...] = (acc * cs_ref[...]).astype(BF16)

    @pl.when(i % 2 == 0)
    def _():
        step(hn_a, hn_b)

    @pl.when(i % 2 == 1)
    def _():
        step(hn_b, hn_a)


def _proj(x2d, mod, g_norm, w_in_bf16, col_scale, seq):
    t, d = x2d.shape
    n = w_in_bf16.shape[1]
    tm, tn = PROJ_TM, PROJ_TN
    n_i, n_j = t // tm, n // tn
    tiles_per_batch = seq // tm
    assert (n_j - 1) * PROJ_NORM_ROWS >= tm and tm % PROJ_NORM_ROWS == 0

    def next_tile(i, j):
        return jnp.where(jnp.logical_and(i == 0, j == 0), 0, jnp.minimum(i + 1, n_i - 1))

    return pl.pallas_call(
        _proj_kernel,
        grid=(n_i, n_j),
        in_specs=[pl.BlockSpec((tm, d), lambda i, j: (next_tile(i, j), 0)),
                  pl.BlockSpec((1, 3, d), lambda i, j: (next_tile(i, j) // tiles_per_batch, 0, 0)),
                  pl.BlockSpec((1, d), lambda i, j: (0, 0)),
                  pl.BlockSpec((d, tn), lambda i, j: (0, j)),
                  pl.BlockSpec((1, tn), lambda i, j: (0, j))],
        out_specs=pl.BlockSpec((tm, tn), lambda i, j: (i, j)),
        out_shape=jax.ShapeDtypeStruct((t, n), BF16),
        scratch_shapes=[pltpu.VMEM((tm, d), BF16), pltpu.VMEM((tm, d), BF16)],
        compiler_params=pltpu.CompilerParams(
            dimension_semantics=("arbitrary", "arbitrary"),
            vmem_limit_bytes=VMEM_LIMIT_BYTES),
        name="proj",
    )(x2d, mod, g_norm, w_in_bf16, col_scale)


def _gather_residues(src_ref, stage, dst_ref, r, n_u, dst_pitch, dst_off):
    pitch = DIL_WIDE_PITCH if r == DIL_WIDE_R else r
    for c0 in range(0, n_u * r, DIL_SPAN):
        x = src_ref[c0:c0 + DIL_SPAN, :].astype(F32)
        if pitch == r:
            stage[c0:c0 + DIL_SPAN, :] = x
        else:
            for k in range(DIL_SPAN // r):
                row = (c0 // r + k) * pitch
                stage[row:row + r, :] = x[k * r:(k + 1) * r, :]
    for p in range(r):
        for c0 in range(0, n_u, DIL_SPAN):
            row = p * dst_pitch + dst_off + c0
            dst_ref[row:row + DIL_SPAN, :] = stage[
                pl.ds(p + pitch * c0, DIL_SPAN, stride=pitch), :].astype(BF16)


def _dil_kernel(*refs):
    ins, y_ref = refs[:10], refs[10]
    qd, stage, bm_ref = refs[11:14]
    kd, vd = refs[14:17], refs[17:20]
    o_scr, lse_scr = refs[20:23], refs[23:26]
    z_ref = ins[9]
    h = pl.program_id(1)
    first_tile = pl.program_id(2) == 0

    iq = lax.broadcasted_iota(jnp.int32, (DIL_SPAN, 2 * DIL_SPAN), 0)
    jk = lax.broadcasted_iota(jnp.int32, (DIL_SPAN, 2 * DIL_SPAN), 1)
    dist = (iq - jk + DIL_SPAN).astype(F32)
    valid = jnp.logical_and(jk >= iq, jk <= iq + DIL_SPAN)
    valid_first = jnp.logical_and(valid, jk >= DIL_SPAN)

    slopes = _alibi_slopes(DIL_N_HEADS)
    for g, (_, r) in enumerate(DIL_GROUPS):
        q_ref, kc_ref, vc_ref = ins[3 * g:3 * g + 3]
        kd_g, vd_g = kd[g], vd[g]
        nb = DIL_UNITS // r
        kpitch = (nb + 1) * DIL_SPAN
        coef = _select_by_index(
            h, [slopes[g * DIL_HEADS_PER_GROUP + i] * r * LOG2E for i in range(DIL_HEADS_PER_GROUP)])
        bias = -coef * dist
        bm_ref[0] = jnp.where(valid, bias, NEG_INF)
        bm_ref[1] = jnp.where(valid_first, bias, NEG_INF)

        @pl.when(first_tile)
        def _(r=r, kpitch=kpitch, kd_g=kd_g, vd_g=vd_g):
            for p in range(r):
                kd_g[p * kpitch:p * kpitch + DIL_SPAN, :] = jnp.zeros((DIL_SPAN, LANES), BF16)
                vd_g[p * kpitch:p * kpitch + DIL_SPAN, :] = jnp.zeros((DIL_SPAN, LANES), BF16)

        if r == 1:
            q_src = q_ref
            kd_g[DIL_SPAN:DIL_SPAN + DIL_TT, :] = kc_ref[...]
            vd_g[DIL_SPAN:DIL_SPAN + DIL_TT, :] = vc_ref[...]
        else:
            q_src = qd
            _gather_residues(q_ref, stage, qd, r, nb * DIL_SPAN, nb * DIL_SPAN, 0)
            _gather_residues(kc_ref, stage, kd_g, r, nb * DIL_SPAN, kpitch, DIL_SPAN)
            _gather_residues(vc_ref, stage, vd_g, r, nb * DIL_SPAN, kpitch, DIL_SPAN)

        for p in range(r):
            for ub in range(nb):
                rq = (p * nb + ub) * DIL_SPAN
                rk = p * kpitch + ub * DIL_SPAN
                bm = bm_ref[jnp.where(first_tile, 1, 0)] if ub == 0 else bm_ref[0]
                s = _dot_nt(q_src[rq:rq + DIL_SPAN, :], kd_g[rk:rk + 2 * DIL_SPAN, :]) + bm
                m = jnp.max(s, axis=-1, keepdims=True)
                e = jnp.exp2(s - m)
                l = jnp.sum(e, axis=-1, keepdims=True)
                o = jnp.dot(e.astype(BF16), vd_g[rk:rk + 2 * DIL_SPAN, :],
                            preferred_element_type=F32) * (1.0 / l)
                lse2 = jnp.broadcast_to(m + jnp.log(l) * LOG2E, (DIL_SPAN, LANES))
                if r == DIL_WIDE_R:
                    rows = pl.ds(p, DIL_SPAN, stride=DIL_WIDE_PITCH)
                elif r > 1:
                    rows = pl.ds(ub * (DIL_SPAN * r) + p, DIL_SPAN, stride=r)
                else:
                    rows = pl.ds(ub * DIL_SPAN, DIL_SPAN)
                o_scr[g][rows, :] = o
                lse_scr[g][rows, :] = lse2

        for p in range(r):
            last = p * kpitch + nb * DIL_SPAN
            kd_g[p * kpitch:p * kpitch + DIL_SPAN, :] = kd_g[last:last + DIL_SPAN, :]
            vd_g[p * kpitch:p * kpitch + DIL_SPAN, :] = vd_g[last:last + DIL_SPAN, :]

    def merge(i, carry):
        rows = pl.ds(pl.multiple_of(i * DIL_MERGE_CHUNK, DIL_MERGE_CHUNK), DIL_MERGE_CHUNK)

        def wide(ref):
            per = DIL_MERGE_CHUNK // DIL_WIDE_R
            return jnp.concatenate(
                [ref[pl.ds(pl.multiple_of((i * per + k) * DIL_WIDE_PITCH, SUBLANES), DIL_WIDE_R), :]
                 for k in range(per)], axis=0)

        l0, l1, l2 = lse_scr[0][rows, :], lse_scr[1][rows, :], wide(lse_scr[2])
        mx = jnp.maximum(jnp.maximum(l0, l1), l2)
        e0, e1, e2 = jnp.exp2(l0 - mx), jnp.exp2(l1 - mx), jnp.exp2(l2 - mx)
        om = (e0 * o_scr[0][rows, :] + e1 * o_scr[1][rows, :] + e2 * wide(o_scr[2])) / (e0 + e1 + e2)
        y_ref[rows, :] = (om * _silu(z_ref[rows, :].astype(F32))).astype(BF16)
        return carry

    lax.fori_loop(0, DIL_TT // DIL_MERGE_CHUNK, merge, 0, unroll=2)


def _dil(proj2d, batch, seq):
    tt = DIL_TT
    tiles = seq // tt

    def cur(col):
        return pl.BlockSpec((tt, LANES), lambda b, h, t: (b * tiles + t, col + h))

    in_specs = []
    for g in range(len(DIL_GROUPS)):
        in_specs += [cur((off + g * DIL_WIDTH) // LANES) for off in (OFF_QA, OFF_KA, OFF_VA)]
    in_specs.append(cur(OFF_ZA // LANES))
    kv_scratch = [pltpu.VMEM((tt + DIL_SPAN * r, LANES), BF16) for _, r in DIL_GROUPS]
    return pl.pallas_call(
        _dil_kernel,
        grid=(batch, DIL_HEADS_PER_GROUP, tiles),
        in_specs=in_specs,
        out_specs=pl.BlockSpec((tt, LANES), lambda b, h, t: (b * tiles + t, h)),
        out_shape=jax.ShapeDtypeStruct((batch * seq, DIL_WIDTH), BF16),
        scratch_shapes=(
            [pltpu.VMEM((tt, LANES), BF16), pltpu.VMEM((DIL_SPAN * DIL_WIDE_PITCH, LANES), F32),
             pltpu.VMEM((2, DIL_SPAN, 2 * DIL_SPAN), F32)]
            + kv_scratch + kv_scratch
            + [pltpu.VMEM((tt, LANES), F32)] * 2 + [pltpu.VMEM((DIL_SPAN * DIL_WIDE_PITCH, LANES), F32)]
            + [pltpu.VMEM((tt, LANES), F32)] * 2 + [pltpu.VMEM((DIL_SPAN * DIL_WIDE_PITCH, LANES), F32)]),
        compiler_params=pltpu.CompilerParams(
            dimension_semantics=("parallel", "parallel", "arbitrary"),
            vmem_limit_bytes=VMEM_LIMIT_BYTES),
        name="dil",
    )(*([proj2d] * 10))


def _diff_kernel(q_ref, k_ref, v_ref, lamv_ref, gsub_ref, o_ref,
                 vt_ref, st_ref, pt_ref, cb_ref, m_ref, a_ref, acc_ref, *, slopes2):
    h = pl.program_id(1)
    qi = pl.program_id(2)
    t = q_ref.shape[0]
    nk = vt_ref.shape[0]
    slope2 = _select_by_index(h, slopes2)

    @pl.when(qi == 0)
    def _():
        for j in range(nk):
            vt_ref[j, 0:DIFF_HEAD_DIM, :] = v_ref[j * t:(j + 1) * t, :].astype(F32).T.astype(BF16)
            vt_ref[j, DIFF_HEAD_DIM:, :] = jnp.ones((DIFF_ONES_ROWS, t), BF16)

    cb_ref[...] = slope2 * lax.broadcasted_iota(jnp.int32, (t, LANES), 0).astype(F32)
    m_ref[...] = jnp.full(m_ref.shape, NEG_INF, F32)
    acc_ref[...] = jnp.zeros(acc_ref.shape, F32)

    def block(kj):
        k0 = pl.multiple_of(kj * t, t)
        cblk = slope2 * ((kj - qi) * t).astype(F32)
        for c in range(2):
            sl = slice(c * HEAD_DIM, (c + 1) * HEAD_DIM)
            st_ref[c] = _dot_nt(k_ref[pl.ds(k0, t), sl], q_ref[:, sl])
        for c in range(2):
            for cg in range(t // LANES):
                cols = slice(cg * LANES, (cg + 1) * LANES)
                m_old = m_ref[c, :, cols]
                m_run = m_old
                m_sub = []
                for r0 in range(0, t, DIFF_SUB):
                    rows = slice(r0, r0 + DIFF_SUB)
                    x = st_ref[c, rows, cols] + cb_ref[rows, :]
                    m_run = jnp.maximum(m_run, jnp.max(x, axis=0, keepdims=True) + cblk)
                    pt_ref[c, rows, cols] = jnp.exp2((x - (m_run - cblk)).astype(BF16))
                    m_sub.append(m_run)
                for i, r0 in enumerate(range(0, t - DIFF_SUB, DIFF_SUB)):
                    rows = slice(r0, r0 + DIFF_SUB)
                    pt_ref[c, rows, cols] = (pt_ref[c, rows, cols]
                                             * jnp.exp2(m_sub[i] - m_run).astype(BF16))
                m_ref[c, :, cols] = m_run
                a_ref[c, :, cols] = jnp.exp2(m_old - m_run)
        vt = vt_ref[kj]
        for c in range(2):
            acc_ref[c] = acc_ref[c] * a_ref[c] + jnp.dot(vt, pt_ref[c], preferred_element_type=F32)

    def diagonal_block():
        k0 = pl.multiple_of(qi * t, t)
        w = DIFF_DIAG_W
        for c in range(2):
            sl = slice(c * HEAD_DIM, (c + 1) * HEAD_DIM)
            for qh in range(t // w):
                kend = (qh + 1) * w
                qs = slice(qh * w, kend)
                st_ref[c, 0:kend, qs] = _dot_nt(k_ref[pl.ds(k0, kend), sl], q_ref[qs, sl])
        for c in range(2):
            for cg in range(t // LANES):
                cols = slice(cg * LANES, (cg + 1) * LANES)
                kend = (cg * LANES // w + 1) * w
                key = lax.broadcasted_iota(jnp.int32, (kend, LANES), 0)
                qry = lax.broadcasted_iota(jnp.int32, (kend, LANES), 1) + cg * LANES
                x = jnp.where(key <= qry, st_ref[c, 0:kend, cols] + cb_ref[0:kend, :], NEG_INF)
                m_old = m_ref[c, :, cols]
                m_new = jnp.maximum(m_old, jnp.max(x, axis=0, keepdims=True))
                m_ref[c, :, cols] = m_new
                a_ref[c, :, cols] = jnp.exp2(m_old - m_new)
                pt_ref[c, 0:kend, cols] = jnp.exp2((x - m_new).astype(BF16))
        vt = vt_ref[qi]
        for c in range(2):
            for qh in range(t // w):
                kend = (qh + 1) * w
                qs = slice(qh * w, kend)
                acc_ref[c, :, qs] = (acc_ref[c, :, qs] * a_ref[c, :, qs]
                                     + jnp.dot(vt[:, 0:kend], pt_ref[c, 0:kend, qs],
                                               preferred_element_type=F32))

    def body(kj, carry):
        block(kj)
        return carry

    lax.fori_loop(0, qi, body, 0)
    diagonal_block()

    lamv = lamv_ref[...]
    lam = (jnp.exp(jnp.sum(lamv[0:1] * lamv[1:2], axis=-1, keepdims=True))
           - jnp.exp(jnp.sum(lamv[2:3] * lamv[3:4], axis=-1, keepdims=True)) + LAMBDA_INIT)
    d = DIFF_HEAD_DIM
    ot = (acc_ref[0, 0:d, :] * (1.0 / acc_ref[0, d:d + 1, :])
          - lam * (acc_ref[1, 0:d, :] * (1.0 / acc_ref[1, d:d + 1, :])))
    ms = jnp.mean(ot * ot, axis=0, keepdims=True)
    ot = ot * lax.rsqrt(ms + EPS)
    o_ref[...] = (ot.T * gsub_ref[...] * (1.0 - LAMBDA_INIT)).astype(o_ref.dtype)


def _diff(proj2d, lamv, g_subln, batch, seq):
    t = DIFF_T
    nq = seq // t
    wq, wk, wv = OFF_QB // DIFF_HEAD_DIM, OFF_KB // DIFF_HEAD_DIM, OFF_VB // DIFF_HEAD_DIM
    slopes2 = tuple(s * LOG2E for s in _alibi_slopes(DIFF_N_HEADS))
    vrows = DIFF_HEAD_DIM + DIFF_ONES_ROWS
    return pl.pallas_call(
        functools.partial(_diff_kernel, slopes2=slopes2),
        grid=(batch, DIFF_N_HEADS, nq),
        in_specs=[
            pl.BlockSpec((t, DIFF_HEAD_DIM), lambda b, h, q: (b * nq + q, wq + h)),
            pl.BlockSpec((seq, DIFF_HEAD_DIM), lambda b, h, q: (b, wk + h)),
            pl.BlockSpec((seq, DIFF_HEAD_DIM), lambda b, h, q: (b, wv + h)),
            pl.BlockSpec((4, HEAD_DIM), lambda b, h, q: (0, 0)),
            pl.BlockSpec((1, DIFF_HEAD_DIM), lambda b, h, q: (0, 0)),
        ],
        out_specs=pl.BlockSpec((t, DIFF_HEAD_DIM), lambda b, h, q: (b * nq + q, h)),
        out_shape=jax.ShapeDtypeStruct((batch * seq, DIFF_WIDTH), BF16),
        scratch_shapes=[
            pltpu.VMEM((nq, vrows, t), BF16),
            pltpu.VMEM((2, t, t), F32),
            pltpu.VMEM((2, t, t), BF16),
            pltpu.VMEM((t, LANES), F32),
            pltpu.VMEM((2, 1, t), F32),
            pltpu.VMEM((2, 1, t), F32),
            pltpu.VMEM((2, vrows, t), F32),
        ],
        compiler_params=pltpu.CompilerParams(
            dimension_semantics=("parallel", "parallel", "arbitrary"),
            vmem_limit_bytes=VMEM_LIMIT_BYTES),
        name="diff",
    )(proj2d, proj2d, proj2d, lamv, g_subln)


def _out_kernel(ya_ref, yb_ref, zb0_ref, zb1_ref, zb2_ref, gla_ref, glb_ref, x_ref, mod_ref,
                gf_ref, wd_ref, wf_ref, wo_ref, out_ref, ybz_ref):
    zb_refs = (zb0_ref, zb1_ref, zb2_ref)
    wz = zb0_ref.shape[1]
    for i in range(3):
        sl = slice(i * wz, (i + 1) * wz)
        ybz_ref[:, sl] = (yb_ref[:, sl].astype(F32) * _silu(zb_refs[i][...].astype(F32))).astype(BF16)
    pa = jnp.dot(ya_ref[...], wd_ref[...], preferred_element_type=F32)
    pb = jnp.dot(ybz_ref[...], wf_ref[...], preferred_element_type=F32)
    merged = (_sigmoid(gla_ref[...].astype(F32)) * pa + _sigmoid(glb_ref[...].astype(F32)) * pb)
    delta = jnp.dot(merged.astype(BF16), wo_ref[...], preferred_element_type=F32)
    y = x_ref[...] + mod_ref[0, 2:3, :] * delta
    ms = jnp.mean(y * y, axis=-1, keepdims=True)
    out_ref[...] = y * lax.rsqrt(ms + EPS) * gf_ref[...]


def _const_spec(shape):
    return pl.BlockSpec(shape, lambda i: (0,) * len(shape), pipeline_mode=pl.Buffered(1))


def _out(ya, yb, proj2d, x2d, mod, g_final, wd, wf, wo, seq):
    t, d = x2d.shape
    tm = OUT_TM
    tiles_per_batch = seq // tm
    wz = OUT_ZB_W
    assert OFF_ZB % wz == 0 and DIFF_WIDTH == 3 * wz and seq % tm == 0
    zb_blk = OFF_ZB // wz
    row = lambda w: pl.BlockSpec((tm, w), lambda i: (i, 0))
    in_specs = (
        [row(DIL_WIDTH), row(DIFF_WIDTH)]
        + [pl.BlockSpec((tm, wz), functools.partial(lambda i, k: (i, zb_blk + k), k=k))
           for k in range(3)]
        + [pl.BlockSpec((tm, d), lambda i: (i, OFF_GLA // D_MODEL)),
           pl.BlockSpec((tm, d), lambda i: (i, OFF_GLB // D_MODEL)),
           row(d),
           pl.BlockSpec((1, 3, d), lambda i: (i // tiles_per_batch, 0, 0)),
           _const_spec((1, d)),
           _const_spec(wd.shape), _const_spec(wf.shape), _const_spec(wo.shape)])
    return pl.pallas_call(
        _out_kernel,
        grid=(t // tm,),
        in_specs=in_specs,
        out_specs=row(d),
        out_shape=jax.ShapeDtypeStruct((t, d), F32),
        scratch_shapes=[pltpu.VMEM((tm, DIFF_WIDTH), BF16)],
        compiler_params=pltpu.CompilerParams(
            dimension_semantics=("parallel",),
            vmem_limit_bytes=VMEM_LIMIT_BYTES),
        name="out",
    )(ya, yb, proj2d, proj2d, proj2d, proj2d, proj2d, x2d, mod, g_final, wd, wf, wo)


def _q_column_scale():
    cs = np.ones((1, N_IN), np.float32)
    cs[:, OFF_QA:OFF_QA + DIL_QKV] = Q_PRESCALE
    cs[:, OFF_QB:OFF_QB + DIFF_WIDTH] = Q_PRESCALE
    return jnp.asarray(cs)


def kernel(x, c, w_ada, b_ada, g_norm, w_in, w_o_dil, w_o_diff, w_out,
           lambda_q1, lambda_k1, lambda_q2, lambda_k2, g_subln, g_final):
    batch, seq, d = x.shape
    assert w_ada.shape[0] == 1 and d == D_MODEL and w_in.shape[2] == N_IN
    assert seq % DIL_TT == 0 and seq % PROJ_TM == 0 and seq % DIFF_T == 0
    rows = -(-batch // SUBLANES) * SUBLANES
    c_pad = jnp.pad(c, ((0, rows - batch), (0, 0)))
    x2d = x.reshape(batch * seq, d)
    ada = _ada(c_pad, w_ada[0], b_ada[0].reshape(1, 3 * d))[:batch]
    mod = ada.reshape(batch, 3, d)
    proj2d = _proj(x2d, mod, g_norm[0].reshape(1, d), w_in[0].astype(BF16), _q_column_scale(), seq)
    ya = _dil(proj2d, batch, seq)
    lamv = jnp.stack([lambda_q1[0], lambda_k1[0], lambda_q2[0], lambda_k2[0]])
    yb = _diff(proj2d, lamv, g_subln[0].reshape(1, DIFF_HEAD_DIM), batch, seq)
    out = _out(ya, yb, proj2d, x2d, mod, g_final.reshape(1, d), w_o_dil[0].astype(BF16),
               w_o_diff[0].astype(BF16), w_out[0].astype(BF16), seq)
    return out.reshape(batch, seq, d)
```

```python
import functools
import math

import numpy as np
import jax
import jax.numpy as jnp
from jax import lax
from jax.experimental import pallas as pl
from jax.experimental.pallas import tpu as pltpu

F32 = jnp.float32
BF16 = jnp.bfloat16

D_MODEL = 2048
HEAD_DIM = 128
LANES = 128
SUBLANES = 8
DIL_GROUPS = ((128, 1), (512, 4), (2048, 16))
DIL_HEADS_PER_GROUP = 8
DIL_N_HEADS = DIL_HEADS_PER_GROUP * len(DIL_GROUPS)
DIL_QKV = DIL_N_HEADS * HEAD_DIM
DIL_WIDTH = DIL_HEADS_PER_GROUP * HEAD_DIM
DIL_SPAN = 128
DIFF_N_HEADS = 6
DIFF_HEAD_DIM = 2 * HEAD_DIM
DIFF_WIDTH = DIFF_N_HEADS * DIFF_HEAD_DIM
EPS = 1e-6
NEG_INF = -1e30
IN_SIZES = (DIL_QKV, DIL_QKV, DIL_QKV, DIL_WIDTH,
            DIFF_WIDTH, DIFF_WIDTH, DIFF_WIDTH, DIFF_WIDTH, D_MODEL, D_MODEL)
N_IN = sum(IN_SIZES)
(OFF_QA, OFF_KA, OFF_VA, OFF_ZA, OFF_QB, OFF_KB, OFF_VB, OFF_ZB, OFF_GLA, OFF_GLB) = (
    int(v) for v in np.concatenate([[0], np.cumsum(IN_SIZES)[:-1]]))
LOG2E = 1.4426950408889634
Q_PRESCALE = HEAD_DIM ** -0.5 * LOG2E
LAMBDA_INIT = 0.8 - 0.6 * math.exp(-0.3 * 0)

VMEM_LIMIT_BYTES = 56 * 1024 * 1024

ADA_TN = 512
PROJ_TM = 1024
PROJ_TN = 2048
PROJ_NORM_CHUNK = 32
PROJ_NORM_ROWS = 128
DIL_TT = DIL_SPAN * DIL_GROUPS[-1][1]
DIL_UNITS = DIL_TT // DIL_SPAN
DIL_WIDE_R = DIL_GROUPS[-1][1]
DIL_WIDE_PITCH = 24
DIL_MERGE_CHUNK = 64
DIFF_T = 1024
DIFF_SUB = 256
DIFF_DIAG_W = 256
DIFF_ONES_ROWS = 16
OUT_TM = 256
OUT_ZB_W = 512


def _alibi_slopes(n):
    return [float(2.0 ** (-8.0 * (i + 1) / n)) for i in range(n)]


def _dot_nt(a, b):
    return lax.dot_general(a, b, (((1,), (1,)), ((), ())), preferred_element_type=F32)


def _silu(z):
    return z / (1.0 + jnp.exp(-z))


def _sigmoid(z):
    return 1.0 / (1.0 + jnp.exp(-z))


def _select_by_index(idx, values):
    out = jnp.float32(values[0])
    for i in range(1, len(values)):
        out = jnp.where(idx == i, jnp.float32(values[i]), out)
    return out


def _ada_kernel(c_ref, w_ref, b_ref, o_ref):
    c = c_ref[...]
    o_ref[...] = jnp.dot(_silu(c), w_ref[...], preferred_element_type=F32,
                         precision=lax.Precision.HIGHEST) + b_ref[...]


def _ada(c_pad, w_ada, b_ada):
    rows, d = c_pad.shape
    n = w_ada.shape[1]
    tn = ADA_TN
    return pl.pallas_call(
        _ada_kernel,
        grid=(n // tn,),
        in_specs=[pl.BlockSpec((rows, d), lambda j: (0, 0)),
                  pl.BlockSpec((d, tn), lambda j: (0, j)),
                  pl.BlockSpec((1, tn), lambda j: (0, j))],
        out_specs=pl.BlockSpec((rows, tn), lambda j: (0, j)),
        out_shape=jax.ShapeDtypeStruct((rows, n), F32),
        name="ada",
    )(c_pad, w_ada, b_ada)


def _proj_kernel(x_ref, mod_ref, g_ref, w_ref, cs_ref, o_ref, hn_a, hn_b):
    i = pl.program_id(0)
    j = pl.program_id(1)
    tm = x_ref.shape[0]
    shift = mod_ref[0, 0:1, :]
    mult = g_ref[...] * (1.0 + mod_ref[0, 1:2, :])

    def normalize(dst_ref, r0, rows):
        xx = x_ref[pl.ds(r0, rows), :]
        ms = jnp.mean(xx * xx, axis=-1, keepdims=True)
        dst_ref[pl.ds(r0, rows), :] = (xx * lax.rsqrt(ms + EPS) * mult + shift).astype(BF16)

    @pl.when(jnp.logical_and(i == 0, j == 0))
    def _():
        def body(c, carry):
            normalize(hn_a, pl.multiple_of(c * PROJ_NORM_CHUNK, PROJ_NORM_CHUNK), PROJ_NORM_CHUNK)
            return carry

        lax.fori_loop(0, tm // PROJ_NORM_CHUNK, body, 0)

    r0 = pl.multiple_of(jnp.clip((j - 1) * PROJ_NORM_ROWS, 0, tm - PROJ_NORM_ROWS), PROJ_NORM_ROWS)

    def step(cur_ref, nxt_ref):
        normalize(nxt_ref, r0, PROJ_NORM_ROWS)
        acc = jnp.dot(cur_ref[...], w_ref[...], preferred_element_type=F32)
        o_ref[...] = (acc * cs_ref[...]).astype(BF16)

    @pl.when(i % 2 == 0)
    def _():
        step(hn_a, hn_b)

    @pl.when(i % 2 == 1)
    def _():
        step(hn_b, hn_a)


def _proj(x2d, mod, g_norm, w_in_bf16, col_scale, seq):
    t, d = x2d.shape
    n = w_in_bf16.shape[1]
    tm, tn = PROJ_TM, PROJ_TN
    n_i, n_j = t // tm, n // tn
    tiles_per_batch = seq // tm
    assert (n_j - 1) * PROJ_NORM_ROWS >= tm and tm % PROJ_NORM_ROWS == 0

    def next_tile(i, j):
        return jnp.where(jnp.logical_and(i == 0, j == 0), 0, jnp.minimum(i + 1, n_i - 1))

    return pl.pallas_call(
        _proj_kernel,
        grid=(n_i, n_j),
        in_specs=[pl.BlockSpec((tm, d), lambda i, j: (next_tile(i, j), 0)),
                  pl.BlockSpec((1, 3, d), lambda i, j: (next_tile(i, j) // tiles_per_batch, 0, 0)),
                  pl.BlockSpec((1, d), lambda i, j: (0, 0)),
                  pl.BlockSpec((d, tn), lambda i, j: (0, j)),
                  pl.BlockSpec((1, tn), lambda i, j: (0, j))],
        out_specs=pl.BlockSpec((tm, tn), lambda i, j: (i, j)),
        out_shape=jax.ShapeDtypeStruct((t, n), BF16),
        scratch_shapes=[pltpu.VMEM((tm, d), BF16), pltpu.VMEM((tm, d), BF16)],
        compiler_params=pltpu.CompilerParams(
            dimension_semantics=("arbitrary", "arbitrary"),
            vmem_limit_bytes=VMEM_LIMIT_BYTES),
        name="proj",
    )(x2d, mod, g_norm, w_in_bf16, col_scale)


def _gather_residues(src_ref, stage, dst_ref, r, n_u, dst_pitch, dst_off):
    pitch = DIL_WIDE_PITCH if r == DIL_WIDE_R else r
    for c0 in range(0, n_u * r, DIL_SPAN):
        x = src_ref[c0:c0 + DIL_SPAN, :].astype(F32)
        if pitch == r:
            stage[c0:c0 + DIL_SPAN, :] = x
        else:
            for k in range(DIL_SPAN // r):
                row = (c0 // r + k) * pitch
                stage[row:row + r, :] = x[k * r:(k + 1) * r, :]
    for p in range(r):
        for c0 in range(0, n_u, DIL_SPAN):
            row = p * dst_pitch + dst_off + c0
            dst_ref[row:row + DIL_SPAN, :] = stage[
                pl.ds(p + pitch * c0, DIL_SPAN, stride=pitch), :].astype(BF16)


def _dil_kernel(*refs):
    ins, y_ref = refs[:10], refs[10]
    qd, stage, bm_ref = refs[11:14]
    kd, vd = refs[14:17], refs[17:20]
    o_scr, lse_scr = refs[20:23], refs[23:26]
    z_ref = ins[9]
    h = pl.program_id(1)
    first_tile = pl.program_id(2) == 0

    iq = lax.broadcasted_iota(jnp.int32, (DIL_SPAN, 2 * DIL_SPAN), 0)
    jk = lax.broadcasted_iota(jnp.int32, (DIL_SPAN, 2 * DIL_SPAN), 1)
    dist = (iq - jk + DIL_SPAN).astype(F32)
    valid = jnp.logical_and(jk >= iq, jk <= iq + DIL_SPAN)
    valid_first = jnp.logical_and(valid, jk >= DIL_SPAN)

    slopes = _alibi_slopes(DIL_N_HEADS)
    for g, (_, r) in enumerate(DIL_GROUPS):
        q_ref, kc_ref, vc_ref = ins[3 * g:3 * g + 3]
        kd_g, vd_g = kd[g], vd[g]
        nb = DIL_UNITS // r
        kpitch = (nb + 1) * DIL_SPAN
        coef = _select_by_index(
            h, [slopes[g * DIL_HEADS_PER_GROUP + i] * r * LOG2E for i in range(DIL_HEADS_PER_GROUP)])
        bias = -coef * dist
        bm_ref[0] = jnp.where(valid, bias, NEG_INF)
        bm_ref[1] = jnp.where(valid_first, bias, NEG_INF)

        @pl.when(first_tile)
        def _(r=r, kpitch=kpitch, kd_g=kd_g, vd_g=vd_g):
            for p in range(r):
                kd_g[p * kpitch:p * kpitch + DIL_SPAN, :] = jnp.zeros((DIL_SPAN, LANES), BF16)
                vd_g[p * kpitch:p * kpitch + DIL_SPAN, :] = jnp.zeros((DIL_SPAN, LANES), BF16)

        if r == 1:
            q_src = q_ref
            kd_g[DIL_SPAN:DIL_SPAN + DIL_TT, :] = kc_ref[...]
            vd_g[DIL_SPAN:DIL_SPAN + DIL_TT, :] = vc_ref[...]
        else:
            q_src = qd
            _gather_residues(q_ref, stage, qd, r, nb * DIL_SPAN, nb * DIL_SPAN, 0)
            _gather_residues(kc_ref, stage, kd_g, r, nb * DIL_SPAN, kpitch, DIL_SPAN)
            _gather_residues(vc_ref, stage, vd_g, r, nb * DIL_SPAN, kpitch, DIL_SPAN)

        for p in range(r):
            for ub in range(nb):
                rq = (p * nb + ub) * DIL_SPAN
                rk = p * kpitch + ub * DIL_SPAN
                bm = bm_ref[jnp.where(first_tile, 1, 0)] if ub == 0 else bm_ref[0]
                s = _dot_nt(q_src[rq:rq + DIL_SPAN, :], kd_g[rk:rk + 2 * DIL_SPAN, :]) + bm
                m = jnp.max(s, axis=-1, keepdims=True)
                e = jnp.exp2(s - m)
                l = jnp.sum(e, axis=-1, keepdims=True)
                o = jnp.dot(e.astype(BF16), vd_g[rk:rk + 2 * DIL_SPAN, :],
                            preferred_element_type=F32) * (1.0 / l)
                lse2 = jnp.broadcast_to(m + jnp.log(l) * LOG2E, (DIL_SPAN, LANES))
                if r == DIL_WIDE_R:
                    rows = pl.ds(p, DIL_SPAN, stride=DIL_WIDE_PITCH)
                elif r > 1:
                    rows = pl.ds(ub * (DIL_SPAN * r) + p, DIL_SPAN, stride=r)
                else:
                    rows = pl.ds(ub * DIL_SPAN, DIL_SPAN)
                o_scr[g][rows, :] = o
                lse_scr[g][rows, :] = lse2

        for p in range(r):
            last = p * kpitch + nb * DIL_SPAN
            kd_g[p * kpitch:p * kpitch + DIL_SPAN, :] = kd_g[last:last + DIL_SPAN, :]
            vd_g[p * kpitch:p * kpitch + DIL_SPAN, :] = vd_g[last:last + DIL_SPAN, :]

    def merge(i, carry):
        rows = pl.ds(pl.multiple_of(i * DIL_MERGE_CHUNK, DIL_MERGE_CHUNK), DIL_MERGE_CHUNK)

        def wide(ref):
            per = DIL_MERGE_CHUNK // DIL_WIDE_R
            return jnp.concatenate(
                [ref[pl.ds(pl.multiple_of((i * per + k) * DIL_WIDE_PITCH, SUBLANES), DIL_WIDE_R), :]
                 for k in range(per)], axis=0)

        l0, l1, l2 = lse_scr[0][rows, :], lse_scr[1][rows, :], wide(lse_scr[2])
        mx = jnp.maximum(jnp.maximum(l0, l1), l2)
        e0, e1, e2 = jnp.exp2(l0 - mx), jnp.exp2(l1 - mx), jnp.exp2(l2 - mx)
        om = (e0 * o_scr[0][rows, :] + e1 * o_scr[1][rows, :] + e2 * wide(o_scr[2])) / (e0 + e1 + e2)
        y_ref[rows, :] = (om * _silu(z_ref[rows, :].astype(F32))).astype(BF16)
        return carry

    lax.fori_loop(0, DIL_TT // DIL_MERGE_CHUNK, merge, 0, unroll=2)


def _dil(proj2d, batch, seq):
    tt = DIL_TT
    tiles = seq // tt

    def cur(col):
        return pl.BlockSpec((tt, LANES), lambda b, h, t: (b * tiles + t, col + h))

    in_specs = []
    for g in range(len(DIL_GROUPS)):
        in_specs += [cur((off + g * DIL_WIDTH) // LANES) for off in (OFF_QA, OFF_KA, OFF_VA)]
    in_specs.append(cur(OFF_ZA // LANES))
    kv_scratch = [pltpu.VMEM((tt + DIL_SPAN * r, LANES), BF16) for _, r in DIL_GROUPS]
    return pl.pallas_call(
        _dil_kernel,
        grid=(batch, DIL_HEADS_PER_GROUP, tiles),
        in_specs=in_specs,
        out_specs=pl.BlockSpec((tt, LANES), lambda b, h, t: (b * tiles + t, h)),
        out_shape=jax.ShapeDtypeStruct((batch * seq, DIL_WIDTH), BF16),
        scratch_shapes=(
            [pltpu.VMEM((tt, LANES), BF16), pltpu.VMEM((DIL_SPAN * DIL_WIDE_PITCH, LANES), F32),
             pltpu.VMEM((2, DIL_SPAN, 2 * DIL_SPAN), F32)]
            + kv_scratch + kv_scratch
            + [pltpu.VMEM((tt, LANES), F32)] * 2 + [pltpu.VMEM((DIL_SPAN * DIL_WIDE_PITCH, LANES), F32)]
            + [pltpu.VMEM((tt, LANES), F32)] * 2 + [pltpu.VMEM((DIL_SPAN * DIL_WIDE_PITCH, LANES), F32)]),
        compiler_params=pltpu.CompilerParams(
            dimension_semantics=("parallel", "parallel", "arbitrary"),
            vmem_limit_bytes=VMEM_LIMIT_BYTES),
        name="dil",
    )(*([proj2d] * 10))


def _diff_kernel(q_ref, k_ref, v_ref, lamv_ref, gsub_ref, o_ref,
                 vt_ref, st_ref, pt_ref, cb_ref, m_ref, a_ref, acc_ref, *, slopes2):
    h = pl.program_id(1)
    qi = pl.program_id(2)
    t = q_ref.shape[0]
    nk = vt_ref.shape[0]
    slope2 = _select_by_index(h, slopes2)

    @pl.when(qi == 0)
    def _():
        for j in range(nk):
            vt_ref[j, 0:DIFF_HEAD_DIM, :] = v_ref[j * t:(j + 1) * t, :].astype(F32).T.astype(BF16)
            vt_ref[j, DIFF_HEAD_DIM:, :] = jnp.ones((DIFF_ONES_ROWS, t), BF16)

    cb_ref[...] = slope2 * lax.broadcasted_iota(jnp.int32, (t, LANES), 0).astype(F32)
    m_ref[...] = jnp.full(m_ref.shape, NEG_INF, F32)
    acc_ref[...] = jnp.zeros(acc_ref.shape, F32)

    def block(kj):
        k0 = pl.multiple_of(kj * t, t)
        cblk = slope2 * ((kj - qi) * t).astype(F32)
        for c in range(2):
            sl = slice(c * HEAD_DIM, (c + 1) * HEAD_DIM)
            st_ref[c] = _dot_nt(k_ref[pl.ds(k0, t), sl], q_ref[:, sl])
        for c in range(2):
            for cg in range(t // LANES):
                cols = slice(cg * LANES, (cg + 1) * LANES)
                m_old = m_ref[c, :, cols]
                m_run = m_old
                m_sub = []
                for r0 in range(0, t, DIFF_SUB):
                    rows = slice(r0, r0 + DIFF_SUB)
                    x = st_ref[c, rows, cols] + cb_ref[rows, :]
                    m_run = jnp.maximum(m_run, jnp.max(x, axis=0, keepdims=True) + cblk)
                    pt_ref[c, rows, cols] = jnp.exp2((x - (m_run - cblk)).astype(BF16))
                    m_sub.append(m_run)
                for i, r0 in enumerate(range(0, t - DIFF_SUB, DIFF_SUB)):
                    rows = slice(r0, r0 + DIFF_SUB)
                    pt_ref[c, rows, cols] = (pt_ref[c, rows, cols]
                                             * jnp.exp2(m_sub[i] - m_run).astype(BF16))
                m_ref[c, :, cols] = m_run
                a_ref[c, :, cols] = jnp.exp2(m_old - m_run)
        vt = vt_ref[kj]
        for c in range(2):
            acc_ref[c] = acc_ref[c] * a_ref[c] + jnp.dot(vt, pt_ref[c], preferred_element_type=F32)

    def diagonal_block():
        k0 = pl.multiple_of(qi * t, t)
        w = DIFF_DIAG_W
        for c in range(2):
            sl = slice(c * HEAD_DIM, (c + 1) * HEAD_DIM)
            for qh in range(t // w):
                kend = (qh + 1) * w
                qs = slice(qh * w, kend)
                st_ref[c, 0:kend, qs] = _dot_nt(k_ref[pl.ds(k0, kend), sl], q_ref[qs, sl])
        for c in range(2):
            for cg in range(t // LANES):
                cols = slice(cg * LANES, (cg + 1) * LANES)
                kend = (cg * LANES // w + 1) * w
                key = lax.broadcasted_iota(jnp.int32, (kend, LANES), 0)
                qry = lax.broadcasted_iota(jnp.int32, (kend, LANES), 1) + cg * LANES
                x = jnp.where(key <= qry, st_ref[c, 0:kend, cols] + cb_ref[0:kend, :], NEG_INF)
                m_old = m_ref[c, :, cols]
                m_new = jnp.maximum(m_old, jnp.max(x, axis=0, keepdims=True))
                m_ref[c, :, cols] = m_new
                a_ref[c, :, cols] = jnp.exp2(m_old - m_new)
                pt_ref[c, 0:kend, cols] = jnp.exp2((x - m_new).astype(BF16))
        vt = vt_ref[qi]
        for c in range(2):
            for qh in range(t // w):
                kend = (qh + 1) * w
                qs = slice(qh * w, kend)
                acc_ref[c, :, qs] = (acc_ref[c, :, qs] * a_ref[c, :, qs]
                                     + jnp.dot(vt[:, 0:kend], pt_ref[c, 0:kend, qs],
                                               preferred_element_type=F32))

    def body(kj, carry):
        block(kj)
        return carry

    lax.fori_loop(0, qi, body, 0)
    diagonal_block()

    lamv = lamv_ref[...]
    lam = (jnp.exp(jnp.sum(lamv[0:1] * lamv[1:2], axis=-1, keepdims=True))
           - jnp.exp(jnp.sum(lamv[2:3] * lamv[3:4], axis=-1, keepdims=True)) + LAMBDA_INIT)
    d = DIFF_HEAD_DIM
    ot = (acc_ref[0, 0:d, :] * (1.0 / acc_ref[0, d:d + 1, :])
          - lam * (acc_ref[1, 0:d, :] * (1.0 / acc_ref[1, d:d + 1, :])))
    ms = jnp.mean(ot * ot, axis=0, keepdims=True)
    ot = ot * lax.rsqrt(ms + EPS)
    o_ref[...] = (ot.T * gsub_ref[...] * (1.0 - LAMBDA_INIT)).astype(o_ref.dtype)


def _diff(proj2d, lamv, g_subln, batch, seq):
    t = DIFF_T
    nq = seq // t
    wq, wk, wv = OFF_QB // DIFF_HEAD_DIM, OFF_KB // DIFF_HEAD_DIM, OFF_VB // DIFF_HEAD_DIM
    slopes2 = tuple(s * LOG2E for s in _alibi_slopes(DIFF_N_HEADS))
    vrows = DIFF_HEAD_DIM + DIFF_ONES_ROWS
    return pl.pallas_call(
        functools.partial(_diff_kernel, slopes2=slopes2),
        grid=(batch, DIFF_N_HEADS, nq),
        in_specs=[
            pl.BlockSpec((t, DIFF_HEAD_DIM), lambda b, h, q: (b * nq + q, wq + h)),
            pl.BlockSpec((seq, DIFF_HEAD_DIM), lambda b, h, q: (b, wk + h)),
            pl.BlockSpec((seq, DIFF_HEAD_DIM), lambda b, h, q: (b, wv + h)),
            pl.BlockSpec((4, HEAD_DIM), lambda b, h, q: (0, 0)),
            pl.BlockSpec((1, DIFF_HEAD_DIM), lambda b, h, q: (0, 0)),
        ],
        out_specs=pl.BlockSpec((t, DIFF_HEAD_DIM), lambda b, h, q: (b * nq + q, h)),
        out_shape=jax.ShapeDtypeStruct((batch * seq, DIFF_WIDTH), BF16),
        scratch_shapes=[
            pltpu.VMEM((nq, vrows, t), BF16),
            pltpu.VMEM((2, t, t), F32),
            pltpu.VMEM((2, t, t), BF16),
            pltpu.VMEM((t, LANES), F32),
            pltpu.VMEM((2, 1, t), F32),
            pltpu.VMEM((2, 1, t), F32),
            pltpu.VMEM((2, vrows, t), F32),
        ],
        compiler_params=pltpu.CompilerParams(
            dimension_semantics=("parallel", "parallel", "arbitrary"),
            vmem_limit_bytes=VMEM_LIMIT_BYTES),
        name="diff",
    )(proj2d, proj2d, proj2d, lamv, g_subln)


def _out_kernel(ya_ref, yb_ref, zb0_ref, zb1_ref, zb2_ref, gla_ref, glb_ref, x_ref, mod_ref,
                gf_ref, wd_ref, wf_ref, wo_ref, out_ref, ybz_ref):
    zb_refs = (zb0_ref, zb1_ref, zb2_ref)
    wz = zb0_ref.shape[1]
    for i in range(3):
        sl = slice(i * wz, (i + 1) * wz)
        ybz_ref[:, sl] = (yb_ref[:, sl].astype(F32) * _silu(zb_refs[i][...].astype(F32))).astype(BF16)
    pa = jnp.dot(ya_ref[...], wd_ref[...], preferred_element_type=F32)
    pb = jnp.dot(ybz_ref[...], wf_ref[...], preferred_element_type=F32)
    merged = (_sigmoid(gla_ref[...].astype(F32)) * pa + _sigmoid(glb_ref[...].astype(F32)) * pb)
    delta = jnp.dot(merged.astype(BF16), wo_ref[...], preferred_element_type=F32)
    y = x_ref[...] + mod_ref[0, 2:3, :] * delta
    ms = jnp.mean(y * y, axis=-1, keepdims=True)
    out_ref[...] = y * lax.rsqrt(ms + EPS) * gf_ref[...]


def _const_spec(shape):
    return pl.BlockSpec(shape, lambda i: (0,) * len(shape), pipeline_mode=pl.Buffered(1))


def _out(ya, yb, proj2d, x2d, mod, g_final, wd, wf, wo, seq):
    t, d = x2d.shape
    tm = OUT_TM
    tiles_per_batch = seq // tm
    wz = OUT_ZB_W
    assert OFF_ZB % wz == 0 and DIFF_WIDTH == 3 * wz and seq % tm == 0
    zb_blk = OFF_ZB // wz
    row = lambda w: pl.BlockSpec((tm, w), lambda i: (i, 0))
    in_specs = (
        [row(DIL_WIDTH), row(DIFF_WIDTH)]
        + [pl.BlockSpec((tm, wz), functools.partial(lambda i, k: (i, zb_blk + k), k=k))
           for k in range(3)]
        + [pl.BlockSpec((tm, d), lambda i: (i, OFF_GLA // D_MODEL)),
           pl.BlockSpec((tm, d), lambda i: (i, OFF_GLB // D_MODEL)),
           row(d),
           pl.BlockSpec((1, 3, d), lambda i: (i // tiles_per_batch, 0, 0)),
           _const_spec((1, d)),
           _const_spec(wd.shape), _const_spec(wf.shape), _const_spec(wo.shape)])
    return pl.pallas_call(
        _out_kernel,
        grid=(t // tm,),
        in_specs=in_specs,
        out_specs=row(d),
        out_shape=jax.ShapeDtypeStruct((t, d), F32),
        scratch_shapes=[pltpu.VMEM((tm, DIFF_WIDTH), BF16)],
        compiler_params=pltpu.CompilerParams(
            dimension_semantics=("parallel",),
            vmem_limit_bytes=VMEM_LIMIT_BYTES),
        name="out",
    )(ya, yb, proj2d, proj2d, proj2d, proj2d, proj2d, x2d, mod, g_final, wd, wf, wo)


def _q_column_scale():
    cs = np.ones((1, N_IN), np.float32)
    cs[:, OFF_QA:OFF_QA + DIL_QKV] = Q_PRESCALE
    cs[:, OFF_QB:OFF_QB + DIFF_WIDTH] = Q_PRESCALE
    return jnp.asarray(cs)


def kernel(x, c, w_ada, b_ada, g_norm, w_in, w_o_dil, w_o_diff, w_out,
           lambda_q1, lambda_k1, lambda_q2, lambda_k2, g_subln, g_final):
    batch, seq, d = x.shape
    assert w_ada.shape[0] == 1 and d == D_MODEL and w_in.shape[2] == N_IN
    assert seq % DIL_TT == 0 and seq % PROJ_TM == 0 and seq % DIFF_T == 0
    rows = -(-batch // SUBLANES) * SUBLANES
    c_pad = jnp.pad(c, ((0, rows - batch), (0, 0)))
    x2d = x.reshape(batch * seq, d)
    ada = _ada(c_pad, w_ada[0], b_ada[0].reshape(1, 3 * d))[:batch]
    mod = ada.reshape(batch, 3, d)
    proj2d = _proj(x2d, mod, g_norm[0].reshape(1, d), w_in[0].astype(BF16), _q_column_scale(), seq)
    ya = _dil(proj2d, batch, seq)
    lamv = jnp.stack([lambda_q1[0], lambda_k1[0], lambda_q2[0], lambda_k2[0]])
    yb = _diff(proj2d, lamv, g_subln[0].reshape(1, DIFF_HEAD_DIM), batch, seq)
    out = _out(ya, yb, proj2d, x2d, mod, g_final.reshape(1, d), w_o_dil[0].astype(BF16),
               w_o_diff[0].astype(BF16), w_out[0].astype(BF16), seq)
    return out.reshape(batch, seq, d)
```

```python
import functools
import math

import numpy as np
import jax
import jax.numpy as jnp
from jax import lax
from jax.experimental import pallas as pl
from jax.experimental.pallas import tpu as pltpu

F32 = jnp.float32
BF16 = jnp.bfloat16

D_MODEL = 2048
HEAD_DIM = 128
LANES = 128
SUBLANES = 8
DIL_GROUPS = ((128, 1), (512, 4), (2048, 16))
DIL_HEADS_PER_GROUP = 8
DIL_N_HEADS = DIL_HEADS_PER_GROUP * len(DIL_GROUPS)
DIL_QKV = DIL_N_HEADS * HEAD_DIM
DIL_WIDTH = DIL_HEADS_PER_GROUP * HEAD_DIM
DIL_SPAN = 128
DIFF_N_HEADS = 6
DIFF_HEAD_DIM = 2 * HEAD_DIM
DIFF_WIDTH = DIFF_N_HEADS * DIFF_HEAD_DIM
EPS = 1e-6
NEG_INF = -1e30
IN_SIZES = (DIL_QKV, DIL_QKV, DIL_QKV, DIL_WIDTH,
            DIFF_WIDTH, DIFF_WIDTH, DIFF_WIDTH, DIFF_WIDTH, D_MODEL, D_MODEL)
N_IN = sum(IN_SIZES)
(OFF_QA, OFF_KA, OFF_VA, OFF_ZA, OFF_QB, OFF_KB, OFF_VB, OFF_ZB, OFF_GLA, OFF_GLB) = (
    int(v) for v in np.concatenate([[0], np.cumsum(IN_SIZES)[:-1]]))
LOG2E = 1.4426950408889634
Q_PRESCALE = HEAD_DIM ** -0.5 * LOG2E
LAMBDA_INIT = 0.8 - 0.6 * math.exp(-0.3 * 0)

VMEM_LIMIT_BYTES = 56 * 1024 * 1024

ADA_TN = 512
PROJ_TM = 1024
PROJ_TN = 2048
PROJ_NORM_CHUNK = 32
PROJ_NORM_ROWS = 128
DIL_TT = DIL_SPAN * DIL_GROUPS[-1][1]
DIL_UNITS = DIL_TT // DIL_SPAN
DIL_WIDE_R = DIL_GROUPS[-1][1]
DIL_WIDE_PITCH = 24
DIL_MERGE_CHUNK = 64
DIFF_T = 1024
DIFF_SUB = 128
DIFF_DIAG_W = 256
DIFF_ONES_ROWS = 16
OUT_TM = 256
OUT_ZB_W = 512


def _alibi_slopes(n):
    return [float(2.0 ** (-8.0 * (i + 1) / n)) for i in range(n)]


def _dot_nt(a, b):
    return lax.dot_general(a, b, (((1,), (1,)), ((), ())), preferred_element_type=F32)


def _silu(z):
    return z / (1.0 + jnp.exp(-z))


def _sigmoid(z):
    return 1.0 / (1.0 + jnp.exp(-z))


def _select_by_index(idx, values):
    out = jnp.float32(values[0])
    for i in range(1, len(values)):
        out = jnp.where(idx == i, jnp.float32(values[i]), out)
    return out


def _ada_kernel(c_ref, w_ref, b_ref, o_ref):
    c = c_ref[...]
    o_ref[...] = jnp.dot(_silu(c), w_ref[...], preferred_element_type=F32,
                         precision=lax.Precision.HIGHEST) + b_ref[...]


def _ada(c_pad, w_ada, b_ada):
    rows, d = c_pad.shape
    n = w_ada.shape[1]
    tn = ADA_TN
    return pl.pallas_call(
        _ada_kernel,
        grid=(n // tn,),
        in_specs=[pl.BlockSpec((rows, d), lambda j: (0, 0)),
                  pl.BlockSpec((d, tn), lambda j: (0, j)),
                  pl.BlockSpec((1, tn), lambda j: (0, j))],
        out_specs=pl.BlockSpec((rows, tn), lambda j: (0, j)),
        out_shape=jax.ShapeDtypeStruct((rows, n), F32),
        name="ada",
    )(c_pad, w_ada, b_ada)


def _proj_kernel(x_ref, mod_ref, g_ref, w_ref, cs_ref, o_ref, hn_a, hn_b):
    i = pl.program_id(0)
    j = pl.program_id(1)
    tm = x_ref.shape[0]
    shift = mod_ref[0, 0:1, :]
    mult = g_ref[...] * (1.0 + mod_ref[0, 1:2, :])

    def normalize(dst_ref, r0, rows):
        xx = x_ref[pl.ds(r0, rows), :]
        ms = jnp.mean(xx * xx, axis=-1, keepdims=True)
        dst_ref[pl.ds(r0, rows), :] = (xx * lax.rsqrt(ms + EPS) * mult + shift).astype(BF16)

    @pl.when(jnp.logical_and(i == 0, j == 0))
    def _():
        def body(c, carry):
            normalize(hn_a, pl.multiple_of(c * PROJ_NORM_CHUNK, PROJ_NORM_CHUNK), PROJ_NORM_CHUNK)
            return carry

        lax.fori_loop(0, tm // PROJ_NORM_CHUNK, body, 0)

    r0 = pl.multiple_of(jnp.clip((j - 1) * PROJ_NORM_ROWS, 0, tm - PROJ_NORM_ROWS), PROJ_NORM_ROWS)

    def step(cur_ref, nxt_ref):
        normalize(nxt_ref, r0, PROJ_NORM_ROWS)
        acc = jnp.dot(cur_ref[...], w_ref[...], preferred_element_type=F32)
        o_ref[...] = (acc * cs_ref[...]).astype(BF16)

    @pl.when(i % 2 == 0)
    def _():
        step(hn_a, hn_b)

    @pl.when(i % 2 == 1)
    def _():
        step(hn_b, hn_a)


def _proj(x2d, mod, g_norm, w_in_bf16, col_scale, seq):
    t, d = x2d.shape
    n = w_in_bf16.shape[1]
    tm, tn = PROJ_TM, PROJ_TN
    n_i, n_j = t // tm, n // tn
    tiles_per_batch = seq // tm
    assert (n_j - 1) * PROJ_NORM_ROWS >= tm and tm % PROJ_NORM_ROWS == 0

    def next_tile(i, j):
        return jnp.where(jnp.logical_and(i == 0, j == 0), 0, jnp.minimum(i + 1, n_i - 1))

    return pl.pallas_call(
        _proj_kernel,
        grid=(n_i, n_j),
        in_specs=[pl.BlockSpec((tm, d), lambda i, j: (next_tile(i, j), 0)),
                  pl.BlockSpec((1, 3, d), lambda i, j: (next_tile(i, j) // tiles_per_batch, 0, 0)),
                  pl.BlockSpec((1, d), lambda i, j: (0, 0)),
                  pl.BlockSpec((d, tn), lambda i, j: (0, j)),
                  pl.BlockSpec((1, tn), lambda i, j: (0, j))],
        out_specs=pl.BlockSpec((tm, tn), lambda i, j: (i, j)),
        out_shape=jax.ShapeDtypeStruct((t, n), BF16),
        scratch_shapes=[pltpu.VMEM((tm, d), BF16), pltpu.VMEM((tm, d), BF16)],
        compiler_params=pltpu.CompilerParams(
            dimension_semantics=("arbitrary", "arbitrary"),
            vmem_limit_bytes=VMEM_LIMIT_BYTES),
        name="proj",
    )(x2d, mod, g_norm, w_in_bf16, col_scale)


def _gather_residues(src_ref, stage, dst_ref, r, n_u, dst_pitch, dst_off):
    pitch = DIL_WIDE_PITCH if r == DIL_WIDE_R else r
    for c0 in range(0, n_u * r, DIL_SPAN):
        x = src_ref[c0:c0 + DIL_SPAN, :].astype(F32)
        if pitch == r:
            stage[c0:c0 + DIL_SPAN, :] = x
        else:
            for k in range(DIL_SPAN // r):
                row = (c0 // r + k) * pitch
                stage[row:row + r, :] = x[k * r:(k + 1) * r, :]
    for p in range(r):
        for c0 in range(0, n_u, DIL_SPAN):
            row = p * dst_pitch + dst_off + c0
            dst_ref[row:row + DIL_SPAN, :] = stage[
                pl.ds(p + pitch * c0, DIL_SPAN, stride=pitch), :].astype(BF16)


def _dil_kernel(*refs):
    ins, y_ref = refs[:10], refs[10]
    qd, stage, bm_ref = refs[11:14]
    kd, vd = refs[14:17], refs[17:20]
    o_scr, lse_scr = refs[20:23], refs[23:26]
    z_ref = ins[9]
    h = pl.program_id(1)
    first_tile = pl.program_id(2) == 0

    iq = lax.broadcasted_iota(jnp.int32, (DIL_SPAN, 2 * DIL_SPAN), 0)
    jk = lax.broadcasted_iota(jnp.int32, (DIL_SPAN, 2 * DIL_SPAN), 1)
    dist = (iq - jk + DIL_SPAN).astype(F32)
    valid = jnp.logical_and(jk >= iq, jk <= iq + DIL_SPAN)
    valid_first = jnp.logical_and(valid, jk >= DIL_SPAN)

    slopes = _alibi_slopes(DIL_N_HEADS)
    for g, (_, r) in enumerate(DIL_GROUPS):
        q_ref, kc_ref, vc_ref = ins[3 * g:3 * g + 3]
        kd_g, vd_g = kd[g], vd[g]
        nb = DIL_UNITS // r
        kpitch = (nb + 1) * DIL_SPAN
        coef = _select_by_index(
            h, [slopes[g * DIL_HEADS_PER_GROUP + i] * r * LOG2E for i in range(DIL_HEADS_PER_GROUP)])
        bias = -coef * dist
        bm_ref[0] = jnp.where(valid, bias, NEG_INF)
        bm_ref[1] = jnp.where(valid_first, bias, NEG_INF)

        @pl.when(first_tile)
        def _(r=r, kpitch=kpitch, kd_g=kd_g, vd_g=vd_g):
            for p in range(r):
                kd_g[p * kpitch:p * kpitch + DIL_SPAN, :] = jnp.zeros((DIL_SPAN, LANES), BF16)
                vd_g[p * kpitch:p * kpitch + DIL_SPAN, :] = jnp.zeros((DIL_SPAN, LANES), BF16)

        if r == 1:
            q_src = q_ref
            kd_g[DIL_SPAN:DIL_SPAN + DIL_TT, :] = kc_ref[...]
            vd_g[DIL_SPAN:DIL_SPAN + DIL_TT, :] = vc_ref[...]
        else:
            q_src = qd
            _gather_residues(q_ref, stage, qd, r, nb * DIL_SPAN, nb * DIL_SPAN, 0)
            _gather_residues(kc_ref, stage, kd_g, r, nb * DIL_SPAN, kpitch, DIL_SPAN)
            _gather_residues(vc_ref, stage, vd_g, r, nb * DIL_SPAN, kpitch, DIL_SPAN)

        for p in range(r):
            for ub in range(nb):
                rq = (p * nb + ub) * DIL_SPAN
                rk = p * kpitch + ub * DIL_SPAN
                bm = bm_ref[jnp.where(first_tile, 1, 0)] if ub == 0 else bm_ref[0]
                s = _dot_nt(q_src[rq:rq + DIL_SPAN, :], kd_g[rk:rk + 2 * DIL_SPAN, :]) + bm
                m = jnp.max(s, axis=-1, keepdims=True)
                e = jnp.exp2(s - m)
                l = jnp.sum(e, axis=-1, keepdims=True)
                o = jnp.dot(e.astype(BF16), vd_g[rk:rk + 2 * DIL_SPAN, :],
                            preferred_element_type=F32) * (1.0 / l)
                lse2 = jnp.broadcast_to(m + jnp.log(l) * LOG2E, (DIL_SPAN, LANES))
                if r == DIL_WIDE_R:
                    rows = pl.ds(p, DIL_SPAN, stride=DIL_WIDE_PITCH)
                elif r > 1:
                    rows = pl.ds(ub * (DIL_SPAN * r) + p, DIL_SPAN, stride=r)
                else:
                    rows = pl.ds(ub * DIL_SPAN, DIL_SPAN)
                o_scr[g][rows, :] = o
                lse_scr[g][rows, :] = lse2

        for p in range(r):
            last = p * kpitch + nb * DIL_SPAN
            kd_g[p * kpitch:p * kpitch + DIL_SPAN, :] = kd_g[last:last + DIL_SPAN, :]
            vd_g[p * kpitch:p * kpitch + DIL_SPAN, :] = vd_g[last:last + DIL_SPAN, :]

    def merge(i, carry):
        rows = pl.ds(pl.multiple_of(i * DIL_MERGE_CHUNK, DIL_MERGE_CHUNK), DIL_MERGE_CHUNK)

        def wide(ref):
            per = DIL_MERGE_CHUNK // DIL_WIDE_R
            return jnp.concatenate(
                [ref[pl.ds(pl.multiple_of((i * per + k) * DIL_WIDE_PITCH, SUBLANES), DIL_WIDE_R), :]
                 for k in range(per)], axis=0)

        l0, l1, l2 = lse_scr[0][rows, :], lse_scr[1][rows, :], wide(lse_scr[2])
        mx = jnp.maximum(jnp.maximum(l0, l1), l2)
        e0, e1, e2 = jnp.exp2(l0 - mx), jnp.exp2(l1 - mx), jnp.exp2(l2 - mx)
        om = (e0 * o_scr[0][rows, :] + e1 * o_scr[1][rows, :] + e2 * wide(o_scr[2])) / (e0 + e1 + e2)
        y_ref[rows, :] = (om * _silu(z_ref[rows, :].astype(F32))).astype(BF16)
        return carry

    lax.fori_loop(0, DIL_TT // DIL_MERGE_CHUNK, merge, 0, unroll=2)


def _dil(proj2d, batch, seq):
    tt = DIL_TT
    tiles = seq // tt

    def cur(col):
        return pl.BlockSpec((tt, LANES), lambda b, h, t: (b * tiles + t, col + h))

    in_specs = []
    for g in range(len(DIL_GROUPS)):
        in_specs += [cur((off + g * DIL_WIDTH) // LANES) for off in (OFF_QA, OFF_KA, OFF_VA)]
    in_specs.append(cur(OFF_ZA // LANES))
    kv_scratch = [pltpu.VMEM((tt + DIL_SPAN * r, LANES), BF16) for _, r in DIL_GROUPS]
    return pl.pallas_call(
        _dil_kernel,
        grid=(batch, DIL_HEADS_PER_GROUP, tiles),
        in_specs=in_specs,
        out_specs=pl.BlockSpec((tt, LANES), lambda b, h, t: (b * tiles + t, h)),
        out_shape=jax.ShapeDtypeStruct((batch * seq, DIL_WIDTH), BF16),
        scratch_shapes=(
            [pltpu.VMEM((tt, LANES), BF16), pltpu.VMEM((DIL_SPAN * DIL_WIDE_PITCH, LANES), F32),
             pltpu.VMEM((2, DIL_SPAN, 2 * DIL_SPAN), F32)]
            + kv_scratch + kv_scratch
            + [pltpu.VMEM((tt, LANES), F32)] * 2 + [pltpu.VMEM((DIL_SPAN * DIL_WIDE_PITCH, LANES), F32)]
            + [pltpu.VMEM((tt, LANES), F32)] * 2 + [pltpu.VMEM((DIL_SPAN * DIL_WIDE_PITCH, LANES), F32)]),
        compiler_params=pltpu.CompilerParams(
            dimension_semantics=("parallel", "parallel", "arbitrary"),
            vmem_limit_bytes=VMEM_LIMIT_BYTES),
        name="dil",
    )(*([proj2d] * 10))


def _diff_kernel(q_ref, k_ref, v_ref, lamv_ref, gsub_ref, o_ref,
                 vt_ref, st_ref, pt_ref, cb_ref, m_ref, a_ref, acc_ref, *, slopes2):
    h = pl.program_id(1)
    qi = pl.program_id(2)
    t = q_ref.shape[0]
    nk = vt_ref.shape[0]
    slope2 = _select_by_index(h, slopes2)

    @pl.when(qi == 0)
    def _():
        for j in range(nk):
            vt_ref[j, 0:DIFF_HEAD_DIM, :] = v_ref[j * t:(j + 1) * t, :].astype(F32).T.astype(BF16)
            vt_ref[j, DIFF_HEAD_DIM:, :] = jnp.ones((DIFF_ONES_ROWS, t), BF16)
        cb_ref[...] = slope2 * lax.broadcasted_iota(jnp.int32, (t, LANES), 0).astype(F32)

    def block(kj):
        k0 = pl.multiple_of(kj * t, t)
        cblk = slope2 * ((kj - qi) * t).astype(F32)
        for c in range(2):
            sl = slice(c * HEAD_DIM, (c + 1) * HEAD_DIM)
            st_ref[c] = _dot_nt(k_ref[pl.ds(k0, t), sl], q_ref[:, sl])
        for c in range(2):
            for cg in range(t // LANES):
                cols = slice(cg * LANES, (cg + 1) * LANES)
                m_old = m_ref[c, :, cols]
                m_run = m_old
                m_sub = []
                for r0 in range(0, t, DIFF_SUB):
                    rows = slice(r0, r0 + DIFF_SUB)
                    x = st_ref[c, rows, cols] + cb_ref[rows, :]
                    m_run = jnp.maximum(m_run, jnp.max(x, axis=0, keepdims=True) + cblk)
                    pt_ref[c, rows, cols] = jnp.exp2((x - (m_run - cblk)).astype(BF16))
                    m_sub.append(m_run)
                for i, r0 in enumerate(range(0, t - DIFF_SUB, DIFF_SUB)):
                    rows = slice(r0, r0 + DIFF_SUB)
                    pt_ref[c, rows, cols] = (pt_ref[c, rows, cols]
                                             * jnp.exp2(m_sub[i] - m_run).astype(BF16))
                m_ref[c, :, cols] = m_run
                a_ref[c, :, cols] = jnp.exp2(m_old - m_run)
        vt = vt_ref[kj]
        for c in range(2):
            acc_ref[c] = acc_ref[c] * a_ref[c] + jnp.dot(vt, pt_ref[c], preferred_element_type=F32)

    def diagonal_block():
        k0 = pl.multiple_of(qi * t, t)
        w = DIFF_DIAG_W
        for c in range(2):
            sl = slice(c * HEAD_DIM, (c + 1) * HEAD_DIM)
            for qh in range(t // w):
                kend = (qh + 1) * w
                qs = slice(qh * w, kend)
                st_ref[c, 0:kend, qs] = _dot_nt(k_ref[pl.ds(k0, kend), sl], q_ref[qs, sl])
        for c in range(2):
            for cg in range(t // LANES):
                cols = slice(cg * LANES, (cg + 1) * LANES)
                kend = (cg * LANES // w + 1) * w
                key = lax.broadcasted_iota(jnp.int32, (kend, LANES), 0)
                qry = lax.broadcasted_iota(jnp.int32, (kend, LANES), 1) + cg * LANES
                x = jnp.where(key <= qry, st_ref[c, 0:kend, cols] + cb_ref[0:kend, :], NEG_INF)
                m_new = jnp.max(x, axis=0, keepdims=True)
                m_ref[c, :, cols] = m_new
                pt_ref[c, 0:kend, cols] = jnp.exp2((x - m_new).astype(BF16))
        vt = vt_ref[qi]
        for c in range(2):
            for qh in range(t // w):
                kend = (qh + 1) * w
                qs = slice(qh * w, kend)
                acc_ref[c, :, qs] = jnp.dot(vt[:, 0:kend], pt_ref[c, 0:kend, qs],
                                            preferred_element_type=F32)

    def body(kj, carry):
        block(kj)
        return carry

    diagonal_block()
    lax.fori_loop(0, qi, body, 0)

    lamv = lamv_ref[...]
    lam = (jnp.exp(jnp.sum(lamv[0:1] * lamv[1:2], axis=-1, keepdims=True))
           - jnp.exp(jnp.sum(lamv[2:3] * lamv[3:4], axis=-1, keepdims=True)) + LAMBDA_INIT)
    d = DIFF_HEAD_DIM
    ot = (acc_ref[0, 0:d, :] * (1.0 / acc_ref[0, d:d + 1, :])
          - lam * (acc_ref[1, 0:d, :] * (1.0 / acc_ref[1, d:d + 1, :])))
    ms = jnp.mean(ot * ot, axis=0, keepdims=True)
    ot = ot * lax.rsqrt(ms + EPS)
    o_ref[...] = (ot.T * gsub_ref[...] * (1.0 - LAMBDA_INIT)).astype(o_ref.dtype)


def _diff(proj2d, lamv, g_subln, batch, seq):
    t = DIFF_T
    nq = seq // t
    wq, wk, wv = OFF_QB // DIFF_HEAD_DIM, OFF_KB // DIFF_HEAD_DIM, OFF_VB // DIFF_HEAD_DIM
    slopes2 = tuple(s * LOG2E for s in _alibi_slopes(DIFF_N_HEADS))
    vrows = DIFF_HEAD_DIM + DIFF_ONES_ROWS
    return pl.pallas_call(
        functools.partial(_diff_kernel, slopes2=slopes2),
        grid=(batch, DIFF_N_HEADS, nq),
        in_specs=[
            pl.BlockSpec((t, DIFF_HEAD_DIM), lambda b, h, q: (b * nq + q, wq + h)),
            pl.BlockSpec((seq, DIFF_HEAD_DIM), lambda b, h, q: (b, wk + h)),
            pl.BlockSpec((seq, DIFF_HEAD_DIM), lambda b, h, q: (b, wv + h)),
            pl.BlockSpec((4, HEAD_DIM), lambda b, h, q: (0, 0)),
            pl.BlockSpec((1, DIFF_HEAD_DIM), lambda b, h, q: (0, 0)),
        ],
        out_specs=pl.BlockSpec((t, DIFF_HEAD_DIM), lambda b, h, q: (b * nq + q, h)),
        out_shape=jax.ShapeDtypeStruct((batch * seq, DIFF_WIDTH), BF16),
        scratch_shapes=[
            pltpu.VMEM((nq, vrows, t), BF16),
            pltpu.VMEM((2, t, t), F32),
            pltpu.VMEM((2, t, t), BF16),
            pltpu.VMEM((t, LANES), F32),
            pltpu.VMEM((2, 1, t), F32),
            pltpu.VMEM((2, 1, t), F32),
            pltpu.VMEM((2, vrows, t), F32),
        ],
        compiler_params=pltpu.CompilerParams(
            dimension_semantics=("parallel", "parallel", "arbitrary"),
            vmem_limit_bytes=VMEM_LIMIT_BYTES),
        name="diff",
    )(proj2d, proj2d, proj2d, lamv, g_subln)


def _out_kernel(ya_ref, yb_ref, zb0_ref, zb1_ref, zb2_ref, gla_ref, glb_ref, x_ref, mod_ref,
                gf_ref, wd_ref, wf_ref, wo_ref, out_ref, ybz_ref):
    zb_refs = (zb0_ref, zb1_ref, zb2_ref)
    wz = zb0_ref.shape[1]
    for i in range(3):
        sl = slice(i * wz, (i + 1) * wz)
        ybz_ref[:, sl] = (yb_ref[:, sl].astype(F32) * _silu(zb_refs[i][...].astype(F32))).astype(BF16)
    pa = jnp.dot(ya_ref[...], wd_ref[...], preferred_element_type=F32)
    pb = jnp.dot(ybz_ref[...], wf_ref[...], preferred_element_type=F32)
    merged = (_sigmoid(gla_ref[...].astype(F32)) * pa + _sigmoid(glb_ref[...].astype(F32)) * pb)
    delta = jnp.dot(merged.astype(BF16), wo_ref[...], preferred_element_type=F32)
    y = x_ref[...] + mod_ref[0, 2:3, :] * delta
    ms = jnp.mean(y * y, axis=-1, keepdims=True)
    out_ref[...] = y * lax.rsqrt(ms + EPS) * gf_ref[...]


def _const_spec(shape):
    return pl.BlockSpec(shape, lambda i: (0,) * len(shape), pipeline_mode=pl.Buffered(1))


def _out(ya, yb, proj2d, x2d, mod, g_final, wd, wf, wo, seq):
    t, d = x2d.shape
    tm = OUT_TM
    tiles_per_batch = seq // tm
    wz = OUT_ZB_W
    assert OFF_ZB % wz == 0 and DIFF_WIDTH == 3 * wz and seq % tm == 0
    zb_blk = OFF_ZB // wz
    row = lambda w: pl.BlockSpec((tm, w), lambda i: (i, 0))
    in_specs = (
        [row(DIL_WIDTH), row(DIFF_WIDTH)]
        + [pl.BlockSpec((tm, wz), functools.partial(lambda i, k: (i, zb_blk + k), k=k))
           for k in range(3)]
        + [pl.BlockSpec((tm, d), lambda i: (i, OFF_GLA // D_MODEL)),
           pl.BlockSpec((tm, d), lambda i: (i, OFF_GLB // D_MODEL)),
           row(d),
           pl.BlockSpec((1, 3, d), lambda i: (i // tiles_per_batch, 0, 0)),
           _const_spec((1, d)),
           _const_spec(wd.shape), _const_spec(wf.shape), _const_spec(wo.shape)])
    return pl.pallas_call(
        _out_kernel,
        grid=(t // tm,),
        in_specs=in_specs,
        out_specs=row(d),
        out_shape=jax.ShapeDtypeStruct((t, d), F32),
        scratch_shapes=[pltpu.VMEM((tm, DIFF_WIDTH), BF16)],
        compiler_params=pltpu.CompilerParams(
            dimension_semantics=("parallel",),
            vmem_limit_bytes=VMEM_LIMIT_BYTES),
        name="out",
    )(ya, yb, proj2d, proj2d, proj2d, proj2d, proj2d, x2d, mod, g_final, wd, wf, wo)


def _q_column_scale():
    cs = np.ones((1, N_IN), np.float32)
    cs[:, OFF_QA:OFF_QA + DIL_QKV] = Q_PRESCALE
    cs[:, OFF_QB:OFF_QB + DIFF_WIDTH] = Q_PRESCALE
    return jnp.asarray(cs)


def kernel(x, c, w_ada, b_ada, g_norm, w_in, w_o_dil, w_o_diff, w_out,
           lambda_q1, lambda_k1, lambda_q2, lambda_k2, g_subln, g_final):
    batch, seq, d = x.shape
    assert w_ada.shape[0] == 1 and d == D_MODEL and w_in.shape[2] == N_IN
    assert seq % DIL_TT == 0 and seq % PROJ_TM == 0 and seq % DIFF_T == 0
    rows = -(-batch // SUBLANES) * SUBLANES
    c_pad = jnp.pad(c, ((0, rows - batch), (0, 0)))
    x2d = x.reshape(batch * seq, d)
    ada = _ada(c_pad, w_ada[0], b_ada[0].reshape(1, 3 * d))[:batch]
    mod = ada.reshape(batch, 3, d)
    proj2d = _proj(x2d, mod, g_norm[0].reshape(1, d), w_in[0].astype(BF16), _q_column_scale(), seq)
    ya = _dil(proj2d, batch, seq)
    lamv = jnp.stack([lambda_q1[0], lambda_k1[0], lambda_q2[0], lambda_k2[0]])
    yb = _diff(proj2d, lamv, g_subln[0].reshape(1, DIFF_HEAD_DIM), batch, seq)
    out = _out(ya, yb, proj2d, x2d, mod, g_final.reshape(1, d), w_o_dil[0].astype(BF16),
               w_o_diff[0].astype(BF16), w_out[0].astype(BF16), seq)
    return out.reshape(batch, seq, d)
```

```python
import functools
import math

import numpy as np
import jax
import jax.numpy as jnp
from jax import lax
from jax.experimental import pallas as pl
from jax.experimental.pallas import tpu as pltpu

F32 = jnp.float32
BF16 = jnp.bfloat16

D_MODEL = 2048
HEAD_DIM = 128
LANES = 128
SUBLANES = 8
DIL_GROUPS = ((128, 1), (512, 4), (2048, 16))
DIL_HEADS_PER_GROUP = 8
DIL_N_HEADS = DIL_HEADS_PER_GROUP * len(DIL_GROUPS)
DIL_QKV = DIL_N_HEADS * HEAD_DIM
DIL_WIDTH = DIL_HEADS_PER_GROUP * HEAD_DIM
DIL_SPAN = 128
DIFF_N_HEADS = 6
DIFF_HEAD_DIM = 2 * HEAD_DIM
DIFF_WIDTH = DIFF_N_HEADS * DIFF_HEAD_DIM
EPS = 1e-6
NEG_INF = -1e30
IN_SIZES = (DIL_QKV, DIL_QKV, DIL_QKV, DIL_WIDTH,
            DIFF_WIDTH, DIFF_WIDTH, DIFF_WIDTH, DIFF_WIDTH, D_MODEL, D_MODEL)
N_IN = sum(IN_SIZES)
(OFF_QA, OFF_KA, OFF_VA, OFF_ZA, OFF_QB, OFF_KB, OFF_VB, OFF_ZB, OFF_GLA, OFF_GLB) = (
    int(v) for v in np.concatenate([[0], np.cumsum(IN_SIZES)[:-1]]))
LOG2E = 1.4426950408889634
Q_PRESCALE = HEAD_DIM ** -0.5 * LOG2E
LAMBDA_INIT = 0.8 - 0.6 * math.exp(-0.3 * 0)

VMEM_LIMIT_BYTES = 56 * 1024 * 1024

ADA_TN = 512
PROJ_TM = 1024
PROJ_TN = 2048
PROJ_NORM_CHUNK = 32
PROJ_NORM_ROWS = 128
DIL_TT = DIL_SPAN * DIL_GROUPS[-1][1]
DIL_UNITS = DIL_TT // DIL_SPAN
DIL_WIDE_R = DIL_GROUPS[-1][1]
DIL_WIDE_PITCH = 24
DIL_MERGE_CHUNK = 64
DIFF_T = 1024
DIFF_SUB = 128
DIFF_DIAG_W = 256
DIFF_ONES_ROWS = 16
OUT_TM = 256
OUT_ZB_W = 512


def _alibi_slopes(n):
    return [float(2.0 ** (-8.0 * (i + 1) / n)) for i in range(n)]


def _dot_nt(a, b):
    return lax.dot_general(a, b, (((1,), (1,)), ((), ())), preferred_element_type=F32)


def _silu(z):
    return z / (1.0 + jnp.exp(-z))


def _sigmoid(z):
    return 1.0 / (1.0 + jnp.exp(-z))


def _select_by_index(idx, values):
    out = jnp.float32(values[0])
    for i in range(1, len(values)):
        out = jnp.where(idx == i, jnp.float32(values[i]), out)
    return out


def _ada_kernel(c_ref, w_ref, b_ref, o_ref):
    c = c_ref[...]
    o_ref[...] = jnp.dot(_silu(c), w_ref[...], preferred_element_type=F32,
                         precision=lax.Precision.HIGHEST) + b_ref[...]


def _ada(c_pad, w_ada, b_ada):
    rows, d = c_pad.shape
    n = w_ada.shape[1]
    tn = ADA_TN
    return pl.pallas_call(
        _ada_kernel,
        grid=(n // tn,),
        in_specs=[pl.BlockSpec((rows, d), lambda j: (0, 0)),
                  pl.BlockSpec((d, tn), lambda j: (0, j)),
                  pl.BlockSpec((1, tn), lambda j: (0, j))],
        out_specs=pl.BlockSpec((rows, tn), lambda j: (0, j)),
        out_shape=jax.ShapeDtypeStruct((rows, n), F32),
        name="ada",
    )(c_pad, w_ada, b_ada)


def _proj_kernel(x_ref, mod_ref, g_ref, w_ref, cs_ref, o_ref, hn_a, hn_b):
    i = pl.program_id(0)
    j = pl.program_id(1)
    tm = x_ref.shape[0]
    shift = mod_ref[0, 0:1, :]
    mult = g_ref[...] * (1.0 + mod_ref[0, 1:2, :])

    def normalize(dst_ref, r0, rows):
        xx = x_ref[pl.ds(r0, rows), :]
        ms = jnp.mean(xx * xx, axis=-1, keepdims=True)
        dst_ref[pl.ds(r0, rows), :] = (xx * lax.rsqrt(ms + EPS) * mult + shift).astype(BF16)

    @pl.when(jnp.logical_and(i == 0, j == 0))
    def _():
        def body(c, carry):
            normalize(hn_a, pl.multiple_of(c * PROJ_NORM_CHUNK, PROJ_NORM_CHUNK), PROJ_NORM_CHUNK)
            return carry

        lax.fori_loop(0, tm // PROJ_NORM_CHUNK, body, 0)

    r0 = pl.multiple_of(jnp.clip((j - 1) * PROJ_NORM_ROWS, 0, tm - PROJ_NORM_ROWS), PROJ_NORM_ROWS)

    def step(cur_ref, nxt_ref):
        normalize(nxt_ref, r0, PROJ_NORM_ROWS)
        acc = jnp.dot(cur_ref[...], w_ref[...], preferred_element_type=F32)
        o_ref[...] = (acc * cs_ref[...]).astype(BF16)

    @pl.when(i % 2 == 0)
    def _():
        step(hn_a, hn_b)

    @pl.when(i % 2 == 1)
    def _():
        step(hn_b, hn_a)


def _proj(x2d, mod, g_norm, w_in_bf16, col_scale, seq):
    t, d = x2d.shape
    n = w_in_bf16.shape[1]
    tm, tn = PROJ_TM, PROJ_TN
    n_i, n_j = t // tm, n // tn
    tiles_per_batch = seq // tm
    assert (n_j - 1) * PROJ_NORM_ROWS >= tm and tm % PROJ_NORM_ROWS == 0

    def next_tile(i, j):
        return jnp.where(jnp.logical_and(i == 0, j == 0), 0, jnp.minimum(i + 1, n_i - 1))

    return pl.pallas_call(
        _proj_kernel,
        grid=(n_i, n_j),
        in_specs=[pl.BlockSpec((tm, d), lambda i, j: (next_tile(i, j), 0)),
                  pl.BlockSpec((1, 3, d), lambda i, j: (next_tile(i, j) // tiles_per_batch, 0, 0)),
                  pl.BlockSpec((1, d), lambda i, j: (0, 0)),
                  pl.BlockSpec((d, tn), lambda i, j: (0, j)),
                  pl.BlockSpec((1, tn), lambda i, j: (0, j))],
        out_specs=pl.BlockSpec((tm, tn), lambda i, j: (i, j)),
        out_shape=jax.ShapeDtypeStruct((t, n), BF16),
        scratch_shapes=[pltpu.VMEM((tm, d), BF16), pltpu.VMEM((tm, d), BF16)],
        compiler_params=pltpu.CompilerParams(
            dimension_semantics=("arbitrary", "arbitrary"),
            allow_input_fusion=[False, False, False, True, False],
            vmem_limit_bytes=VMEM_LIMIT_BYTES),
        name="proj",
    )(x2d, mod, g_norm, w_in_bf16, col_scale)


def _gather_residues(src_ref, stage, dst_ref, r, n_u, dst_pitch, dst_off):
    pitch = DIL_WIDE_PITCH if r == DIL_WIDE_R else r
    for c0 in range(0, n_u * r, DIL_SPAN):
        x = src_ref[c0:c0 + DIL_SPAN, :].astype(F32)
        if pitch == r:
            stage[c0:c0 + DIL_SPAN, :] = x
        else:
            for k in range(DIL_SPAN // r):
                row = (c0 // r + k) * pitch
                stage[row:row + r, :] = x[k * r:(k + 1) * r, :]
    for p in range(r):
        for c0 in range(0, n_u, DIL_SPAN):
            row = p * dst_pitch + dst_off + c0
            dst_ref[row:row + DIL_SPAN, :] = stage[
                pl.ds(p + pitch * c0, DIL_SPAN, stride=pitch), :].astype(BF16)


def _dil_kernel(*refs):
    ins, y_ref = refs[:10], refs[10]
    qd, stage, bm_ref = refs[11:14]
    kd, vd = refs[14:17], refs[17:20]
    o_scr, lse_scr = refs[20:23], refs[23:26]
    z_ref = ins[9]
    h = pl.program_id(1)
    first_tile = pl.program_id(2) == 0

    iq = lax.broadcasted_iota(jnp.int32, (DIL_SPAN, 2 * DIL_SPAN), 0)
    jk = lax.broadcasted_iota(jnp.int32, (DIL_SPAN, 2 * DIL_SPAN), 1)
    dist = (iq - jk + DIL_SPAN).astype(F32)
    valid = jnp.logical_and(jk >= iq, jk <= iq + DIL_SPAN)
    valid_first = jnp.logical_and(valid, jk >= DIL_SPAN)

    slopes = _alibi_slopes(DIL_N_HEADS)
    for g, (_, r) in enumerate(DIL_GROUPS):
        q_ref, kc_ref, vc_ref = ins[3 * g:3 * g + 3]
        kd_g, vd_g = kd[g], vd[g]
        nb = DIL_UNITS // r
        kpitch = (nb + 1) * DIL_SPAN
        coef = _select_by_index(
            h, [slopes[g * DIL_HEADS_PER_GROUP + i] * r * LOG2E for i in range(DIL_HEADS_PER_GROUP)])
        bias = -coef * dist
        bm_ref[0] = jnp.where(valid, bias, NEG_INF)
        bm_ref[1] = jnp.where(valid_first, bias, NEG_INF)

        @pl.when(first_tile)
        def _(r=r, kpitch=kpitch, kd_g=kd_g, vd_g=vd_g):
            for p in range(r):
                kd_g[p * kpitch:p * kpitch + DIL_SPAN, :] = jnp.zeros((DIL_SPAN, LANES), BF16)
                vd_g[p * kpitch:p * kpitch + DIL_SPAN, :] = jnp.zeros((DIL_SPAN, LANES), BF16)

        if r == 1:
            q_src = q_ref
            kd_g[DIL_SPAN:DIL_SPAN + DIL_TT, :] = kc_ref[...]
            vd_g[DIL_SPAN:DIL_SPAN + DIL_TT, :] = vc_ref[...]
        else:
            q_src = qd
            _gather_residues(q_ref, stage, qd, r, nb * DIL_SPAN, nb * DIL_SPAN, 0)
            _gather_residues(kc_ref, stage, kd_g, r, nb * DIL_SPAN, kpitch, DIL_SPAN)
            _gather_residues(vc_ref, stage, vd_g, r, nb * DIL_SPAN, kpitch, DIL_SPAN)

        for p in range(r):
            for ub in range(nb):
                rq = (p * nb + ub) * DIL_SPAN
                rk = p * kpitch + ub * DIL_SPAN
                bm = bm_ref[jnp.where(first_tile, 1, 0)] if ub == 0 else bm_ref[0]
                s = _dot_nt(q_src[rq:rq + DIL_SPAN, :], kd_g[rk:rk + 2 * DIL_SPAN, :]) + bm
                m = jnp.max(s, axis=-1, keepdims=True)
                e = jnp.exp2(s - m)
                l = jnp.sum(e, axis=-1, keepdims=True)
                o = jnp.dot(e.astype(BF16), vd_g[rk:rk + 2 * DIL_SPAN, :],
                            preferred_element_type=F32) * (1.0 / l)
                lse2 = jnp.broadcast_to(m + jnp.log(l) * LOG2E, (DIL_SPAN, LANES))
                if r == DIL_WIDE_R:
                    rows = pl.ds(p, DIL_SPAN, stride=DIL_WIDE_PITCH)
                elif r > 1:
                    rows = pl.ds(ub * (DIL_SPAN * r) + p, DIL_SPAN, stride=r)
                else:
                    rows = pl.ds(ub * DIL_SPAN, DIL_SPAN)
                o_scr[g][rows, :] = o
                lse_scr[g][rows, :] = lse2

        for p in range(r):
            last = p * kpitch + nb * DIL_SPAN
            kd_g[p * kpitch:p * kpitch + DIL_SPAN, :] = kd_g[last:last + DIL_SPAN, :]
            vd_g[p * kpitch:p * kpitch + DIL_SPAN, :] = vd_g[last:last + DIL_SPAN, :]

    def merge(i, carry):
        rows = pl.ds(pl.multiple_of(i * DIL_MERGE_CHUNK, DIL_MERGE_CHUNK), DIL_MERGE_CHUNK)

        def wide(ref):
            per = DIL_MERGE_CHUNK // DIL_WIDE_R
            return jnp.concatenate(
                [ref[pl.ds(pl.multiple_of((i * per + k) * DIL_WIDE_PITCH, SUBLANES), DIL_WIDE_R), :]
                 for k in range(per)], axis=0)

        l0, l1, l2 = lse_scr[0][rows, :], lse_scr[1][rows, :], wide(lse_scr[2])
        mx = jnp.maximum(jnp.maximum(l0, l1), l2)
        e0, e1, e2 = jnp.exp2(l0 - mx), jnp.exp2(l1 - mx), jnp.exp2(l2 - mx)
        om = (e0 * o_scr[0][rows, :] + e1 * o_scr[1][rows, :] + e2 * wide(o_scr[2])) / (e0 + e1 + e2)
        y_ref[rows, :] = (om * _silu(z_ref[rows, :].astype(F32))).astype(BF16)
        return carry

    lax.fori_loop(0, DIL_TT // DIL_MERGE_CHUNK, merge, 0, unroll=2)


def _dil(proj2d, batch, seq):
    tt = DIL_TT
    tiles = seq // tt

    def cur(col):
        return pl.BlockSpec((tt, LANES), lambda b, h, t: (b * tiles + t, col + h))

    in_specs = []
    for g in range(len(DIL_GROUPS)):
        in_specs += [cur((off + g * DIL_WIDTH) // LANES) for off in (OFF_QA, OFF_KA, OFF_VA)]
    in_specs.append(cur(OFF_ZA // LANES))
    kv_scratch = [pltpu.VMEM((tt + DIL_SPAN * r, LANES), BF16) for _, r in DIL_GROUPS]
    return pl.pallas_call(
        _dil_kernel,
        grid=(batch, DIL_HEADS_PER_GROUP, tiles),
        in_specs=in_specs,
        out_specs=pl.BlockSpec((tt, LANES), lambda b, h, t: (b * tiles + t, h)),
        out_shape=jax.ShapeDtypeStruct((batch * seq, DIL_WIDTH), BF16),
        scratch_shapes=(
            [pltpu.VMEM((tt, LANES), BF16), pltpu.VMEM((DIL_SPAN * DIL_WIDE_PITCH, LANES), F32),
             pltpu.VMEM((2, DIL_SPAN, 2 * DIL_SPAN), F32)]
            + kv_scratch + kv_scratch
            + [pltpu.VMEM((tt, LANES), F32)] * 2 + [pltpu.VMEM((DIL_SPAN * DIL_WIDE_PITCH, LANES), F32)]
            + [pltpu.VMEM((tt, LANES), F32)] * 2 + [pltpu.VMEM((DIL_SPAN * DIL_WIDE_PITCH, LANES), F32)]),
        compiler_params=pltpu.CompilerParams(
            dimension_semantics=("parallel", "parallel", "arbitrary"),
            vmem_limit_bytes=VMEM_LIMIT_BYTES),
        name="dil",
    )(*([proj2d] * 10))


def _diff_kernel(q_ref, k_ref, v_ref, lamv_ref, gsub_ref, o_ref,
                 vt_ref, st_ref, pt_ref, cb_ref, m_ref, a_ref, acc_ref, *, slopes2):
    h = pl.program_id(1)
    qi = pl.program_id(2)
    t = q_ref.shape[0]
    nk = vt_ref.shape[0]
    slope2 = _select_by_index(h, slopes2)

    @pl.when(qi == 0)
    def _():
        for j in range(nk):
            vt_ref[j, 0:DIFF_HEAD_DIM, :] = v_ref[j * t:(j + 1) * t, :].astype(F32).T.astype(BF16)
            vt_ref[j, DIFF_HEAD_DIM:, :] = jnp.ones((DIFF_ONES_ROWS, t), BF16)
        cb_ref[...] = slope2 * lax.broadcasted_iota(jnp.int32, (t, LANES), 0).astype(F32)

    def block(kj):
        k0 = pl.multiple_of(kj * t, t)
        cblk = slope2 * ((kj - qi) * t).astype(F32)
        for c in range(2):
            sl = slice(c * HEAD_DIM, (c + 1) * HEAD_DIM)
            st_ref[c] = _dot_nt(k_ref[pl.ds(k0, t), sl], q_ref[:, sl])
        for c in range(2):
            for cg in range(t // LANES):
                cols = slice(cg * LANES, (cg + 1) * LANES)
                m_old = m_ref[c, :, cols]
                m_run = m_old
                m_sub = []
                for r0 in range(0, t, DIFF_SUB):
                    rows = slice(r0, r0 + DIFF_SUB)
                    x = st_ref[c, rows, cols] + cb_ref[rows, :]
                    m_run = jnp.maximum(m_run, jnp.max(x, axis=0, keepdims=True) + cblk)
                    pt_ref[c, rows, cols] = jnp.exp2((x - (m_run - cblk)).astype(BF16))
                    m_sub.append(m_run)
                for i, r0 in enumerate(range(0, t - DIFF_SUB, DIFF_SUB)):
                    rows = slice(r0, r0 + DIFF_SUB)
                    pt_ref[c, rows, cols] = (pt_ref[c, rows, cols]
                                             * jnp.exp2(m_sub[i] - m_run).astype(BF16))
                m_ref[c, :, cols] = m_run
                a_ref[c, :, cols] = jnp.exp2(m_old - m_run)
        vt = vt_ref[kj]
        for c in range(2):
            acc_ref[c] = acc_ref[c] * a_ref[c] + jnp.dot(vt, pt_ref[c], preferred_element_type=F32)

    def diagonal_block():
        k0 = pl.multiple_of(qi * t, t)
        w = DIFF_DIAG_W
        for c in range(2):
            sl = slice(c * HEAD_DIM, (c + 1) * HEAD_DIM)
            for qh in range(t // w):
                kend = (qh + 1) * w
                qs = slice(qh * w, kend)
                st_ref[c, 0:kend, qs] = _dot_nt(k_ref[pl.ds(k0, kend), sl], q_ref[qs, sl])
        for c in range(2):
            for cg in range(t // LANES):
                cols = slice(cg * LANES, (cg + 1) * LANES)
                kend = (cg * LANES // w + 1) * w
                key = lax.broadcasted_iota(jnp.int32, (kend, LANES), 0)
                qry = lax.broadcasted_iota(jnp.int32, (kend, LANES), 1) + cg * LANES
                x = jnp.where(key <= qry, st_ref[c, 0:kend, cols] + cb_ref[0:kend, :], NEG_INF)
                m_new = jnp.max(x, axis=0, keepdims=True)
                m_ref[c, :, cols] = m_new
                pt_ref[c, 0:kend, cols] = jnp.exp2((x - m_new).astype(BF16))
        vt = vt_ref[qi]
        for c in range(2):
            for qh in range(t // w):
                kend = (qh + 1) * w
                qs = slice(qh * w, kend)
                acc_ref[c, :, qs] = jnp.dot(vt[:, 0:kend], pt_ref[c, 0:kend, qs],
                                            preferred_element_type=F32)

    def body(kj, carry):
        block(kj)
        return carry

    diagonal_block()
    lax.fori_loop(0, qi, body, 0)

    lamv = lamv_ref[...]
    lam = (jnp.exp(jnp.sum(lamv[0:1] * lamv[1:2], axis=-1, keepdims=True))
           - jnp.exp(jnp.sum(lamv[2:3] * lamv[3:4], axis=-1, keepdims=True)) + LAMBDA_INIT)
    d = DIFF_HEAD_DIM
    ot = (acc_ref[0, 0:d, :] * (1.0 / acc_ref[0, d:d + 1, :])
          - lam * (acc_ref[1, 0:d, :] * (1.0 / acc_ref[1, d:d + 1, :])))
    ms = jnp.mean(ot * ot, axis=0, keepdims=True)
    ot = ot * lax.rsqrt(ms + EPS)
    o_ref[...] = (ot.T * gsub_ref[...] * (1.0 - LAMBDA_INIT)).astype(o_ref.dtype)


def _diff(proj2d, lamv, g_subln, batch, seq):
    t = DIFF_T
    nq = seq // t
    wq, wk, wv = OFF_QB // DIFF_HEAD_DIM, OFF_KB // DIFF_HEAD_DIM, OFF_VB // DIFF_HEAD_DIM
    slopes2 = tuple(s * LOG2E for s in _alibi_slopes(DIFF_N_HEADS))
    vrows = DIFF_HEAD_DIM + DIFF_ONES_ROWS
    return pl.pallas_call(
        functools.partial(_diff_kernel, slopes2=slopes2),
        grid=(batch, DIFF_N_HEADS, nq),
        in_specs=[
            pl.BlockSpec((t, DIFF_HEAD_DIM), lambda b, h, q: (b * nq + q, wq + h)),
            pl.BlockSpec((seq, DIFF_HEAD_DIM), lambda b, h, q: (b, wk + h)),
            pl.BlockSpec((seq, DIFF_HEAD_DIM), lambda b, h, q: (b, wv + h)),
            pl.BlockSpec((4, HEAD_DIM), lambda b, h, q: (0, 0)),
            pl.BlockSpec((1, DIFF_HEAD_DIM), lambda b, h, q: (0, 0)),
        ],
        out_specs=pl.BlockSpec((t, DIFF_HEAD_DIM), lambda b, h, q: (b * nq + q, h)),
        out_shape=jax.ShapeDtypeStruct((batch * seq, DIFF_WIDTH), BF16),
        scratch_shapes=[
            pltpu.VMEM((nq, vrows, t), BF16),
            pltpu.VMEM((2, t, t), F32),
            pltpu.VMEM((2, t, t), BF16),
            pltpu.VMEM((t, LANES), F32),
            pltpu.VMEM((2, 1, t), F32),
            pltpu.VMEM((2, 1, t), F32),
            pltpu.VMEM((2, vrows, t), F32),
        ],
        compiler_params=pltpu.CompilerParams(
            dimension_semantics=("parallel", "parallel", "arbitrary"),
            vmem_limit_bytes=VMEM_LIMIT_BYTES),
        name="diff",
    )(proj2d, proj2d, proj2d, lamv, g_subln)


def _out_kernel(ya_ref, yb_ref, zb0_ref, zb1_ref, zb2_ref, gla_ref, glb_ref, x_ref, mod_ref,
                gf_ref, wd_ref, wf_ref, wo_ref, out_ref, ybz_ref):
    zb_refs = (zb0_ref, zb1_ref, zb2_ref)
    wz = zb0_ref.shape[1]
    for i in range(3):
        sl = slice(i * wz, (i + 1) * wz)
        ybz_ref[:, sl] = (yb_ref[:, sl].astype(F32) * _silu(zb_refs[i][...].astype(F32))).astype(BF16)
    pa = jnp.dot(ya_ref[...], wd_ref[...], preferred_element_type=F32)
    pb = jnp.dot(ybz_ref[...], wf_ref[...], preferred_element_type=F32)
    merged = (_sigmoid(gla_ref[...].astype(F32)) * pa + _sigmoid(glb_ref[...].astype(F32)) * pb)
    delta = jnp.dot(merged.astype(BF16), wo_ref[...], preferred_element_type=F32)
    y = x_ref[...] + mod_ref[0, 2:3, :] * delta
    ms = jnp.mean(y * y, axis=-1, keepdims=True)
    out_ref[...] = y * lax.rsqrt(ms + EPS) * gf_ref[...]


def _const_spec(shape):
    return pl.BlockSpec(shape, lambda i: (0,) * len(shape), pipeline_mode=pl.Buffered(1))


def _out(ya, yb, proj2d, x2d, mod, g_final, wd, wf, wo, seq):
    t, d = x2d.shape
    tm = OUT_TM
    tiles_per_batch = seq // tm
    wz = OUT_ZB_W
    assert OFF_ZB % wz == 0 and DIFF_WIDTH == 3 * wz and seq % tm == 0
    zb_blk = OFF_ZB // wz
    row = lambda w: pl.BlockSpec((tm, w), lambda i: (i, 0))
    in_specs = (
        [row(DIL_WIDTH), row(DIFF_WIDTH)]
        + [pl.BlockSpec((tm, wz), functools.partial(lambda i, k: (i, zb_blk + k), k=k))
           for k in range(3)]
        + [pl.BlockSpec((tm, d), lambda i: (i, OFF_GLA // D_MODEL)),
           pl.BlockSpec((tm, d), lambda i: (i, OFF_GLB // D_MODEL)),
           row(d),
           pl.BlockSpec((1, 3, d), lambda i: (i // tiles_per_batch, 0, 0)),
           _const_spec((1, d)),
           _const_spec(wd.shape), _const_spec(wf.shape), _const_spec(wo.shape)])
    return pl.pallas_call(
        _out_kernel,
        grid=(t // tm,),
        in_specs=in_specs,
        out_specs=row(d),
        out_shape=jax.ShapeDtypeStruct((t, d), F32),
        scratch_shapes=[pltpu.VMEM((tm, DIFF_WIDTH), BF16)],
        compiler_params=pltpu.CompilerParams(
            dimension_semantics=("parallel",),
            vmem_limit_bytes=VMEM_LIMIT_BYTES),
        name="out",
    )(ya, yb, proj2d, proj2d, proj2d, proj2d, proj2d, x2d, mod, g_final, wd, wf, wo)


def _q_column_scale():
    cs = np.ones((1, N_IN), np.float32)
    cs[:, OFF_QA:OFF_QA + DIL_QKV] = Q_PRESCALE
    cs[:, OFF_QB:OFF_QB + DIFF_WIDTH] = Q_PRESCALE
    return jnp.asarray(cs)


def kernel(x, c, w_ada, b_ada, g_norm, w_in, w_o_dil, w_o_diff, w_out,
           lambda_q1, lambda_k1, lambda_q2, lambda_k2, g_subln, g_final):
    batch, seq, d = x.shape
    assert w_ada.shape[0] == 1 and d == D_MODEL and w_in.shape[2] == N_IN
    assert seq % DIL_TT == 0 and seq % PROJ_TM == 0 and seq % DIFF_T == 0
    rows = -(-batch // SUBLANES) * SUBLANES
    c_pad = jnp.pad(c, ((0, rows - batch), (0, 0)))
    x2d = x.reshape(batch * seq, d)
    ada = _ada(c_pad, w_ada[0], b_ada[0].reshape(1, 3 * d))[:batch]
    mod = ada.reshape(batch, 3, d)
    proj2d = _proj(x2d, mod, g_norm[0].reshape(1, d), w_in[0].astype(BF16), _q_column_scale(), seq)
    ya = _dil(proj2d, batch, seq)
    lamv = jnp.stack([lambda_q1[0], lambda_k1[0], lambda_q2[0], lambda_k2[0]])
    yb = _diff(proj2d, lamv, g_subln[0].reshape(1, DIFF_HEAD_DIM), batch, seq)
    out = _out(ya, yb, proj2d, x2d, mod, g_final.reshape(1, d), w_o_dil[0].astype(BF16),
               w_o_diff[0].astype(BF16), w_out[0].astype(BF16), seq)
    return out.reshape(batch, seq, d)
```
